```python
import math
import jax, jax.numpy as jnp
from jax import lax
import numpy as np

D_MODEL = 1024
BATCH = 2
SEQ = 16384
DEPTH = 2
DEC_BATCH = 16
DEC_SEQ = 32
PAST_LEN = 4096

CHUNK = 64
POOL_WIDTH = 512
POOL_WINDOWS = (2, 4, 8, 16)
POOL_GROUP = POOL_WIDTH // 4
POOL_HIST = 16 - 1
SSM_WIDTH = 512
SSM_GROUP = 16
SSM_GROUPS = SSM_WIDTH // SSM_GROUP
SSM_STATE = 64
SSM_BLOCK = 128
CONV_WIDTH = 512
CONV_K = 3
N_HEADS = 8
QK_HEAD_DIM = 64
V_HEAD_DIM = 2 * QK_HEAD_DIM
QK_WIDTH = N_HEADS * 2 * QK_HEAD_DIM
ATTN_WIDTH = N_HEADS * V_HEAD_DIM
ROT_DIM = QK_HEAD_DIM // 4
ROPE_THETA = 500000.0
Q_BLOCK = 128
N_BRANCH = 4
BRANCH_WIDTH = POOL_WIDTH + SSM_WIDTH + CONV_WIDTH + ATTN_WIDTH
IN_WIDTH = POOL_WIDTH + SSM_WIDTH + 3 * CONV_WIDTH + 2 * QK_WIDTH + ATTN_WIDTH + N_BRANCH * D_MODEL
D_FF = 4 * D_MODEL
DEEPNORM_ALPHA = (2 * DEPTH) ** 0.25
DEEPNORM_BETA = (8 * DEPTH) ** -0.25
LN_EPS = 1e-5

kernel_name = "hybrid_streaming_encoder_step"

F32 = jnp.float32


def _offsets(widths):
    return [int(o) for o in np.cumsum(widths)[:-1]]


def layer_norm(x, g, b):
    xf = x.astype(F32)
    mu = jnp.mean(xf, -1, keepdims=True)
    var = jnp.mean(jnp.square(xf - mu), -1, keepdims=True)
    return ((xf - mu) * lax.rsqrt(var + LN_EPS) * g.astype(F32) + b.astype(F32)).astype(x.dtype)


def pool_mixer(z, hist, pos, pool_w, pool_scale):
    L = z.shape[1]
    zp = jnp.concatenate([hist.astype(z.dtype), z], axis=1)
    cs = jnp.pad(jnp.cumsum(zp.astype(F32), axis=1), ((0, 0), (1, 0), (0, 0)))
    outs = []
    for g, w in enumerate(POOL_WINDOWS):
        sl = slice(g * POOL_GROUP, (g + 1) * POOL_GROUP)
        hi = cs[:, POOL_HIST + 1:POOL_HIST + 1 + L, sl]
        lo = cs[:, POOL_HIST + 1 - w:POOL_HIST + 1 - w + L, sl]
        cnt = jnp.minimum(w, pos + 1).astype(F32)
        u = (hi - lo) / cnt[None, :, None] - z[..., sl].astype(F32)
        outs.append(jnp.einsum('blc,cd->bld', u, pool_w[g].astype(F32)))
    out = jnp.concatenate(outs, axis=-1) * pool_scale.astype(F32)
    return out.astype(z.dtype), zp[:, -POOL_HIST:]


def _cmul_scan(e1, e2):
    a1r, a1i, b1r, b1i = e1
    a2r, a2i, b2r, b2i = e2
    return (a2r * a1r - a2i * a1i, a2r * a1i + a2i * a1r,
            a2r * b1r - a2i * b1i + b2r, a2r * b1i + a2i * b1r + b2i)


def s5_mixer(u, h_re, h_im, a_re, a_im, log_dt, b_re, b_im, c_re, c_im, d_skip, w_glu):
    bt, L, _ = u.shape
    uf = u.astype(F32).reshape(bt, L, SSM_GROUPS, SSM_GROUP)
    a_re = a_re.astype(F32); a_im = a_im.astype(F32)
    dt = jnp.exp(log_dt.astype(F32))[:, None]
    mag = jnp.exp(a_re * dt)
    ab_re = mag * jnp.cos(a_im * dt); ab_im = mag * jnp.sin(a_im * dt)
    den = a_re * a_re + a_im * a_im
    cr = ((ab_re - 1.0) * a_re + ab_im * a_im) / den
    ci = (ab_im * a_re - (ab_re - 1.0) * a_im) / den
    b_re = b_re.astype(F32); b_im = b_im.astype(F32)
    bb_re = cr[..., None] * b_re - ci[..., None] * b_im
    bb_im = cr[..., None] * b_im + ci[..., None] * b_re
    cm_re = c_re.astype(F32); cm_im = c_im.astype(F32)
    blk = min(SSM_BLOCK, L)
    nb = L // blk
    ub = jnp.swapaxes(uf.reshape(bt, nb, blk, SSM_GROUPS, SSM_GROUP), 0, 1)

    def step(carry, u_blk):
        s_re, s_im = carry
        bu_re = jnp.einsum('blgn,gpn->blgp', u_blk, bb_re)
        bu_im = jnp.einsum('blgn,gpn->blgp', u_blk, bb_im)
        a_r = jnp.broadcast_to(ab_re, bu_re.shape)
        a_i = jnp.broadcast_to(ab_im, bu_im.shape)
        pa_re, pa_im, sb_re, sb_im = lax.associative_scan(_cmul_scan, (a_r, a_i, bu_re, bu_im), axis=1)
        st_re = sb_re + pa_re * s_re[:, None] - pa_im * s_im[:, None]
        st_im = sb_im + pa_re * s_im[:, None] + pa_im * s_re[:, None]
        y = jnp.einsum('blgp,gnp->blgn', st_re, cm_re) - jnp.einsum('blgp,gnp->blgn', st_im, cm_im)
        return (st_re[:, -1], st_im[:, -1]), y

    (n_re, n_im), yb = lax.scan(step, (h_re.astype(F32), h_im.astype(F32)), ub)
    y = jnp.swapaxes(yb, 0, 1).reshape(bt, L, SSM_WIDTH) + d_skip.astype(F32) * uf.reshape(bt, L, SSM_WIDTH)
    v = jax.nn.gelu(y).astype(u.dtype)
    out = v * jax.nn.sigmoid((v @ w_glu).astype(F32)).astype(u.dtype)
    return out, n_re, n_im


def short_conv_mixer(h, b, c, hist, conv_w, conv_b):
    L = h.shape[1]
    z = c * h
    zp = jnp.concatenate([hist.astype(z.dtype), z], axis=1)
    y = conv_b + zp[:, 0:L] * conv_w[0]
    for i in range(1, CONV_K):
        y = y + zp[:, i:i + L] * conv_w[i]
    return b * y, zp[:, -(CONV_K - 1):]


def partial_rope(x, pos):
    half = ROT_DIM // 2
    inv = ROPE_THETA ** (-jnp.arange(0, ROT_DIM, 2, dtype=F32) / ROT_DIM)
    ang = pos.astype(F32)[:, None] * inv[None, :]
    cos = jnp.cos(ang)[None, :, None, None, :]
    sin = jnp.sin(ang)[None, :, None, None, :]
    xr = x[..., :ROT_DIM].astype(F32)
    x1, x2 = xr[..., :half], xr[..., half:]
    rot = jnp.concatenate([x1 * cos - x2 * sin, x2 * cos + x1 * sin], axis=-1).astype(x.dtype)
    return jnp.concatenate([rot, x[..., ROT_DIM:]], axis=-1)


def diff_attn_core(q, k, v, q_pos, k_pos, lam):
    s = jnp.einsum('bqhcd,bkhcd->bchqk', q, k, preferred_element_type=F32) * (QK_HEAD_DIM ** -0.5)
    visible = k_pos[None, :] < ((q_pos // CHUNK + 1) * CHUNK)[:, None]
    p = jax.nn.softmax(jnp.where(visible, s, -jnp.inf), axis=-1)
    a = p[:, 0] - lam * p[:, 1]
    return jnp.einsum('bhqk,bkhe->bqhe', a.astype(v.dtype), v)


def blocked_diff_attention(q, k, v, q_pos, k_pos, lam):
    bt, L = q.shape[0], q.shape[1]
    if L <= Q_BLOCK or L % Q_BLOCK:
        return diff_attn_core(q, k, v, q_pos, k_pos, lam)
    nb = L // Q_BLOCK
    qb = jnp.swapaxes(q.reshape(bt, nb, Q_BLOCK, N_HEADS, 2, QK_HEAD_DIM), 0, 1)
    pb = q_pos.reshape(nb, Q_BLOCK)
    ob = lax.map(lambda qp: diff_attn_core(qp[0], k, v, qp[1], k_pos, lam), (qb, pb))
    return jnp.swapaxes(ob, 0, 1).reshape(bt, L, N_HEADS, V_HEAD_DIM)


def trunk_layer(x, pos0, hist_pool, hist_conv, h_re, h_im, past_k, past_v, p, layer_idx):
    bt, L, _ = x.shape
    q_pos = pos0 + jnp.arange(L, dtype=jnp.int32)
    proj = x @ p['w_in']
    z_pool, u_ssm, h_conv, b_conv, c_conv, q, k, v, g = jnp.split(
        proj, _offsets([POOL_WIDTH, SSM_WIDTH, CONV_WIDTH, CONV_WIDTH, CONV_WIDTH,
                        QK_WIDTH, QK_WIDTH, ATTN_WIDTH, N_BRANCH * D_MODEL]), axis=-1)
    o_pool, new_pool = pool_mixer(z_pool, hist_pool, q_pos, p['pool_w'], p['pool_scale'])
    o_ssm, new_re, new_im = s5_mixer(u_ssm, h_re, h_im, p['ssm_a_re'], p['ssm_a_im'], p['ssm_log_dt'],
                                     p['ssm_b_re'], p['ssm_b_im'], p['ssm_c_re'], p['ssm_c_im'],
                                     p['ssm_d'], p['ssm_w_glu'])
    o_conv, new_conv = short_conv_mixer(h_conv, b_conv, c_conv, hist_conv, p['conv_w'], p['conv_b'])
    q = partial_rope(q.reshape(bt, L, N_HEADS, 2, QK_HEAD_DIM), q_pos)
    k = partial_rope(k.reshape(bt, L, N_HEADS, 2, QK_HEAD_DIM), q_pos)
    v = v.reshape(bt, L, N_HEADS, V_HEAD_DIM)
    if past_k is None:
        k_all, v_all = k, v
    else:
        k_all = jnp.concatenate([past_k.astype(k.dtype), k], axis=1)
        v_all = jnp.concatenate([past_v.astype(v.dtype), v], axis=1)
    k_pos = jnp.arange(k_all.shape[1], dtype=jnp.int32)
    lam_init = 0.8 - 0.6 * math.exp(-0.3 * layer_idx)
    lam = (jnp.exp(jnp.sum(p['lambda_q1'].astype(F32) * p['lambda_k1'].astype(F32)))
           - jnp.exp(jnp.sum(p['lambda_q2'].astype(F32) * p['lambda_k2'].astype(F32))) + lam_init)
    o = blocked_diff_attention(q, k_all, v_all, q_pos, k_pos, lam).astype(F32)
    o = o * lax.rsqrt(jnp.mean(o * o, -1, keepdims=True) + LN_EPS) * p['subln_w'].astype(F32) * (1.0 - lam_init)
    o_attn = o.reshape(bt, L, ATTN_WIDTH).astype(x.dtype)
    gates = jax.nn.sigmoid(g.astype(F32)).astype(x.dtype).reshape(bt, L, N_BRANCH, D_MODEL)
    wp, ws, wc, wa = jnp.split(p['w_branch'], _offsets([POOL_WIDTH, SSM_WIDTH, CONV_WIDTH, ATTN_WIDTH]), axis=0)
    merged = (gates[:, :, 0] * (o_pool @ wp) + gates[:, :, 1] * (o_ssm @ ws)
              + gates[:, :, 2] * (o_conv @ wc) + gates[:, :, 3] * (o_attn @ wa))
    x = layer_norm(DEEPNORM_ALPHA * x + merged @ p['w_out'], p['ln1_g'], p['ln1_b'])
    hid = jnp.square(jax.nn.relu(x @ p['w_up']))
    x = layer_norm(DEEPNORM_ALPHA * x + hid @ p['w_down'], p['ln2_g'], p['ln2_b'])
    return x, (k, v, new_re, new_im, new_conv, new_pool)


def setup_inputs(seed: int = 0) -> dict:
    key = jax.random.key(seed)
    ks = jax.random.split(key, 40)
    nrm = lambda i, shape, s=1.0: jax.random.normal(ks[i], shape, F32) * s
    branch_scale = jnp.concatenate([jnp.full((POOL_WIDTH + SSM_WIDTH + CONV_WIDTH,), (512.0) ** -0.5, F32),
                                    jnp.full((ATTN_WIDTH,), float(ATTN_WIDTH) ** -0.5, F32)])
    return {
        'x_prompt': nrm(0, (BATCH, SEQ, D_MODEL)),
        'x_sample': nrm(1, (DEC_BATCH, DEC_SEQ, D_MODEL)),
        'cache_k': nrm(2, (DEPTH, DEC_BATCH, PAST_LEN, N_HEADS, 2, QK_HEAD_DIM)),
        'cache_v': nrm(3, (DEPTH, DEC_BATCH, PAST_LEN, N_HEADS, V_HEAD_DIM)),
        'state_ssm_re': nrm(4, (DEPTH, DEC_BATCH, SSM_GROUPS, SSM_STATE), 0.1),
        'state_ssm_im': nrm(5, (DEPTH, DEC_BATCH, SSM_GROUPS, SSM_STATE), 0.1),
        'state_conv': nrm(6, (DEPTH, DEC_BATCH, CONV_K - 1, CONV_WIDTH)),
        'state_pool': nrm(7, (DEPTH, DEC_BATCH, POOL_HIST, POOL_WIDTH)),
        'w_in': nrm(8, (DEPTH, D_MODEL, IN_WIDTH), D_MODEL ** -0.5),
        'pool_w': nrm(9, (DEPTH, 4, POOL_GROUP, POOL_GROUP), POOL_GROUP ** -0.5),
        'pool_scale': 1.0 + nrm(10, (DEPTH, POOL_WIDTH), 0.02),
        'ssm_a_re': -0.5 + nrm(11, (DEPTH, SSM_GROUPS, SSM_STATE), 0.01),
        'ssm_a_im': math.pi * jnp.arange(SSM_STATE, dtype=F32) + nrm(12, (DEPTH, SSM_GROUPS, SSM_STATE), 0.01),
        'ssm_log_dt': jax.random.uniform(ks[13], (DEPTH, SSM_GROUPS), F32, math.log(1e-3), math.log(1e-1)),
        'ssm_b_re': nrm(14, (DEPTH, SSM_GROUPS, SSM_STATE, SSM_GROUP), (2 * SSM_GROUP) ** -0.5),
        'ssm_b_im': nrm(15, (DEPTH, SSM_GROUPS, SSM_STATE, SSM_GROUP), (2 * SSM_GROUP) ** -0.5),
        'ssm_c_re': nrm(16, (DEPTH, SSM_GROUPS, SSM_GROUP, SSM_STATE), SSM_STATE ** -0.5),
        'ssm_c_im': nrm(17, (DEPTH, SSM_GROUPS, SSM_GROUP, SSM_STATE), SSM_STATE ** -0.5),
        'ssm_d': nrm(18, (DEPTH, SSM_WIDTH)),
        'ssm_w_glu': nrm(19, (DEPTH, SSM_WIDTH, SSM_WIDTH), SSM_WIDTH ** -0.5),
        'conv_w': nrm(20, (DEPTH, CONV_K, CONV_WIDTH), CONV_K ** -0.5),
        'conv_b': nrm(21, (DEPTH, CONV_WIDTH), 0.01),
        'lambda_q1': nrm(22, (DEPTH, QK_HEAD_DIM), 0.1),
        'lambda_k1': nrm(23, (DEPTH, QK_HEAD_DIM), 0.1),
        'lambda_q2': nrm(24, (DEPTH, QK_HEAD_DIM), 0.1),
        'lambda_k2': nrm(25, (DEPTH, QK_HEAD_DIM), 0.1),
        'subln_w': 1.0 + nrm(26, (DEPTH, V_HEAD_DIM), 0.02),
        'w_branch': nrm(27, (DEPTH, BRANCH_WIDTH, D_MODEL)) * branch_scale[None, :, None],
        'w_out': nrm(28, (DEPTH, D_MODEL, D_MODEL), D_MODEL ** -0.5 * DEEPNORM_BETA),
        'ln1_g': 1.0 + nrm(29, (DEPTH, D_MODEL), 0.02),
        'ln1_b': nrm(30, (DEPTH, D_MODEL), 0.02),
        'w_up': nrm(31, (DEPTH, D_MODEL, D_FF), D_MODEL ** -0.5),
        'w_down': nrm(32, (DEPTH, D_FF, D_MODEL), D_FF ** -0.5 * DEEPNORM_BETA),
        'ln2_g': 1.0 + nrm(33, (DEPTH, D_MODEL), 0.02),
        'ln2_b': nrm(34, (DEPTH, D_MODEL), 0.02),
    }


def reference(x_prompt, x_sample, cache_k, cache_v, state_ssm_re, state_ssm_im, state_conv, state_pool,
              w_in, pool_w, pool_scale, ssm_a_re, ssm_a_im, ssm_log_dt, ssm_b_re, ssm_b_im, ssm_c_re, ssm_c_im,
              ssm_d, ssm_w_glu, conv_w, conv_b, lambda_q1, lambda_k1, lambda_q2, lambda_k2, subln_w,
              w_branch, w_out, ln1_g, ln1_b, w_up, w_down, ln2_g, ln2_b):
    def layer_params(l):
        return dict(w_in=w_in[l], pool_w=pool_w[l], pool_scale=pool_scale[l],
                    ssm_a_re=ssm_a_re[l], ssm_a_im=ssm_a_im[l], ssm_log_dt=ssm_log_dt[l],
                    ssm_b_re=ssm_b_re[l], ssm_b_im=ssm_b_im[l], ssm_c_re=ssm_c_re[l], ssm_c_im=ssm_c_im[l],
                    ssm_d=ssm_d[l], ssm_w_glu=ssm_w_glu[l], conv_w=conv_w[l], conv_b=conv_b[l],
                    lambda_q1=lambda_q1[l], lambda_k1=lambda_k1[l], lambda_q2=lambda_q2[l], lambda_k2=lambda_k2[l],
                    subln_w=subln_w[l], w_branch=w_branch[l], w_out=w_out[l], ln1_g=ln1_g[l], ln1_b=ln1_b[l],
                    w_up=w_up[l], w_down=w_down[l], ln2_g=ln2_g[l], ln2_b=ln2_b[l])

    bp = x_prompt.shape[0]
    zero_pool = jnp.zeros((bp, POOL_HIST, POOL_WIDTH), x_prompt.dtype)
    zero_conv = jnp.zeros((bp, CONV_K - 1, CONV_WIDTH), x_prompt.dtype)
    zero_ssm = jnp.zeros((bp, SSM_GROUPS, SSM_STATE), F32)
    xp = x_prompt
    pk, pv, pre, pim, pconv, ppool = [], [], [], [], [], []
    for l in range(DEPTH):
        xp, (k_l, v_l, re_l, im_l, c_l, p_l) = trunk_layer(
            xp, 0, zero_pool, zero_conv, zero_ssm, zero_ssm, None, None, layer_params(l), l)
        pk.append(k_l); pv.append(v_l); pre.append(re_l); pim.append(im_l); pconv.append(c_l); ppool.append(p_l)

    past = cache_k.shape[2]
    xs = x_sample
    sk, sv, sre, sim, sconv, spool = [], [], [], [], [], []
    for l in range(DEPTH):
        xs, (k_l, v_l, re_l, im_l, c_l, p_l) = trunk_layer(
            xs, past, state_pool[l], state_conv[l], state_ssm_re[l], state_ssm_im[l],
            cache_k[l], cache_v[l], layer_params(l), l)
        sk.append(k_l); sv.append(v_l); sre.append(re_l); sim.append(im_l); sconv.append(c_l); spool.append(p_l)

    return (xp, xs,
            jnp.stack(pk), jnp.stack(pv), jnp.stack(pre), jnp.stack(pim), jnp.stack(pconv), jnp.stack(ppool),
            jnp.stack(sk), jnp.stack(sv), jnp.stack(sre), jnp.stack(sim), jnp.stack(sconv), jnp.stack(spool))
```

```python
import functools
import math

import numpy as np
import jax
import jax.numpy as jnp
from jax import lax
from jax.experimental import pallas as pl
from jax.experimental.pallas import tpu as pltpu

F32 = jnp.float32
BF16 = jnp.bfloat16

LANES = 128
CHUNK = 64
POOL_WINDOWS = (2, 4, 8, 16)
POOL_PAD = 16
CONV_PAD = 8
ROT_DIM = 16
ROPE_THETA = 500000.0
LN_EPS = 1e-5
VMEM_LIMIT = 56 * 1024 * 1024

ROW_TILE = 512
ATTN_TILE = 512
SSM_TILE = 128


def _cparams(sem):
    return pltpu.CompilerParams(dimension_semantics=sem, vmem_limit_bytes=VMEM_LIMIT)


def _const_spec(shape):
    zeros = (0,) * len(shape)
    return pl.BlockSpec(shape, lambda *_: zeros, pipeline_mode=pl.Buffered(1))


def _layer_norm(y, g, b):
    mu = jnp.mean(y, axis=-1, keepdims=True)
    d = y - mu
    var = jnp.mean(d * d, axis=-1, keepdims=True)
    return d * lax.rsqrt(var + LN_EPS) * g + b


def _proj_kernel(x_ref, w_ref, o_ref, *, sigmoid):
    y = jnp.dot(x_ref[...].astype(BF16), w_ref[...], preferred_element_type=F32)
    if sigmoid:
        y = jax.nn.sigmoid(y)
    o_ref[...] = y.astype(o_ref.dtype)


def _project(x, w, out_dtype, tn, sigmoid):
    m, k = x.shape
    n = w.shape[1]
    return pl.pallas_call(
        functools.partial(_proj_kernel, sigmoid=sigmoid),
        grid=(m // ROW_TILE, n // tn),
        in_specs=[pl.BlockSpec((ROW_TILE, k), lambda i, j: (i, 0)),
                  pl.BlockSpec((k, tn), lambda i, j: (0, j))],
        out_specs=pl.BlockSpec((ROW_TILE, tn), lambda i, j: (i, j)),
        out_shape=jax.ShapeDtypeStruct((m, n), out_dtype),
        compiler_params=_cparams(("parallel", "arbitrary")),
        name="proj_sigmoid" if sigmoid else "proj_plain",
    )(x, w)


def _rope(y, cos, sin_up, sin_dn):
    outs = []
    for c in range(y.shape[1] // LANES):
        yc = y[:, c * LANES:(c + 1) * LANES]
        outs.append(yc * cos + pltpu.roll(yc, LANES - ROT_DIM // 2, 1) * sin_up
                    + pltpu.roll(yc, ROT_DIM // 2, 1) * sin_dn)
    return jnp.concatenate(outs, axis=1)


def _qkv_kernel(x_ref, w_ref, cos_ref, su_ref, sd_ref, q_ref, kf_ref, vf_ref, kb_ref, vb_ref, *, width, q_scale):
    xb = x_ref[...].astype(BF16)
    cos, su, sd = cos_ref[...], su_ref[...], sd_ref[...]
    q = jnp.dot(xb, w_ref[:, 0:width], preferred_element_type=F32)
    q_ref[...] = (_rope(q, cos, su, sd) * q_scale).astype(BF16)
    k = _rope(jnp.dot(xb, w_ref[:, width:2 * width], preferred_element_type=F32), cos, su, sd)
    kf_ref[...] = k
    kb_ref[...] = k.astype(BF16)
    v = jnp.dot(xb, w_ref[:, 2 * width:3 * width], preferred_element_type=F32)
    vf_ref[...] = v
    vb_ref[...] = v.astype(BF16)


def _qkv_project(x_all, w_qkv, tables, row_off, rows, q_scale):
    k = x_all.shape[1]
    width = w_qkv.shape[1] // 3
    tm = min(ROW_TILE, rows)
    blk0 = row_off // tm
    row_spec = lambda cols: pl.BlockSpec((tm, cols), lambda i: (i, 0))
    tab_spec = pl.BlockSpec((tm, LANES), lambda i: (blk0 + i, 0))
    return pl.pallas_call(
        functools.partial(_qkv_kernel, width=width, q_scale=q_scale),
        grid=(rows // tm,),
        in_specs=[pl.BlockSpec((tm, k), lambda i: (blk0 + i, 0)), _const_spec(w_qkv.shape),
                  tab_spec, tab_spec, tab_spec],
        out_specs=[row_spec(width)] * 5,
        out_shape=[jax.ShapeDtypeStruct((rows, width), BF16),
                   jax.ShapeDtypeStruct((rows, width), F32), jax.ShapeDtypeStruct((rows, width), F32),
                   jax.ShapeDtypeStruct((rows, width), BF16), jax.ShapeDtypeStruct((rows, width), BF16)],
        compiler_params=_cparams(("parallel",)),
        name="proj_qkv",
    )(x_all, w_qkv, *tables)


def _rope_tables(pos):
    half = ROT_DIM // 2
    inv = ROPE_THETA ** (-jnp.arange(0, ROT_DIM, 2, dtype=F32) / ROT_DIM)
    ang = pos.astype(F32)[:, None] * inv[None, :]
    cos, sin = jnp.cos(ang), jnp.sin(ang)
    n = pos.shape[0]
    ones = jnp.ones((n, 64 - ROT_DIM), F32)
    zeros = jnp.zeros((n, 64 - half), F32)
    cos64 = jnp.concatenate([cos, cos, ones], axis=1)
    up64 = jnp.concatenate([-sin, zeros], axis=1)
    dn64 = jnp.concatenate([jnp.zeros((n, half), F32), sin, jnp.zeros((n, 64 - ROT_DIM), F32)], axis=1)
    return tuple(jnp.concatenate([t, t], axis=1) for t in (cos64, up64, dn64))


def _local_kernel(z_ref, h_ref, b_ref, c_ref, hp_ref, hc_ref, pw_ref, ps_ref, cw_ref, cb_ref,
                  op_ref, oc_ref, np_ref, nc_ref, zbuf, cbuf, *, tl, pos0):
    i = pl.program_id(1)
    last = pl.num_programs(1) - 1

    @pl.when(i == 0)
    def _():
        zbuf[0:POOL_PAD, :] = hp_ref[0]
        cbuf[0:CONV_PAD, :] = hc_ref[0]

    @pl.when(i > 0)
    def _():
        zbuf[0:POOL_PAD, :] = zbuf[tl:tl + POOL_PAD, :]
        cbuf[0:CONV_PAD, :] = cbuf[tl:tl + CONV_PAD, :]

    z = z_ref[...]
    zbuf[POOL_PAD:POOL_PAD + tl, :] = z
    cbuf[CONV_PAD:CONV_PAD + tl, :] = c_ref[...] * h_ref[...]

    pos = pos0 + i * tl + lax.broadcasted_iota(jnp.int32, (tl, 1), 0)
    gw = z.shape[1] // len(POOL_WINDOWS)
    outs = []
    for g, w in enumerate(POOL_WINDOWS):
        cols = slice(g * gw, (g + 1) * gw)
        s = z[:, cols]
        for j in range(1, w):
            s = s + zbuf[POOL_PAD - j:POOL_PAD - j + tl, cols]
        cnt = jnp.minimum(w, pos + 1).astype(F32)
        u = s / cnt - z[:, cols]
        outs.append(jnp.dot(u.astype(BF16), pw_ref[g], preferred_element_type=F32))
    op_ref[...] = (jnp.concatenate(outs, axis=1) * ps_ref[...]).astype(op_ref.dtype)

    y = cb_ref[...] + cbuf[CONV_PAD - 2:CONV_PAD - 2 + tl, :] * cw_ref[0:1, :]
    y = y + cbuf[CONV_PAD - 1:CONV_PAD - 1 + tl, :] * cw_ref[1:2, :]
    y = y + cbuf[CONV_PAD:CONV_PAD + tl, :] * cw_ref[2:3, :]
    oc_ref[...] = (b_ref[...] * y).astype(oc_ref.dtype)

    @pl.when(i == last)
    def _():
        np_ref[0] = zbuf[tl + 1:tl + POOL_PAD, :]
        nc_ref[0] = cbuf[tl + CONV_PAD - 2:tl + CONV_PAD, :]


def _local_mixers(mix, hist_pool, hist_conv, pool_w, pool_scale, conv_w, conv_b, row_off, bt, L, pos0):
    width = pool_scale.shape[-1]
    tl = min(ROW_TILE, L)
    nl = L // tl
    blk0 = row_off // tl
    col_spec = lambda cb: pl.BlockSpec((tl, width), lambda b, i: (blk0 + b * nl + i, cb))
    out_spec = pl.BlockSpec((tl, width), lambda b, i: (b * nl + i, 0))
    hp = jnp.pad(hist_pool, ((0, 0), (POOL_PAD - hist_pool.shape[1], 0), (0, 0)))
    hc = jnp.pad(hist_conv, ((0, 0), (CONV_PAD - hist_conv.shape[1], 0), (0, 0)))
    n_hp, n_hc = hist_pool.shape[1], hist_conv.shape[1]
    return pl.pallas_call(
        functools.partial(_local_kernel, tl=tl, pos0=pos0),
        grid=(bt, nl),
        in_specs=[col_spec(0), col_spec(2), col_spec(3), col_spec(4),
                  pl.BlockSpec((1, POOL_PAD, width), lambda b, i: (b, 0, 0)),
                  pl.BlockSpec((1, CONV_PAD, width), lambda b, i: (b, 0, 0)),
                  _const_spec(pool_w.shape), _const_spec((1, width)),
                  _const_spec(conv_w.shape), _const_spec((1, width))],
        out_specs=[out_spec, out_spec,
                   pl.BlockSpec((1, n_hp, width), lambda b, i: (b, 0, 0)),
                   pl.BlockSpec((1, n_hc, width), lambda b, i: (b, 0, 0))],
        out_shape=[jax.ShapeDtypeStruct((bt * L, width), BF16), jax.ShapeDtypeStruct((bt * L, width), BF16),
                   jax.ShapeDtypeStruct((bt, n_hp, width), F32), jax.ShapeDtypeStruct((bt, n_hc, width), F32)],
        scratch_shapes=[pltpu.VMEM((POOL_PAD + tl, width), F32), pltpu.VMEM((CONV_PAD + tl, width), F32)],
        compiler_params=_cparams(("parallel", "arbitrary")),
        name="local_mixers",
    )(mix, mix, mix, mix, hp, hc, pool_w.astype(BF16), pool_scale.reshape(1, width),
      conv_w, conv_b.reshape(1, width))


def _discretize_kernel(are_ref, aim_ref, ldt_ref, bre_ref, bim_ref, abr_ref, abi_ref, bbr_ref, bbi_ref):
    a_re, a_im = are_ref[...], aim_ref[...]
    dt = jnp.exp(ldt_ref[...])
    mag = jnp.exp(a_re * dt)
    ab_re = mag * jnp.cos(a_im * dt)
    ab_im = mag * jnp.sin(a_im * dt)
    den = a_re * a_re + a_im * a_im
    cr = ((ab_re - 1.0) * a_re + ab_im * a_im) / den
    ci = (ab_im * a_re - (ab_re - 1.0) * a_im) / den
    b_re, b_im = bre_ref[...], bim_ref[...]
    abr_ref[...] = ab_re
    abi_ref[...] = ab_im
    bbr_ref[...] = cr * b_re - ci * b_im
    bbi_ref[...] = cr * b_im + ci * b_re


def _discretize(a_re, a_im, log_dt, b_re, b_im):
    g, p = a_re.shape
    n = b_re.shape[-1]
    col = lambda t: t.reshape(g * p, 1)
    ldt = jnp.broadcast_to(log_dt[:, None], (g, p))
    shapes = [jax.ShapeDtypeStruct((g * p, 1), F32)] * 2 + [jax.ShapeDtypeStruct((g * p, n), F32)] * 2
    return pl.pallas_call(_discretize_kernel, out_shape=shapes, name="ssm_discretize")(
        col(a_re), col(a_im), col(ldt), b_re.reshape(g * p, n), b_im.reshape(g * p, n))


def _shift_rows(x, d):
    rows = x.shape[0]
    if d % 8 == 0:
        return jnp.concatenate([jnp.zeros((d, x.shape[1]), x.dtype), x[:rows - d]], axis=0)
    keep = lax.broadcasted_iota(jnp.int32, (rows, 1), 0) >= d
    return jnp.where(keep, pltpu.roll(x, d, 0), 0.0)


def _ssm_kernel(u_ref, hre_ref, him_ref, are_ref, aim_ref, bcat_ref, ccat_ref, d_ref, wg_ref,
                o_ref, nre_ref, nim_ref, s_ref, cre_ref, cim_ref, *, tl, halves, chunks):
    i = pl.program_id(1)

    @pl.when(i == 0)
    def _():
        cre_ref[...] = hre_ref[0]
        cim_ref[...] = him_ref[0]

    u = u_ref[...]
    ub = u.astype(BF16)
    kw = ub.shape[1] // halves
    per_half = 2 * chunks
    for h in range(halves):
        bu = jnp.dot(ub[:, h * kw:(h + 1) * kw], bcat_ref[h], preferred_element_type=F32)
        for q in range(per_half):
            s_ref[h * per_half + q] = bu[:, q * LANES:(q + 1) * LANES]

    row0 = lax.broadcasted_iota(jnp.int32, (tl, 1), 0) == 0

    def scan_chunk(j, carry):
        ire = (j // chunks) * per_half + (j % chunks)
        iim = ire + chunks
        xr, xi = s_ref[ire], s_ref[iim]
        ar, ai = are_ref[j], aim_ref[j]
        pr, pi = cre_ref[j], cim_ref[j]
        xr = xr + jnp.where(row0, ar * pr - ai * pi, 0.0)
        xi = xi + jnp.where(row0, ar * pi + ai * pr, 0.0)
        d = 1
        while d < tl:
            sr, si = _shift_rows(xr, d), _shift_rows(xi, d)
            xr, xi = xr + ar * sr - ai * si, xi + ar * si + ai * sr
            ar, ai = ar * ar - ai * ai, 2.0 * ar * ai
            d *= 2
        s_ref[ire] = xr
        s_ref[iim] = xi
        cre_ref[j] = xr[tl - 1:tl]
        cim_ref[j] = xi[tl - 1:tl]
        return carry

    lax.fori_loop(0, halves * chunks, scan_chunk, 0)

    ys = []
    for h in range(halves):
        st = jnp.concatenate([s_ref[h * per_half + q] for q in range(per_half)], axis=1)
        ys.append(jnp.dot(st.astype(BF16), ccat_ref[h], preferred_element_type=F32))
    y = jnp.concatenate(ys, axis=1) + d_ref[...] * u
    v = 0.5 * y * (1.0 + jnp.tanh(math.sqrt(2.0 / math.pi) * (y + 0.044715 * (y * y * y))))
    gate = jax.nn.sigmoid(jnp.dot(v.astype(BF16), wg_ref[...], preferred_element_type=F32))
    o_ref[...] = (v * gate).astype(o_ref.dtype)

    @pl.when(i == pl.num_programs(1) - 1)
    def _():
        nre_ref[0] = cre_ref[...]
        nim_ref[0] = cim_ref[...]


def _ssm_weights(ab_re, ab_im, bb_re, bb_im, c_re, c_im, halves):
    g, n, p = c_re.shape
    gh = g // halves
    eye = jnp.eye(gh, dtype=F32)
    bcat, ccat = [], []
    for h in range(halves):
        sl = slice(h * gh, (h + 1) * gh)
        dense_b = lambda t: jnp.einsum('gpn,gk->gnkp', t.reshape(g, p, n)[sl], eye).reshape(gh * n, gh * p)
        dense_c = lambda t: jnp.einsum('gnp,gk->gpkn', t[sl], eye).reshape(gh * p, gh * n)
        bcat.append(jnp.concatenate([dense_b(bb_re), dense_b(bb_im)], axis=1))
        ccat.append(jnp.concatenate([dense_c(c_re), -dense_c(c_im)], axis=0))
    nch = g * p // LANES
    return (ab_re.reshape(nch, 1, LANES), ab_im.reshape(nch, 1, LANES),
            jnp.stack(bcat).astype(BF16), jnp.stack(ccat).astype(BF16))


def _ssm_mixer(mix, h_re, h_im, ssm_w, d_skip, w_glu, row_off, bt, L):
    a_re, a_im, bcat, ccat = ssm_w
    halves = bcat.shape[0]
    width = d_skip.shape[-1]
    nch = a_re.shape[0]
    chunks = nch // halves
    g, p = h_re.shape[1], h_re.shape[2]
    tl = min(SSM_TILE, L)
    nl = L // tl
    blk0 = row_off // tl
    state_spec = pl.BlockSpec((1, nch, 1, LANES), lambda b, i: (b, 0, 0, 0))
    o, n_re, n_im = pl.pallas_call(
        functools.partial(_ssm_kernel, tl=tl, halves=halves, chunks=chunks),
        grid=(bt, nl),
        in_specs=[pl.BlockSpec((tl, width), lambda b, i: (blk0 + b * nl + i, 1)),
                  state_spec, state_spec,
                  _const_spec(a_re.shape), _const_spec(a_im.shape),
                  _const_spec(bcat.shape), _const_spec(ccat.shape),
                  _const_spec((1, width)), _const_spec(w_glu.shape)],
        out_specs=[pl.BlockSpec((tl, width), lambda b, i: (b * nl + i, 0)), state_spec, state_spec],
        out_shape=[jax.ShapeDtypeStruct((bt * L, width), BF16),
                   jax.ShapeDtypeStruct((bt, nch, 1, LANES), F32), jax.ShapeDtypeStruct((bt, nch, 1, LANES), F32)],
        scratch_shapes=[pltpu.VMEM((2 * nch, tl, LANES), F32),
                        pltpu.VMEM((nch, 1, LANES), F32), pltpu.VMEM((nch, 1, LANES), F32)],
        compiler_params=_cparams(("parallel", "arbitrary")),
        name="ssm_mixer",
    )(mix, h_re.astype(F32).reshape(bt, nch, 1, LANES), h_im.astype(F32).reshape(bt, nch, 1, LANES),
      a_re, a_im, bcat, ccat, d_skip.reshape(1, width), w_glu.astype(BF16))
    return o, n_re.reshape(bt, g, p), n_im.reshape(bt, g, p)


def _attn_kernel(qi_ref, ki_ref, fl_ref, q_ref, k_ref, v_ref, lam_ref, sw_ref, o_ref, m_ref, l_ref, acc_ref,
                 *, tq, tk, heads, q_pos0, lk, lam_init):
    step = pl.program_id(1)
    qi, ki, fl = qi_ref[step], ki_ref[step], fl_ref[step]

    @pl.when(ki == 0)
    def _():
        m_ref[...] = jnp.full(m_ref.shape, -jnp.inf, F32)
        l_ref[...] = jnp.zeros(l_ref.shape, F32)
        acc_ref[...] = jnp.zeros(acc_ref.shape, F32)

    low_lanes = lax.broadcasted_iota(jnp.int32, (1, LANES), 1) < (LANES // 2)

    def accumulate(masked):
        if masked:
            q_pos = q_pos0 + qi * tq + lax.broadcasted_iota(jnp.int32, (tq, 1), 0)
            k_pos = ki * tk + lax.broadcasted_iota(jnp.int32, (1, tk), 1)
            visible = (k_pos < (q_pos // CHUNK + 1) * CHUNK) & (k_pos < lk)
        for h in range(heads):
            cols = slice(h * LANES, (h + 1) * LANES)
            qh, kh, vh = q_ref[0, :, cols], k_ref[0, :, cols], v_ref[0, :, cols]
            for c in range(2):
                qm = jnp.where(low_lanes if c == 0 else jnp.logical_not(low_lanes), qh, jnp.zeros_like(qh))
                s = lax.dot_general(qm, kh, (((1,), (1,)), ((), ())), preferred_element_type=F32)
                if masked:
                    s = jnp.where(visible, s, -jnp.inf)
                idx = 2 * h + c
                m_old = m_ref[idx]
                m_new = jnp.maximum(m_old, jnp.max(s, axis=1, keepdims=True))
                alpha = jnp.exp(m_old - m_new)
                p = jnp.exp(s - m_new)
                l_ref[idx] = alpha * l_ref[idx] + jnp.sum(p, axis=1, keepdims=True)
                acc_ref[idx] = alpha * acc_ref[idx] + jnp.dot(p.astype(BF16), vh, preferred_element_type=F32)
                m_ref[idx] = m_new

    @pl.when((fl & 1) == 0)
    def _():
        accumulate(False)

    @pl.when((fl & 1) != 0)
    def _():
        accumulate(True)

    @pl.when((fl & 2) != 0)
    def _():
        lp = lam_ref[...]
        lam = (jnp.exp(jnp.sum(lp[0:1] * lp[1:2], axis=1, keepdims=True))
               - jnp.exp(jnp.sum(lp[2:3] * lp[3:4], axis=1, keepdims=True)) + lam_init)
        for h in range(heads):
            o = acc_ref[2 * h] / l_ref[2 * h] - lam * (acc_ref[2 * h + 1] / l_ref[2 * h + 1])
            o = o * lax.rsqrt(jnp.mean(o * o, axis=1, keepdims=True) + LN_EPS) * sw_ref[...] * (1.0 - lam_init)
            o_ref[0, :, h * LANES:(h + 1) * LANES] = o.astype(o_ref.dtype)


def _attn_schedule(L, lk, tq, tk, q_pos0):
    qi, ki, fl = [], [], []
    for a in range(L // tq):
        first_end = ((q_pos0 + a * tq) // CHUNK + 1) * CHUNK
        last_end = min(((q_pos0 + a * tq + tq - 1) // CHUNK + 1) * CHUNK, lk)
        nk = -(-last_end // tk)
        for b in range(nk):
            full = (b + 1) * tk <= min(first_end, lk)
            qi.append(a); ki.append(b); fl.append((0 if full else 1) | (2 if b == nk - 1 else 0))
    return tuple(jnp.asarray(np.asarray(t, np.int32)) for t in (qi, ki, fl))


def _attention(q, k_all, v_all, lam_p, subln_w, bt, L, lk, q_pos0, lam_init):
    width = q.shape[-1]
    heads = width // LANES
    tq = min(ATTN_TILE, L)
    tk = ATTN_TILE
    lk_pad = k_all.shape[1]
    qi, ki, fl = _attn_schedule(L, lk, tq, tk, q_pos0)
    grid_spec = pltpu.PrefetchScalarGridSpec(
        num_scalar_prefetch=3,
        grid=(bt, int(qi.shape[0])),
        in_specs=[pl.BlockSpec((1, tq, width), lambda b, s, qi, ki, fl: (b, qi[s], 0)),
                  pl.BlockSpec((1, tk, width), lambda b, s, qi, ki, fl: (b, ki[s], 0)),
                  pl.BlockSpec((1, tk, width), lambda b, s, qi, ki, fl: (b, ki[s], 0)),
                  pl.BlockSpec(lam_p.shape, lambda b, s, qi, ki, fl: (0, 0)),
                  pl.BlockSpec((1, LANES), lambda b, s, qi, ki, fl: (0, 0))],
        out_specs=pl.BlockSpec((1, tq, width), lambda b, s, qi, ki, fl: (b, qi[s], 0)),
        scratch_shapes=[pltpu.VMEM((2 * heads, tq, 1), F32), pltpu.VMEM((2 * heads, tq, 1), F32),
                        pltpu.VMEM((2 * heads, tq, LANES), F32)])
    assert lk_pad % tk == 0 and L % tq == 0
    out = pl.pallas_call(
        functools.partial(_attn_kernel, tq=tq, tk=tk, heads=heads, q_pos0=q_pos0, lk=lk, lam_init=lam_init),
        grid_spec=grid_spec,
        out_shape=jax.ShapeDtypeStruct((bt, L, width), BF16),
        compiler_params=_cparams(("parallel", "arbitrary")),
        name="diff_attention",
    )(qi, ki, fl, q.reshape(bt, L, width), k_all, v_all, lam_p, subln_w.reshape(1, LANES))
    return out.reshape(bt * L, width)


def _merge_kernel(x_ref, g_ref, op_ref, os_ref, oc_ref, oa_ref, wb_ref, wo_ref, lg_ref, lb_ref, o_ref,
                  *, d_model, alpha, offs):
    merged = None
    for b, (o_b, (lo, hi)) in enumerate(zip((op_ref, os_ref, oc_ref, oa_ref), offs)):
        t = jnp.dot(o_b[...], wb_ref[lo:hi, :], preferred_element_type=F32)
        t = t * g_ref[:, b * d_model:(b + 1) * d_model].astype(F32)
        merged = t if merged is None else merged + t
    y = alpha * x_ref[...] + jnp.dot(merged.astype(BF16), wo_ref[...], preferred_element_type=F32)
    o_ref[...] = _layer_norm(y, lg_ref[...], lb_ref[...])


def _merge(x, gates, o_pool, o_ssm, o_conv, o_attn, w_branch, w_out, ln_g, ln_b, alpha):
    m, d = x.shape
    widths = [o_pool.shape[1], o_ssm.shape[1], o_conv.shape[1], o_attn.shape[1]]
    ends = np.cumsum(widths)
    offs = tuple((int(e - w), int(e)) for e, w in zip(ends, widths))
    row = lambda cols: pl.BlockSpec((ROW_TILE, cols), lambda i: (i, 0))
    return pl.pallas_call(
        functools.partial(_merge_kernel, d_model=d, alpha=alpha, offs=offs),
        grid=(m // ROW_TILE,),
        in_specs=[row(d), row(gates.shape[1])] + [row(w) for w in widths]
                 + [_const_spec(w_branch.shape), _const_spec(w_out.shape), _const_spec((1, d)), _const_spec((1, d))],
        out_specs=row(d),
        out_shape=jax.ShapeDtypeStruct((m, d), F32),
        compiler_params=_cparams(("parallel",)),
        name="merge_out_ln",
    )(x, gates, o_pool, o_ssm, o_conv, o_attn, w_branch, w_out, ln_g.reshape(1, d), ln_b.reshape(1, d))


def _ffn_kernel(x_ref, wu_ref, wd_ref, lg_ref, lb_ref, o_ref, *, alpha, chunk):
    x = x_ref[...]
    xb = x.astype(BF16)
    acc = alpha * x
    for c in range(wu_ref.shape[1] // chunk):
        hid = jnp.dot(xb, wu_ref[:, c * chunk:(c + 1) * chunk], preferred_element_type=F32)
        hid = jnp.square(jnp.maximum(hid, 0.0)).astype(BF16)
        acc = acc + jnp.dot(hid, wd_ref[c * chunk:(c + 1) * chunk, :], preferred_element_type=F32)
    o_ref[...] = _layer_norm(acc, lg_ref[...], lb_ref[...])


def _ffn(x, w_up, w_down, ln_g, ln_b, alpha):
    m, d = x.shape
    row = pl.BlockSpec((ROW_TILE, d), lambda i: (i, 0))
    return pl.pallas_call(
        functools.partial(_ffn_kernel, alpha=alpha, chunk=1024),
        grid=(m // ROW_TILE,),
        in_specs=[row, _const_spec(w_up.shape), _const_spec(w_down.shape), _const_spec((1, d)), _const_spec((1, d))],
        out_specs=row,
        out_shape=jax.ShapeDtypeStruct((m, d), F32),
        compiler_params=_cparams(("parallel",)),
        name="ffn_ln",
    )(x, w_up, w_down, ln_g.reshape(1, d), ln_b.reshape(1, d))


def kernel(x_prompt, x_sample, cache_k, cache_v, state_ssm_re, state_ssm_im, state_conv, state_pool, w_in, pool_w, pool_scale, ssm_a_re, ssm_a_im, ssm_log_dt, ssm_b_re, ssm_b_im, ssm_c_re, ssm_c_im, ssm_d, ssm_w_glu, conv_w, conv_b, lambda_q1, lambda_k1, lambda_q2, lambda_k2, subln_w, w_branch, w_out, ln1_g, ln1_b, w_up, w_down, ln2_g, ln2_b):
    depth = w_in.shape[0]
    bp, lp, d = x_prompt.shape
    bs, ls, _ = x_sample.shape
    past = cache_k.shape[2]
    heads, qk_dim = cache_k.shape[3], cache_k.shape[5]
    width_qk = heads * 2 * qk_dim
    pool_width, ssm_width, conv_width = pool_scale.shape[1], ssm_d.shape[1], conv_w.shape[2]
    n_mix = pool_width + ssm_width + 3 * conv_width
    alpha = float((2 * depth) ** 0.25)
    paths = ((bp, lp, 0, 0), (bs, ls, past, bp * lp))

    x = jnp.concatenate([x_prompt.reshape(bp * lp, d), x_sample.reshape(bs * ls, d)], axis=0)
    pos = jnp.concatenate([jnp.tile(jnp.arange(lp, dtype=jnp.int32), bp),
                           jnp.tile(past + jnp.arange(ls, dtype=jnp.int32), bs)])
    tables = _rope_tables(pos)
    lk_s = past + ls
    lk_s_pad = -(-lk_s // ATTN_TILE) * ATTN_TILE

    outs = [[[] for _ in range(6)] for _ in paths]
    for l in range(depth):
        wl = w_in[l].astype(BF16)
        mix = _project(x, wl[:, :n_mix], F32, n_mix, sigmoid=False)
        gates = _project(x, wl[:, n_mix + 3 * width_qk:], BF16, 2048, sigmoid=True)
        ab_re, ab_im, bb_re, bb_im = _discretize(ssm_a_re[l], ssm_a_im[l], ssm_log_dt[l], ssm_b_re[l], ssm_b_im[l])
        ssm_w = _ssm_weights(ab_re, ab_im, bb_re, bb_im, ssm_c_re[l], ssm_c_im[l], halves=2)
        lam_p = jnp.stack([lambda_q1[l], lambda_k1[l], lambda_q2[l], lambda_k2[l]]).astype(F32)
        lam_init = 0.8 - 0.6 * math.exp(-0.3 * l)

        branch = []
        for pi, (bt, L, pos0, row_off) in enumerate(paths):
            q, kf, vf, kb, vb = _qkv_project(x, wl[:, n_mix:n_mix + 3 * width_qk], tables, row_off, bt * L,
                                             float(qk_dim) ** -0.5)
            if pi == 0:
                hist_pool = jnp.zeros((bt, state_pool.shape[2], pool_width), F32)
                hist_conv = jnp.zeros((bt, state_conv.shape[2], conv_width), F32)
                h_re = h_im = jnp.zeros((bt,) + state_ssm_re.shape[2:], F32)
                k_all, v_all, lk = kb.reshape(bt, L, width_qk), vb.reshape(bt, L, width_qk), L
            else:
                hist_pool, hist_conv, h_re, h_im = state_pool[l], state_conv[l], state_ssm_re[l], state_ssm_im[l]
                pad = ((0, 0), (0, lk_s_pad - lk_s), (0, 0))
                k_all = jnp.pad(jnp.concatenate([cache_k[l].reshape(bt, past, width_qk).astype(BF16),
                                                 kb.reshape(bt, L, width_qk)], axis=1), pad)
                v_all = jnp.pad(jnp.concatenate([cache_v[l].reshape(bt, past, width_qk).astype(BF16),
                                                 vb.reshape(bt, L, width_qk)], axis=1), pad)
                lk = lk_s
            o_pool, o_conv, new_pool, new_conv = _local_mixers(
                mix, hist_pool, hist_conv, pool_w[l], pool_scale[l], conv_w[l], conv_b[l], row_off, bt, L, pos0)
            o_ssm, new_re, new_im = _ssm_mixer(mix, h_re, h_im, ssm_w, ssm_d[l], ssm_w_glu[l], row_off, bt, L)
            o_attn = _attention(q, k_all, v_all, lam_p, subln_w[l], bt, L, lk, pos0, lam_init)
            branch.append((o_pool, o_ssm, o_conv, o_attn))
            new = (kf.reshape(bt, L, heads, 2, qk_dim), vf.reshape(bt, L, heads, 2 * qk_dim),
                   new_re, new_im, new_conv, new_pool)
            for slot, val in zip(outs[pi], new):
                slot.append(val)

        o_pool, o_ssm, o_conv, o_attn = (jnp.concatenate(parts, axis=0) for parts in zip(*branch))
        x = _merge(x, gates, o_pool, o_ssm, o_conv, o_attn, w_branch[l].astype(BF16), w_out[l].astype(BF16),
                   ln1_g[l], ln1_b[l], alpha)
        x = _ffn(x, w_up[l].astype(BF16), w_down[l].astype(BF16), ln2_g[l], ln2_b[l], alpha)

    y_prompt = x[:bp * lp].reshape(bp, lp, d)
    y_sample = x[bp * lp:].reshape(bs, ls, d)
    stacked = [[jnp.stack(slot) for slot in path_outs] for path_outs in outs]
    return (y_prompt, y_sample, *stacked[0], *stacked[1])
```

```python
import functools
import math

import numpy as np
import jax
import jax.numpy as jnp
from jax import lax
from jax.experimental import pallas as pl
from jax.experimental.pallas import tpu as pltpu

F32 = jnp.float32
BF16 = jnp.bfloat16

LANES = 128
CHUNK = 64
POOL_WINDOWS = (2, 4, 8, 16)
POOL_PAD = 16
CONV_PAD = 8
ROT_DIM = 16
ROPE_THETA = 500000.0
LN_EPS = 1e-5
VMEM_LIMIT = 56 * 1024 * 1024

ROW_TILE = 512
ATTN_TILE = 512
SSM_TILE = 128


def _cparams(sem):
    return pltpu.CompilerParams(dimension_semantics=sem, vmem_limit_bytes=VMEM_LIMIT)


def _const_spec(shape):
    zeros = (0,) * len(shape)
    return pl.BlockSpec(shape, lambda *_: zeros, pipeline_mode=pl.Buffered(1))


def _layer_norm(y, g, b):
    mu = jnp.mean(y, axis=-1, keepdims=True)
    d = y - mu
    var = jnp.mean(d * d, axis=-1, keepdims=True)
    return d * lax.rsqrt(var + LN_EPS) * g + b


def _proj_kernel(x_ref, w_ref, o_ref, *, sigmoid):
    y = jnp.dot(x_ref[...].astype(BF16), w_ref[...], preferred_element_type=F32)
    if sigmoid:
        y = jax.nn.sigmoid(y)
    o_ref[...] = y.astype(o_ref.dtype)


def _project(x, w, out_dtype, tn, sigmoid):
    m, k = x.shape
    n = w.shape[1]
    return pl.pallas_call(
        functools.partial(_proj_kernel, sigmoid=sigmoid),
        grid=(m // ROW_TILE, n // tn),
        in_specs=[pl.BlockSpec((ROW_TILE, k), lambda i, j: (i, 0)),
                  pl.BlockSpec((k, tn), lambda i, j: (0, j))],
        out_specs=pl.BlockSpec((ROW_TILE, tn), lambda i, j: (i, j)),
        out_shape=jax.ShapeDtypeStruct((m, n), out_dtype),
        compiler_params=_cparams(("parallel", "arbitrary")),
        name="proj_sigmoid" if sigmoid else "proj_plain",
    )(x, w)


def _rope(y, cos, sin_up, sin_dn):
    outs = []
    for c in range(y.shape[1] // LANES):
        yc = y[:, c * LANES:(c + 1) * LANES]
        outs.append(yc * cos + pltpu.roll(yc, LANES - ROT_DIM // 2, 1) * sin_up
                    + pltpu.roll(yc, ROT_DIM // 2, 1) * sin_dn)
    return jnp.concatenate(outs, axis=1)


def _qkv_kernel(x_ref, w_ref, cos_ref, su_ref, sd_ref, q_ref, kf_ref, vf_ref, kb_ref, vb_ref, *, width, q_scale):
    xb = x_ref[...].astype(BF16)
    cos, su, sd = cos_ref[...], su_ref[...], sd_ref[...]
    q = jnp.dot(xb, w_ref[:, 0:width], preferred_element_type=F32)
    q_ref[...] = (_rope(q, cos, su, sd) * q_scale).astype(BF16)
    k = _rope(jnp.dot(xb, w_ref[:, width:2 * width], preferred_element_type=F32), cos, su, sd)
    kf_ref[...] = k
    kb_ref[...] = k.astype(BF16)
    v = jnp.dot(xb, w_ref[:, 2 * width:3 * width], preferred_element_type=F32)
    vf_ref[...] = v
    vb_ref[...] = v.astype(BF16)


def _qkv_project(x_all, w_qkv, tables, row_off, rows, q_scale):
    k = x_all.shape[1]
    width = w_qkv.shape[1] // 3
    tm = min(ROW_TILE, rows)
    blk0 = row_off // tm
    row_spec = lambda cols: pl.BlockSpec((tm, cols), lambda i: (i, 0))
    tab_spec = pl.BlockSpec((tm, LANES), lambda i: (blk0 + i, 0))
    return pl.pallas_call(
        functools.partial(_qkv_kernel, width=width, q_scale=q_scale),
        grid=(rows // tm,),
        in_specs=[pl.BlockSpec((tm, k), lambda i: (blk0 + i, 0)), _const_spec(w_qkv.shape),
                  tab_spec, tab_spec, tab_spec],
        out_specs=[row_spec(width)] * 5,
        out_shape=[jax.ShapeDtypeStruct((rows, width), BF16),
                   jax.ShapeDtypeStruct((rows, width), F32), jax.ShapeDtypeStruct((rows, width), F32),
                   jax.ShapeDtypeStruct((rows, width), BF16), jax.ShapeDtypeStruct((rows, width), BF16)],
        compiler_params=_cparams(("parallel",)),
        name="proj_qkv",
    )(x_all, w_qkv, *tables)


def _rope_tables(pos):
    half = ROT_DIM // 2
    inv = ROPE_THETA ** (-jnp.arange(0, ROT_DIM, 2, dtype=F32) / ROT_DIM)
    ang = pos.astype(F32)[:, None] * inv[None, :]
    cos, sin = jnp.cos(ang), jnp.sin(ang)
    n = pos.shape[0]
    ones = jnp.ones((n, 64 - ROT_DIM), F32)
    zeros = jnp.zeros((n, 64 - half), F32)
    cos64 = jnp.concatenate([cos, cos, ones], axis=1)
    up64 = jnp.concatenate([-sin, zeros], axis=1)
    dn64 = jnp.concatenate([jnp.zeros((n, half), F32), sin, jnp.zeros((n, 64 - ROT_DIM), F32)], axis=1)
    return tuple(jnp.concatenate([t, t], axis=1) for t in (cos64, up64, dn64))


def _local_kernel(z_ref, h_ref, b_ref, c_ref, hp_ref, hc_ref, pw_ref, ps_ref, cw_ref, cb_ref,
                  op_ref, oc_ref, np_ref, nc_ref, zbuf, cbuf, *, tl, pos0):
    i = pl.program_id(1)
    last = pl.num_programs(1) - 1

    @pl.when(i == 0)
    def _():
        zbuf[0:POOL_PAD, :] = hp_ref[0]
        cbuf[0:CONV_PAD, :] = hc_ref[0]

    @pl.when(i > 0)
    def _():
        zbuf[0:POOL_PAD, :] = zbuf[tl:tl + POOL_PAD, :]
        cbuf[0:CONV_PAD, :] = cbuf[tl:tl + CONV_PAD, :]

    z = z_ref[...]
    zbuf[POOL_PAD:POOL_PAD + tl, :] = z
    cbuf[CONV_PAD:CONV_PAD + tl, :] = c_ref[...] * h_ref[...]

    pos = pos0 + i * tl + lax.broadcasted_iota(jnp.int32, (tl, 1), 0)
    gw = z.shape[1] // len(POOL_WINDOWS)
    outs = []
    for g, w in enumerate(POOL_WINDOWS):
        cols = slice(g * gw, (g + 1) * gw)
        s = z[:, cols]
        for j in range(1, w):
            s = s + zbuf[POOL_PAD - j:POOL_PAD - j + tl, cols]
        cnt = jnp.minimum(w, pos + 1).astype(F32)
        u = s / cnt - z[:, cols]
        outs.append(jnp.dot(u.astype(BF16), pw_ref[g], preferred_element_type=F32))
    op_ref[...] = (jnp.concatenate(outs, axis=1) * ps_ref[...]).astype(op_ref.dtype)

    y = cb_ref[...] + cbuf[CONV_PAD - 2:CONV_PAD - 2 + tl, :] * cw_ref[0:1, :]
    y = y + cbuf[CONV_PAD - 1:CONV_PAD - 1 + tl, :] * cw_ref[1:2, :]
    y = y + cbuf[CONV_PAD:CONV_PAD + tl, :] * cw_ref[2:3, :]
    oc_ref[...] = (b_ref[...] * y).astype(oc_ref.dtype)

    @pl.when(i == last)
    def _():
        np_ref[0] = zbuf[tl + 1:tl + POOL_PAD, :]
        nc_ref[0] = cbuf[tl + CONV_PAD - 2:tl + CONV_PAD, :]


def _local_mixers(mix, hist_pool, hist_conv, pool_w, pool_scale, conv_w, conv_b, row_off, bt, L, pos0):
    width = pool_scale.shape[-1]
    tl = min(ROW_TILE, L)
    nl = L // tl
    blk0 = row_off // tl
    col_spec = lambda cb: pl.BlockSpec((tl, width), lambda b, i: (blk0 + b * nl + i, cb))
    out_spec = pl.BlockSpec((tl, width), lambda b, i: (b * nl + i, 0))
    hp = jnp.pad(hist_pool, ((0, 0), (POOL_PAD - hist_pool.shape[1], 0), (0, 0)))
    hc = jnp.pad(hist_conv, ((0, 0), (CONV_PAD - hist_conv.shape[1], 0), (0, 0)))
    n_hp, n_hc = hist_pool.shape[1], hist_conv.shape[1]
    return pl.pallas_call(
        functools.partial(_local_kernel, tl=tl, pos0=pos0),
        grid=(bt, nl),
        in_specs=[col_spec(0), col_spec(2), col_spec(3), col_spec(4),
                  pl.BlockSpec((1, POOL_PAD, width), lambda b, i: (b, 0, 0)),
                  pl.BlockSpec((1, CONV_PAD, width), lambda b, i: (b, 0, 0)),
                  _const_spec(pool_w.shape), _const_spec((1, width)),
                  _const_spec(conv_w.shape), _const_spec((1, width))],
        out_specs=[out_spec, out_spec,
                   pl.BlockSpec((1, n_hp, width), lambda b, i: (b, 0, 0)),
                   pl.BlockSpec((1, n_hc, width), lambda b, i: (b, 0, 0))],
        out_shape=[jax.ShapeDtypeStruct((bt * L, width), BF16), jax.ShapeDtypeStruct((bt * L, width), BF16),
                   jax.ShapeDtypeStruct((bt, n_hp, width), F32), jax.ShapeDtypeStruct((bt, n_hc, width), F32)],
        scratch_shapes=[pltpu.VMEM((POOL_PAD + tl, width), F32), pltpu.VMEM((CONV_PAD + tl, width), F32)],
        compiler_params=_cparams(("parallel", "arbitrary")),
        name="local_mixers",
    )(mix, mix, mix, mix, hp, hc, pool_w.astype(BF16), pool_scale.reshape(1, width),
      conv_w, conv_b.reshape(1, width))


def _discretize_kernel(are_ref, aim_ref, ldt_ref, bre_ref, bim_ref, abr_ref, abi_ref, bbr_ref, bbi_ref):
    a_re, a_im = are_ref[...], aim_ref[...]
    dt = jnp.exp(ldt_ref[...])
    mag = jnp.exp(a_re * dt)
    ab_re = mag * jnp.cos(a_im * dt)
    ab_im = mag * jnp.sin(a_im * dt)
    den = a_re * a_re + a_im * a_im
    cr = ((ab_re - 1.0) * a_re + ab_im * a_im) / den
    ci = (ab_im * a_re - (ab_re - 1.0) * a_im) / den
    b_re, b_im = bre_ref[...], bim_ref[...]
    abr_ref[...] = ab_re
    abi_ref[...] = ab_im
    bbr_ref[...] = cr * b_re - ci * b_im
    bbi_ref[...] = cr * b_im + ci * b_re


def _discretize(a_re, a_im, log_dt, b_re, b_im):
    g, p = a_re.shape
    n = b_re.shape[-1]
    col = lambda t: t.reshape(g * p, 1)
    ldt = jnp.broadcast_to(log_dt[:, None], (g, p))
    shapes = [jax.ShapeDtypeStruct((g * p, 1), F32)] * 2 + [jax.ShapeDtypeStruct((g * p, n), F32)] * 2
    return pl.pallas_call(_discretize_kernel, out_shape=shapes, name="ssm_discretize")(
        col(a_re), col(a_im), col(ldt), b_re.reshape(g * p, n), b_im.reshape(g * p, n))


def _shift_rows(x, d):
    rows = x.shape[0]
    if d % 8 == 0:
        return jnp.concatenate([jnp.zeros((d, x.shape[1]), x.dtype), x[:rows - d]], axis=0)
    keep = lax.broadcasted_iota(jnp.int32, (rows, 1), 0) >= d
    return jnp.where(keep, pltpu.roll(x, d, 0), 0.0)


def _ssm_kernel(u_ref, hre_ref, him_ref, are_ref, aim_ref, bcat_ref, ccat_ref, d_ref, wg_ref,
                o_ref, nre_ref, nim_ref, s_ref, cre_ref, cim_ref, *, tl, halves, chunks):
    i = pl.program_id(1)

    @pl.when(i == 0)
    def _():
        cre_ref[...] = hre_ref[0]
        cim_ref[...] = him_ref[0]

    u = u_ref[...]
    ub = u.astype(BF16)
    kw = ub.shape[1] // halves
    per_half = 2 * chunks
    for h in range(halves):
        bu = jnp.dot(ub[:, h * kw:(h + 1) * kw], bcat_ref[h], preferred_element_type=F32)
        for q in range(per_half):
            s_ref[h * per_half + q] = bu[:, q * LANES:(q + 1) * LANES]

    row0 = lax.broadcasted_iota(jnp.int32, (tl, 1), 0) == 0

    def scan_chunk(j, carry):
        ire = (j // chunks) * per_half + (j % chunks)
        iim = ire + chunks
        xr, xi = s_ref[ire], s_ref[iim]
        ar, ai = are_ref[j], aim_ref[j]
        pr, pi = cre_ref[j], cim_ref[j]
        xr = xr + jnp.where(row0, ar * pr - ai * pi, 0.0)
        xi = xi + jnp.where(row0, ar * pi + ai * pr, 0.0)
        d = 1
        while d < tl:
            sr, si = _shift_rows(xr, d), _shift_rows(xi, d)
            xr, xi = xr + ar * sr - ai * si, xi + ar * si + ai * sr
            ar, ai = ar * ar - ai * ai, 2.0 * ar * ai
            d *= 2
        s_ref[ire] = xr
        s_ref[iim] = xi
        cre_ref[j] = xr[tl - 1:tl]
        cim_ref[j] = xi[tl - 1:tl]
        return carry

    lax.fori_loop(0, halves * chunks, scan_chunk, 0)

    ys = []
    for h in range(halves):
        st = jnp.concatenate([s_ref[h * per_half + q] for q in range(per_half)], axis=1)
        ys.append(jnp.dot(st.astype(BF16), ccat_ref[h], preferred_element_type=F32))
    y = jnp.concatenate(ys, axis=1) + d_ref[...] * u
    v = 0.5 * y * (1.0 + jnp.tanh(math.sqrt(2.0 / math.pi) * (y + 0.044715 * (y * y * y))))
    gate = jax.nn.sigmoid(jnp.dot(v.astype(BF16), wg_ref[...], preferred_element_type=F32))
    o_ref[...] = (v * gate).astype(o_ref.dtype)

    @pl.when(i == pl.num_programs(1) - 1)
    def _():
        nre_ref[0] = cre_ref[...]
        nim_ref[0] = cim_ref[...]


def _ssm_weights(ab_re, ab_im, bb_re, bb_im, c_re, c_im, halves):
    g, n, p = c_re.shape
    gh = g // halves
    eye = jnp.eye(gh, dtype=F32)
    bcat, ccat = [], []
    for h in range(halves):
        sl = slice(h * gh, (h + 1) * gh)
        dense_b = lambda t: jnp.einsum('gpn,gk->gnkp', t.reshape(g, p, n)[sl], eye).reshape(gh * n, gh * p)
        dense_c = lambda t: jnp.einsum('gnp,gk->gpkn', t[sl], eye).reshape(gh * p, gh * n)
        bcat.append(jnp.concatenate([dense_b(bb_re), dense_b(bb_im)], axis=1))
        ccat.append(jnp.concatenate([dense_c(c_re), -dense_c(c_im)], axis=0))
    nch = g * p // LANES
    return (ab_re.reshape(nch, 1, LANES), ab_im.reshape(nch, 1, LANES),
            jnp.stack(bcat).astype(BF16), jnp.stack(ccat).astype(BF16))


def _ssm_mixer(mix, h_re, h_im, ssm_w, d_skip, w_glu, row_off, bt, L):
    a_re, a_im, bcat, ccat = ssm_w
    halves = bcat.shape[0]
    width = d_skip.shape[-1]
    nch = a_re.shape[0]
    chunks = nch // halves
    g, p = h_re.shape[1], h_re.shape[2]
    tl = min(SSM_TILE, L)
    nl = L // tl
    blk0 = row_off // tl
    state_spec = pl.BlockSpec((1, nch, 1, LANES), lambda b, i: (b, 0, 0, 0))
    o, n_re, n_im = pl.pallas_call(
        functools.partial(_ssm_kernel, tl=tl, halves=halves, chunks=chunks),
        grid=(bt, nl),
        in_specs=[pl.BlockSpec((tl, width), lambda b, i: (blk0 + b * nl + i, 1)),
                  state_spec, state_spec,
                  _const_spec(a_re.shape), _const_spec(a_im.shape),
                  _const_spec(bcat.shape), _const_spec(ccat.shape),
                  _const_spec((1, width)), _const_spec(w_glu.shape)],
        out_specs=[pl.BlockSpec((tl, width), lambda b, i: (b * nl + i, 0)), state_spec, state_spec],
        out_shape=[jax.ShapeDtypeStruct((bt * L, width), BF16),
                   jax.ShapeDtypeStruct((bt, nch, 1, LANES), F32), jax.ShapeDtypeStruct((bt, nch, 1, LANES), F32)],
        scratch_shapes=[pltpu.VMEM((2 * nch, tl, LANES), F32),
                        pltpu.VMEM((nch, 1, LANES), F32), pltpu.VMEM((nch, 1, LANES), F32)],
        compiler_params=_cparams(("parallel", "arbitrary")),
        name="ssm_mixer",
    )(mix, h_re.astype(F32).reshape(bt, nch, 1, LANES), h_im.astype(F32).reshape(bt, nch, 1, LANES),
      a_re, a_im, bcat, ccat, d_skip.reshape(1, width), w_glu.astype(BF16))
    return o, n_re.reshape(bt, g, p), n_im.reshape(bt, g, p)


def _attn_init(m_ref, l_ref, acc_ref):
    m_ref[...] = jnp.full(m_ref.shape, -jnp.inf, F32)
    l_ref[...] = jnp.zeros(l_ref.shape, F32)
    acc_ref[...] = jnp.zeros(acc_ref.shape, F32)


def _attn_update(q_of, k_of, v_of, visible, heads, m_ref, l_ref, acc_ref):
    low_lanes = lax.broadcasted_iota(jnp.int32, (1, LANES), 1) < (LANES // 2)
    if visible is not None:
        visible = jnp.concatenate([visible, visible], axis=0)
    for h in range(heads):
        qh, kh, vh = q_of(h), k_of(h), v_of(h)
        zero = jnp.zeros_like(qh)
        qm = jnp.concatenate([jnp.where(low_lanes, qh, zero), jnp.where(low_lanes, zero, qh)], axis=0)
        s = lax.dot_general(qm, kh, (((1,), (1,)), ((), ())), preferred_element_type=F32)
        if visible is not None:
            s = jnp.where(visible, s, -jnp.inf)
        m_old = m_ref[h]
        m_new = jnp.maximum(m_old, jnp.max(s, axis=1, keepdims=True))
        alpha = jnp.exp2(m_old - m_new)
        ps = [jnp.exp2(s[:, j * LANES:(j + 1) * LANES] - m_new) for j in range(s.shape[1] // LANES)]
        lsum = ps[0]
        for pj in ps[1:]:
            lsum = lsum + pj
        l_ref[h] = alpha * l_ref[h] + lsum
        p = jnp.concatenate([pj.astype(BF16) for pj in ps], axis=1)
        acc_ref[h] = alpha * acc_ref[h] + jnp.dot(p, vh, preferred_element_type=F32)
        m_ref[h] = m_new


def _attn_finalize(lam_ref, sw_ref, o_ref, l_ref, acc_ref, heads, lam_init):
    lp = lam_ref[...]
    lam = (jnp.exp(jnp.sum(lp[0:1] * lp[1:2], axis=1, keepdims=True))
           - jnp.exp(jnp.sum(lp[2:3] * lp[3:4], axis=1, keepdims=True)) + lam_init)
    rows = o_ref.shape[1]
    for h in range(heads):
        oh = acc_ref[h] / jnp.sum(l_ref[h], axis=1, keepdims=True)
        o = oh[:rows] - lam * oh[rows:]
        o = o * lax.rsqrt(jnp.mean(o * o, axis=1, keepdims=True) + LN_EPS) * sw_ref[...] * (1.0 - lam_init)
        o_ref[0, :, h * LANES:(h + 1) * LANES] = o.astype(o_ref.dtype)


def _head_cols(ref):
    return lambda h: ref[0, :, h * LANES:(h + 1) * LANES]


def _attn_kernel(qi_ref, ki_ref, fl_ref, q_ref, k_ref, v_ref, lam_ref, sw_ref, o_ref, m_ref, l_ref, acc_ref,
                 *, tq, tk, heads, q_pos0, lk, lam_init):
    step = pl.program_id(1)
    qi, ki, fl = qi_ref[step], ki_ref[step], fl_ref[step]

    @pl.when(ki == 0)
    def _():
        _attn_init(m_ref, l_ref, acc_ref)

    def accumulate(masked):
        visible = None
        if masked:
            q_pos = q_pos0 + qi * tq + lax.broadcasted_iota(jnp.int32, (tq, 1), 0)
            k_pos = ki * tk + lax.broadcasted_iota(jnp.int32, (1, tk), 1)
            visible = (k_pos < (q_pos // CHUNK + 1) * CHUNK) & (k_pos < lk)
        _attn_update(_head_cols(q_ref), _head_cols(k_ref), _head_cols(v_ref), visible, heads, m_ref, l_ref, acc_ref)

    @pl.when((fl & 1) == 0)
    def _():
        accumulate(False)

    @pl.when((fl & 1) != 0)
    def _():
        accumulate(True)

    @pl.when((fl & 2) != 0)
    def _():
        _attn_finalize(lam_ref, sw_ref, o_ref, l_ref, acc_ref, heads, lam_init)


def _decode_attn_kernel(q_ref, ck_ref, cv_ref, nk_ref, nv_ref, lam_ref, sw_ref, o_ref, m_ref, l_ref, acc_ref,
                        *, tq, heads, past, n_cache, lam_init):
    j = pl.program_id(1)

    @pl.when(j == 0)
    def _():
        _attn_init(m_ref, l_ref, acc_ref)

    @pl.when(j < n_cache)
    def _():
        _attn_update(_head_cols(q_ref), _head_cols(ck_ref), _head_cols(cv_ref), None, heads, m_ref, l_ref, acc_ref)

    @pl.when(j == n_cache)
    def _():
        nk = nk_ref.shape[1]
        t = lax.broadcasted_iota(jnp.int32, (tq, 1), 0)
        i = lax.broadcasted_iota(jnp.int32, (1, nk), 1)
        visible = (past + i < ((past + t) // CHUNK + 1) * CHUNK) & (i < tq)
        _attn_update(_head_cols(q_ref), _head_cols(nk_ref), _head_cols(nv_ref), visible, heads, m_ref, l_ref, acc_ref)
        _attn_finalize(lam_ref, sw_ref, o_ref, l_ref, acc_ref, heads, lam_init)


def _attn_schedule(L, lk, tq, tk, q_pos0):
    qi, ki, fl = [], [], []
    for a in range(L // tq):
        first_end = ((q_pos0 + a * tq) // CHUNK + 1) * CHUNK
        last_end = min(((q_pos0 + a * tq + tq - 1) // CHUNK + 1) * CHUNK, lk)
        nk = -(-last_end // tk)
        for b in range(nk):
            full = (b + 1) * tk <= min(first_end, lk)
            qi.append(a); ki.append(b); fl.append((0 if full else 1) | (2 if b == nk - 1 else 0))
    return tuple(jnp.asarray(np.asarray(t, np.int32)) for t in (qi, ki, fl))


def _attention(q, k_all, v_all, lam_p, subln_w, bt, L, lk, q_pos0, lam_init):
    width = q.shape[-1]
    heads = width // LANES
    tq = min(ATTN_TILE, L)
    tk = ATTN_TILE
    lk_pad = k_all.shape[1]
    qi, ki, fl = _attn_schedule(L, lk, tq, tk, q_pos0)
    grid_spec = pltpu.PrefetchScalarGridSpec(
        num_scalar_prefetch=3,
        grid=(bt, int(qi.shape[0])),
        in_specs=[pl.BlockSpec((1, tq, width), lambda b, s, qi, ki, fl: (b, qi[s], 0)),
                  pl.BlockSpec((1, tk, width), lambda b, s, qi, ki, fl: (b, ki[s], 0)),
                  pl.BlockSpec((1, tk, width), lambda b, s, qi, ki, fl: (b, ki[s], 0)),
                  pl.BlockSpec(lam_p.shape, lambda b, s, qi, ki, fl: (0, 0)),
                  pl.BlockSpec((1, LANES), lambda b, s, qi, ki, fl: (0, 0))],
        out_specs=pl.BlockSpec((1, tq, width), lambda b, s, qi, ki, fl: (b, qi[s], 0)),
        scratch_shapes=[pltpu.VMEM((heads, 2 * tq, LANES), F32)] * 3)
    assert lk_pad % tk == 0 and L % tq == 0
    out = pl.pallas_call(
        functools.partial(_attn_kernel, tq=tq, tk=tk, heads=heads, q_pos0=q_pos0, lk=lk, lam_init=lam_init),
        grid_spec=grid_spec,
        out_shape=jax.ShapeDtypeStruct((bt, L, width), BF16),
        compiler_params=_cparams(("parallel", "arbitrary")),
        name="diff_attention",
    )(qi, ki, fl, q.reshape(bt, L, width), k_all, v_all, lam_p, subln_w.reshape(1, LANES))
    return out.reshape(bt * L, width)


def _decode_attention(q, cache_k, cache_v, new_k, new_v, lam_p, subln_w, bt, L, past, lam_init):
    width = q.shape[-1]
    heads = width // LANES
    tk = ATTN_TILE
    assert past % tk == 0 and past >= tk and L <= LANES and past % CHUNK == 0
    n_cache = past // tk
    pad = ((0, 0), (0, LANES - L), (0, 0))
    cache_spec = pl.BlockSpec((1, tk, width), lambda b, j: (b, jnp.minimum(j, n_cache - 1), 0))
    new_spec = pl.BlockSpec((1, LANES, width), lambda b, j: (b, 0, 0))
    row_spec = pl.BlockSpec((1, L, width), lambda b, j: (b, 0, 0))
    scratch = pltpu.VMEM((heads, 2 * L, LANES), F32)
    out = pl.pallas_call(
        functools.partial(_decode_attn_kernel, tq=L, heads=heads, past=past, n_cache=n_cache, lam_init=lam_init),
        grid=(bt, n_cache + 1),
        in_specs=[row_spec, cache_spec, cache_spec, new_spec, new_spec,
                  pl.BlockSpec(lam_p.shape, lambda b, j: (0, 0)), pl.BlockSpec((1, LANES), lambda b, j: (0, 0))],
        out_specs=row_spec,
        out_shape=jax.ShapeDtypeStruct((bt, L, width), BF16),
        scratch_shapes=[scratch, scratch, scratch],
        compiler_params=_cparams(("parallel", "arbitrary")),
        name="decode_attention",
    )(q.reshape(bt, L, width), cache_k, cache_v, jnp.pad(new_k.reshape(bt, L, width), pad),
      jnp.pad(new_v.reshape(bt, L, width), pad), lam_p, subln_w.reshape(1, LANES))
    return out.reshape(bt * L, width)


def _merge_kernel(x_ref, g_ref, op_ref, os_ref, oc_ref, oa_ref, wb_ref, wo_ref, lg_ref, lb_ref, o_ref,
                  *, d_model, alpha, offs):
    merged = None
    for b, (o_b, (lo, hi)) in enumerate(zip((op_ref, os_ref, oc_ref, oa_ref), offs)):
        t = jnp.dot(o_b[...], wb_ref[lo:hi, :], preferred_element_type=F32)
        t = t * g_ref[:, b * d_model:(b + 1) * d_model].astype(F32)
        merged = t if merged is None else merged + t
    y = alpha * x_ref[...] + jnp.dot(merged.astype(BF16), wo_ref[...], preferred_element_type=F32)
    o_ref[...] = _layer_norm(y, lg_ref[...], lb_ref[...])


def _merge(x, gates, o_pool, o_ssm, o_conv, o_attn, w_branch, w_out, ln_g, ln_b, alpha):
    m, d = x.shape
    widths = [o_pool.shape[1], o_ssm.shape[1], o_conv.shape[1], o_attn.shape[1]]
    ends = np.cumsum(widths)
    offs = tuple((int(e - w), int(e)) for e, w in zip(ends, widths))
    row = lambda cols: pl.BlockSpec((ROW_TILE, cols), lambda i: (i, 0))
    return pl.pallas_call(
        functools.partial(_merge_kernel, d_model=d, alpha=alpha, offs=offs),
        grid=(m // ROW_TILE,),
        in_specs=[row(d), row(gates.shape[1])] + [row(w) for w in widths]
                 + [_const_spec(w_branch.shape), _const_spec(w_out.shape), _const_spec((1, d)), _const_spec((1, d))],
        out_specs=row(d),
        out_shape=jax.ShapeDtypeStruct((m, d), F32),
        compiler_params=_cparams(("parallel",)),
        name="merge_out_ln",
    )(x, gates, o_pool, o_ssm, o_conv, o_attn, w_branch, w_out, ln_g.reshape(1, d), ln_b.reshape(1, d))


def _ffn_kernel(x_ref, wu_ref, wd_ref, lg_ref, lb_ref, o_ref, *, alpha, chunk):
    x = x_ref[...]
    xb = x.astype(BF16)
    acc = alpha * x
    for c in range(wu_ref.shape[1] // chunk):
        hid = jnp.dot(xb, wu_ref[:, c * chunk:(c + 1) * chunk], preferred_element_type=F32)
        hid = jnp.square(jnp.maximum(hid, 0.0)).astype(BF16)
        acc = acc + jnp.dot(hid, wd_ref[c * chunk:(c + 1) * chunk, :], preferred_element_type=F32)
    o_ref[...] = _layer_norm(acc, lg_ref[...], lb_ref[...])


def _ffn(x, w_up, w_down, ln_g, ln_b, alpha):
    m, d = x.shape
    row = pl.BlockSpec((ROW_TILE, d), lambda i: (i, 0))
    return pl.pallas_call(
        functools.partial(_ffn_kernel, alpha=alpha, chunk=1024),
        grid=(m // ROW_TILE,),
        in_specs=[row, _const_spec(w_up.shape), _const_spec(w_down.shape), _const_spec((1, d)), _const_spec((1, d))],
        out_specs=row,
        out_shape=jax.ShapeDtypeStruct((m, d), F32),
        compiler_params=_cparams(("parallel",)),
        name="ffn_ln",
    )(x, w_up, w_down, ln_g.reshape(1, d), ln_b.reshape(1, d))


def kernel(x_prompt, x_sample, cache_k, cache_v, state_ssm_re, state_ssm_im, state_conv, state_pool, w_in, pool_w, pool_scale, ssm_a_re, ssm_a_im, ssm_log_dt, ssm_b_re, ssm_b_im, ssm_c_re, ssm_c_im, ssm_d, ssm_w_glu, conv_w, conv_b, lambda_q1, lambda_k1, lambda_q2, lambda_k2, subln_w, w_branch, w_out, ln1_g, ln1_b, w_up, w_down, ln2_g, ln2_b):
    depth = w_in.shape[0]
    bp, lp, d = x_prompt.shape
    bs, ls, _ = x_sample.shape
    past = cache_k.shape[2]
    heads, qk_dim = cache_k.shape[3], cache_k.shape[5]
    width_qk = heads * 2 * qk_dim
    pool_width, ssm_width, conv_width = pool_scale.shape[1], ssm_d.shape[1], conv_w.shape[2]
    n_mix = pool_width + ssm_width + 3 * conv_width
    alpha = float((2 * depth) ** 0.25)
    paths = ((bp, lp, 0, 0), (bs, ls, past, bp * lp))

    x = jnp.concatenate([x_prompt.reshape(bp * lp, d), x_sample.reshape(bs * ls, d)], axis=0)
    pos = jnp.concatenate([jnp.tile(jnp.arange(lp, dtype=jnp.int32), bp),
                           jnp.tile(past + jnp.arange(ls, dtype=jnp.int32), bs)])
    tables = _rope_tables(pos)

    outs = [[[] for _ in range(6)] for _ in paths]
    for l in range(depth):
        wl = w_in[l].astype(BF16)
        mix = _project(x, wl[:, :n_mix], F32, n_mix, sigmoid=False)
        gates = _project(x, wl[:, n_mix + 3 * width_qk:], BF16, 2048, sigmoid=True)
        ab_re, ab_im, bb_re, bb_im = _discretize(ssm_a_re[l], ssm_a_im[l], ssm_log_dt[l], ssm_b_re[l], ssm_b_im[l])
        ssm_w = _ssm_weights(ab_re, ab_im, bb_re, bb_im, ssm_c_re[l], ssm_c_im[l], halves=2)
        lam_p = jnp.stack([lambda_q1[l], lambda_k1[l], lambda_q2[l], lambda_k2[l]]).astype(F32)
        lam_init = 0.8 - 0.6 * math.exp(-0.3 * l)

        branch = []
        for pi, (bt, L, pos0, row_off) in enumerate(paths):
            q, kf, vf, kb, vb = _qkv_project(x, wl[:, n_mix:n_mix + 3 * width_qk], tables, row_off, bt * L,
                                             float(qk_dim) ** -0.5 * math.log2(math.e))
            if pi == 0:
                hist_pool = jnp.zeros((bt, state_pool.shape[2], pool_width), F32)
                hist_conv = jnp.zeros((bt, state_conv.shape[2], conv_width), F32)
                h_re = h_im = jnp.zeros((bt,) + state_ssm_re.shape[2:], F32)
                o_attn = _attention(q, kb.reshape(bt, L, width_qk), vb.reshape(bt, L, width_qk), lam_p, subln_w[l],
                                    bt, L, L, pos0, lam_init)
            else:
                hist_pool, hist_conv, h_re, h_im = state_pool[l], state_conv[l], state_ssm_re[l], state_ssm_im[l]
                o_attn = _decode_attention(q, cache_k[l].reshape(bt, past, width_qk).astype(BF16),
                                           cache_v[l].reshape(bt, past, width_qk).astype(BF16), kb, vb,
                                           lam_p, subln_w[l], bt, L, past, lam_init)
            o_pool, o_conv, new_pool, new_conv = _local_mixers(
                mix, hist_pool, hist_conv, pool_w[l], pool_scale[l], conv_w[l], conv_b[l], row_off, bt, L, pos0)
            o_ssm, new_re, new_im = _ssm_mixer(mix, h_re, h_im, ssm_w, ssm_d[l], ssm_w_glu[l], row_off, bt, L)
            branch.append((o_pool, o_ssm, o_conv, o_attn))
            new = (kf.reshape(bt, L, heads, 2, qk_dim), vf.reshape(bt, L, heads, 2 * qk_dim),
                   new_re, new_im, new_conv, new_pool)
            for slot, val in zip(outs[pi], new):
                slot.append(val)

        o_pool, o_ssm, o_conv, o_attn = (jnp.concatenate(parts, axis=0) for parts in zip(*branch))
        x = _merge(x, gates, o_pool, o_ssm, o_conv, o_attn, w_branch[l].astype(BF16), w_out[l].astype(BF16),
                   ln1_g[l], ln1_b[l], alpha)
        x = _ffn(x, w_up[l].astype(BF16), w_down[l].astype(BF16), ln2_g[l], ln2_b[l], alpha)

    y_prompt = x[:bp * lp].reshape(bp, lp, d)
    y_sample = x[bp * lp:].reshape(bs, ls, d)
    stacked = [[jnp.stack(slot) for slot in path_outs] for path_outs in outs]
    return (y_prompt, y_sample, *stacked[0], *stacked[1])
```

```python
import functools
import math

import numpy as np
import jax
import jax.numpy as jnp
from jax import lax
from jax.experimental import pallas as pl
from jax.experimental.pallas import tpu as pltpu

F32 = jnp.float32
BF16 = jnp.bfloat16

LANES = 128
CHUNK = 64
POOL_WINDOWS = (2, 4, 8, 16)
POOL_PAD = 16
CONV_PAD = 8
ROT_DIM = 16
ROPE_THETA = 500000.0
LN_EPS = 1e-5
VMEM_LIMIT = 56 * 1024 * 1024

ROW_TILE = 512
ATTN_TILE = 512
SSM_TILE = 128


def _cparams(sem):
    return pltpu.CompilerParams(dimension_semantics=sem, vmem_limit_bytes=VMEM_LIMIT)


def _const_spec(shape):
    zeros = (0,) * len(shape)
    return pl.BlockSpec(shape, lambda *_: zeros, pipeline_mode=pl.Buffered(1))


def _layer_norm(y, g, b):
    mu = jnp.mean(y, axis=-1, keepdims=True)
    d = y - mu
    var = jnp.mean(d * d, axis=-1, keepdims=True)
    return d * lax.rsqrt(var + LN_EPS) * g + b


def _proj_kernel(x_ref, w_ref, o_ref, *, sigmoid):
    y = jnp.dot(x_ref[...].astype(BF16), w_ref[...], preferred_element_type=F32)
    if sigmoid:
        y = jax.nn.sigmoid(y)
    o_ref[...] = y.astype(o_ref.dtype)


def _project(x, w, out_dtype, tn, sigmoid):
    m, k = x.shape
    n = w.shape[1]
    return pl.pallas_call(
        functools.partial(_proj_kernel, sigmoid=sigmoid),
        grid=(m // ROW_TILE, n // tn),
        in_specs=[pl.BlockSpec((ROW_TILE, k), lambda i, j: (i, 0)),
                  pl.BlockSpec((k, tn), lambda i, j: (0, j))],
        out_specs=pl.BlockSpec((ROW_TILE, tn), lambda i, j: (i, j)),
        out_shape=jax.ShapeDtypeStruct((m, n), out_dtype),
        compiler_params=_cparams(("parallel", "arbitrary")),
        name="proj_sigmoid" if sigmoid else "proj_plain",
    )(x, w)


def _rope(y, cos, sin_up, sin_dn):
    outs = []
    for c in range(y.shape[1] // LANES):
        yc = y[:, c * LANES:(c + 1) * LANES]
        outs.append(yc * cos + pltpu.roll(yc, LANES - ROT_DIM // 2, 1) * sin_up
                    + pltpu.roll(yc, ROT_DIM // 2, 1) * sin_dn)
    return jnp.concatenate(outs, axis=1)


def _qkv_kernel(x_ref, w_ref, cos_ref, su_ref, sd_ref, *refs, width, q_scale, stacked):
    q_ref, kf_ref, vf_ref, kb_ref, vb_ref = refs[-5:]
    xb = x_ref[...].astype(BF16)
    cos, su, sd = cos_ref[...], su_ref[...], sd_ref[...]
    q = jnp.dot(xb, w_ref[:, 0:width], preferred_element_type=F32)
    q_ref[...] = (_rope(q, cos, su, sd) * q_scale).astype(BF16)
    k = _rope(jnp.dot(xb, w_ref[:, width:2 * width], preferred_element_type=F32), cos, su, sd)
    v = jnp.dot(xb, w_ref[:, 2 * width:3 * width], preferred_element_type=F32)
    if stacked:
        kt = k.T
        kf_ref[0, 0] = kt
        kb_ref[0] = kt.astype(BF16)
        vf_ref[0] = v
    else:
        kf_ref[...] = k
        kb_ref[...] = k.astype(BF16)
        vf_ref[...] = v
    vb_ref[...] = v.astype(BF16)


def _qkv_project(x_all, w_qkv, tables, row_off, rows, q_scale):
    k = x_all.shape[1]
    width = w_qkv.shape[1] // 3
    tm = min(ROW_TILE, rows)
    blk0 = row_off // tm
    row_spec = lambda cols: pl.BlockSpec((tm, cols), lambda i: (i, 0))
    tab_spec = pl.BlockSpec((tm, LANES), lambda i: (blk0 + i, 0))
    return pl.pallas_call(
        functools.partial(_qkv_kernel, width=width, q_scale=q_scale, stacked=False),
        grid=(rows // tm,),
        in_specs=[pl.BlockSpec((tm, k), lambda i: (blk0 + i, 0)), _const_spec(w_qkv.shape),
                  tab_spec, tab_spec, tab_spec],
        out_specs=[row_spec(width)] * 5,
        out_shape=[jax.ShapeDtypeStruct((rows, width), BF16),
                   jax.ShapeDtypeStruct((rows, width), F32), jax.ShapeDtypeStruct((rows, width), F32),
                   jax.ShapeDtypeStruct((rows, width), BF16), jax.ShapeDtypeStruct((rows, width), BF16)],
        compiler_params=_cparams(("parallel",)),
        name="proj_qkv",
    )(x_all, w_qkv, *tables)


def _qkv_project_stacked(x_all, w_qkv, tables, bt, L, q_scale, layer, depth, prev):
    k = x_all.shape[1]
    width = w_qkv.shape[1] // 3
    rows = bt * L
    tm = min(ROW_TILE, L)
    nl = L // tm
    tab_spec = pl.BlockSpec((tm, LANES), lambda i: (i, 0))
    row_spec = pl.BlockSpec((tm, width), lambda i: (i, 0))
    any_spec = pl.BlockSpec(memory_space=pl.ANY)
    n_prev = 0 if prev is None else 2
    return pl.pallas_call(
        functools.partial(_qkv_kernel, width=width, q_scale=q_scale, stacked=True),
        grid=(rows // tm,),
        in_specs=[pl.BlockSpec((tm, k), lambda i: (i, 0)), _const_spec(w_qkv.shape),
                  tab_spec, tab_spec, tab_spec] + [any_spec] * n_prev,
        out_specs=[row_spec,
                   pl.BlockSpec((1, 1, width, tm), lambda i: (layer, i // nl, 0, i % nl)),
                   pl.BlockSpec((1, tm, width), lambda i: (layer, i, 0)),
                   pl.BlockSpec((1, width, tm), lambda i: (i // nl, 0, i % nl)),
                   row_spec],
        out_shape=[jax.ShapeDtypeStruct((rows, width), BF16),
                   jax.ShapeDtypeStruct((depth, bt, width, L), F32), jax.ShapeDtypeStruct((depth, rows, width), F32),
                   jax.ShapeDtypeStruct((bt, width, L), BF16), jax.ShapeDtypeStruct((rows, width), BF16)],
        input_output_aliases={} if prev is None else {5: 1, 6: 2},
        compiler_params=_cparams(("parallel",)),
        name="proj_qkv_stacked",
    )(x_all, w_qkv, *tables, *(() if prev is None else prev))


def _rope_tables(pos):
    half = ROT_DIM // 2
    inv = ROPE_THETA ** (-jnp.arange(0, ROT_DIM, 2, dtype=F32) / ROT_DIM)
    ang = pos.astype(F32)[:, None] * inv[None, :]
    cos, sin = jnp.cos(ang), jnp.sin(ang)
    n = pos.shape[0]
    ones = jnp.ones((n, 64 - ROT_DIM), F32)
    zeros = jnp.zeros((n, 64 - half), F32)
    cos64 = jnp.concatenate([cos, cos, ones], axis=1)
    up64 = jnp.concatenate([-sin, zeros], axis=1)
    dn64 = jnp.concatenate([jnp.zeros((n, half), F32), sin, jnp.zeros((n, 64 - ROT_DIM), F32)], axis=1)
    return tuple(jnp.concatenate([t, t], axis=1) for t in (cos64, up64, dn64))


def _local_kernel(z_ref, h_ref, b_ref, c_ref, hp_ref, hc_ref, pw_ref, ps_ref, cw_ref, cb_ref,
                  op_ref, oc_ref, np_ref, nc_ref, zbuf, cbuf, *, tl, pos0):
    i = pl.program_id(1)
    last = pl.num_programs(1) - 1

    @pl.when(i == 0)
    def _():
        zbuf[0:POOL_PAD, :] = hp_ref[0]
        cbuf[0:CONV_PAD, :] = hc_ref[0]

    @pl.when(i > 0)
    def _():
        zbuf[0:POOL_PAD, :] = zbuf[tl:tl + POOL_PAD, :]
        cbuf[0:CONV_PAD, :] = cbuf[tl:tl + CONV_PAD, :]

    z = z_ref[...]
    zbuf[POOL_PAD:POOL_PAD + tl, :] = z
    cbuf[CONV_PAD:CONV_PAD + tl, :] = c_ref[...] * h_ref[...]

    pos = pos0 + i * tl + lax.broadcasted_iota(jnp.int32, (tl, 1), 0)
    gw = z.shape[1] // len(POOL_WINDOWS)
    outs = []
    for g, w in enumerate(POOL_WINDOWS):
        cols = slice(g * gw, (g + 1) * gw)
        s = z[:, cols]
        for j in range(1, w):
            s = s + zbuf[POOL_PAD - j:POOL_PAD - j + tl, cols]
        cnt = jnp.minimum(w, pos + 1).astype(F32)
        u = s / cnt - z[:, cols]
        outs.append(jnp.dot(u.astype(BF16), pw_ref[g], preferred_element_type=F32))
    op_ref[...] = (jnp.concatenate(outs, axis=1) * ps_ref[...]).astype(op_ref.dtype)

    y = cb_ref[...] + cbuf[CONV_PAD - 2:CONV_PAD - 2 + tl, :] * cw_ref[0:1, :]
    y = y + cbuf[CONV_PAD - 1:CONV_PAD - 1 + tl, :] * cw_ref[1:2, :]
    y = y + cbuf[CONV_PAD:CONV_PAD + tl, :] * cw_ref[2:3, :]
    oc_ref[...] = (b_ref[...] * y).astype(oc_ref.dtype)

    @pl.when(i == last)
    def _():
        np_ref[0] = zbuf[tl + 1:tl + POOL_PAD, :]
        nc_ref[0] = cbuf[tl + CONV_PAD - 2:tl + CONV_PAD, :]


def _local_mixers(mix, hist_pool, hist_conv, pool_w, pool_scale, conv_w, conv_b, row_off, bt, L, pos0):
    width = pool_scale.shape[-1]
    tl = min(ROW_TILE, L)
    nl = L // tl
    blk0 = row_off // tl
    col_spec = lambda cb: pl.BlockSpec((tl, width), lambda b, i: (blk0 + b * nl + i, cb))
    out_spec = pl.BlockSpec((tl, width), lambda b, i: (b * nl + i, 0))
    hp = jnp.pad(hist_pool, ((0, 0), (POOL_PAD - hist_pool.shape[1], 0), (0, 0)))
    hc = jnp.pad(hist_conv, ((0, 0), (CONV_PAD - hist_conv.shape[1], 0), (0, 0)))
    n_hp, n_hc = hist_pool.shape[1], hist_conv.shape[1]
    return pl.pallas_call(
        functools.partial(_local_kernel, tl=tl, pos0=pos0),
        grid=(bt, nl),
        in_specs=[col_spec(0), col_spec(2), col_spec(3), col_spec(4),
                  pl.BlockSpec((1, POOL_PAD, width), lambda b, i: (b, 0, 0)),
                  pl.BlockSpec((1, CONV_PAD, width), lambda b, i: (b, 0, 0)),
                  _const_spec(pool_w.shape), _const_spec((1, width)),
                  _const_spec(conv_w.shape), _const_spec((1, width))],
        out_specs=[out_spec, out_spec,
                   pl.BlockSpec((1, n_hp, width), lambda b, i: (b, 0, 0)),
                   pl.BlockSpec((1, n_hc, width), lambda b, i: (b, 0, 0))],
        out_shape=[jax.ShapeDtypeStruct((bt * L, width), BF16), jax.ShapeDtypeStruct((bt * L, width), BF16),
                   jax.ShapeDtypeStruct((bt, n_hp, width), F32), jax.ShapeDtypeStruct((bt, n_hc, width), F32)],
        scratch_shapes=[pltpu.VMEM((POOL_PAD + tl, width), F32), pltpu.VMEM((CONV_PAD + tl, width), F32)],
        compiler_params=_cparams(("parallel", "arbitrary")),
        name="local_mixers",
    )(mix, mix, mix, mix, hp, hc, pool_w.astype(BF16), pool_scale.reshape(1, width),
      conv_w, conv_b.reshape(1, width))


def _discretize_kernel(are_ref, aim_ref, ldt_ref, bre_ref, bim_ref, abr_ref, abi_ref, bbr_ref, bbi_ref):
    a_re, a_im = are_ref[...], aim_ref[...]
    dt = jnp.exp(ldt_ref[...])
    mag = jnp.exp(a_re * dt)
    ab_re = mag * jnp.cos(a_im * dt)
    ab_im = mag * jnp.sin(a_im * dt)
    den = a_re * a_re + a_im * a_im
    cr = ((ab_re - 1.0) * a_re + ab_im * a_im) / den
    ci = (ab_im * a_re - (ab_re - 1.0) * a_im) / den
    b_re, b_im = bre_ref[...], bim_ref[...]
    abr_ref[...] = ab_re
    abi_ref[...] = ab_im
    bbr_ref[...] = cr * b_re - ci * b_im
    bbi_ref[...] = cr * b_im + ci * b_re


def _discretize(a_re, a_im, log_dt, b_re, b_im):
    g, p = a_re.shape
    n = b_re.shape[-1]
    col = lambda t: t.reshape(g * p, 1)
    ldt = jnp.broadcast_to(log_dt[:, None], (g, p))
    shapes = [jax.ShapeDtypeStruct((g * p, 1), F32)] * 2 + [jax.ShapeDtypeStruct((g * p, n), F32)] * 2
    return pl.pallas_call(_discretize_kernel, out_shape=shapes, name="ssm_discretize")(
        col(a_re), col(a_im), col(ldt), b_re.reshape(g * p, n), b_im.reshape(g * p, n))


def _shift_rows(x, d):
    rows = x.shape[0]
    if d % 8 == 0:
        return jnp.concatenate([jnp.zeros((d, x.shape[1]), x.dtype), x[:rows - d]], axis=0)
    keep = lax.broadcasted_iota(jnp.int32, (rows, 1), 0) >= d
    return jnp.where(keep, pltpu.roll(x, d, 0), 0.0)


def _ssm_kernel(u_ref, hre_ref, him_ref, are_ref, aim_ref, bcat_ref, ccat_ref, d_ref, wg_ref,
                o_ref, nre_ref, nim_ref, s_ref, cre_ref, cim_ref, *, tl, halves, chunks):
    i = pl.program_id(1)

    @pl.when(i == 0)
    def _():
        cre_ref[...] = hre_ref[0]
        cim_ref[...] = him_ref[0]

    u = u_ref[...]
    ub = u.astype(BF16)
    kw = ub.shape[1] // halves
    per_half = 2 * chunks
    for h in range(halves):
        bu = jnp.dot(ub[:, h * kw:(h + 1) * kw], bcat_ref[h], preferred_element_type=F32)
        for q in range(per_half):
            s_ref[h * per_half + q] = bu[:, q * LANES:(q + 1) * LANES]

    row0 = lax.broadcasted_iota(jnp.int32, (tl, 1), 0) == 0

    def scan_chunk(j, carry):
        ire = (j // chunks) * per_half + (j % chunks)
        iim = ire + chunks
        xr, xi = s_ref[ire], s_ref[iim]
        ar, ai = are_ref[j], aim_ref[j]
        pr, pi = cre_ref[j], cim_ref[j]
        xr = xr + jnp.where(row0, ar * pr - ai * pi, 0.0)
        xi = xi + jnp.where(row0, ar * pi + ai * pr, 0.0)
        d = 1
        while d < tl:
            sr, si = _shift_rows(xr, d), _shift_rows(xi, d)
            xr, xi = xr + ar * sr - ai * si, xi + ar * si + ai * sr
            ar, ai = ar * ar - ai * ai, 2.0 * ar * ai
            d *= 2
        s_ref[ire] = xr
        s_ref[iim] = xi
        cre_ref[j] = xr[tl - 1:tl]
        cim_ref[j] = xi[tl - 1:tl]
        return carry

    lax.fori_loop(0, halves * chunks, scan_chunk, 0)

    ys = []
    for h in range(halves):
        st = jnp.concatenate([s_ref[h * per_half + q] for q in range(per_half)], axis=1)
        ys.append(jnp.dot(st.astype(BF16), ccat_ref[h], preferred_element_type=F32))
    y = jnp.concatenate(ys, axis=1) + d_ref[...] * u
    v = 0.5 * y * (1.0 + jnp.tanh(math.sqrt(2.0 / math.pi) * (y + 0.044715 * (y * y * y))))
    gate = jax.nn.sigmoid(jnp.dot(v.astype(BF16), wg_ref[...], preferred_element_type=F32))
    o_ref[...] = (v * gate).astype(o_ref.dtype)

    @pl.when(i == pl.num_programs(1) - 1)
    def _():
        nre_ref[0] = cre_ref[...]
        nim_ref[0] = cim_ref[...]


def _ssm_weights(ab_re, ab_im, bb_re, bb_im, c_re, c_im, halves):
    g, n, p = c_re.shape
    gh = g // halves
    eye = jnp.eye(gh, dtype=F32)
    bcat, ccat = [], []
    for h in range(halves):
        sl = slice(h * gh, (h + 1) * gh)
        dense_b = lambda t: jnp.einsum('gpn,gk->gnkp', t.reshape(g, p, n)[sl], eye).reshape(gh * n, gh * p)
        dense_c = lambda t: jnp.einsum('gnp,gk->gpkn', t[sl], eye).reshape(gh * p, gh * n)
        bcat.append(jnp.concatenate([dense_b(bb_re), dense_b(bb_im)], axis=1))
        ccat.append(jnp.concatenate([dense_c(c_re), -dense_c(c_im)], axis=0))
    nch = g * p // LANES
    return (ab_re.reshape(nch, 1, LANES), ab_im.reshape(nch, 1, LANES),
            jnp.stack(bcat).astype(BF16), jnp.stack(ccat).astype(BF16))


def _ssm_mixer(mix, h_re, h_im, ssm_w, d_skip, w_glu, row_off, bt, L):
    a_re, a_im, bcat, ccat = ssm_w
    halves = bcat.shape[0]
    width = d_skip.shape[-1]
    nch = a_re.shape[0]
    chunks = nch // halves
    g, p = h_re.shape[1], h_re.shape[2]
    tl = min(SSM_TILE, L)
    nl = L // tl
    blk0 = row_off // tl
    state_spec = pl.BlockSpec((1, nch, 1, LANES), lambda b, i: (b, 0, 0, 0))
    o, n_re, n_im = pl.pallas_call(
        functools.partial(_ssm_kernel, tl=tl, halves=halves, chunks=chunks),
        grid=(bt, nl),
        in_specs=[pl.BlockSpec((tl, width), lambda b, i: (blk0 + b * nl + i, 1)),
                  state_spec, state_spec,
                  _const_spec(a_re.shape), _const_spec(a_im.shape),
                  _const_spec(bcat.shape), _const_spec(ccat.shape),
                  _const_spec((1, width)), _const_spec(w_glu.shape)],
        out_specs=[pl.BlockSpec((tl, width), lambda b, i: (b * nl + i, 0)), state_spec, state_spec],
        out_shape=[jax.ShapeDtypeStruct((bt * L, width), BF16),
                   jax.ShapeDtypeStruct((bt, nch, 1, LANES), F32), jax.ShapeDtypeStruct((bt, nch, 1, LANES), F32)],
        scratch_shapes=[pltpu.VMEM((2 * nch, tl, LANES), F32),
                        pltpu.VMEM((nch, 1, LANES), F32), pltpu.VMEM((nch, 1, LANES), F32)],
        compiler_params=_cparams(("parallel", "arbitrary")),
        name="ssm_mixer",
    )(mix, h_re.astype(F32).reshape(bt, nch, 1, LANES), h_im.astype(F32).reshape(bt, nch, 1, LANES),
      a_re, a_im, bcat, ccat, d_skip.reshape(1, width), w_glu.astype(BF16))
    return o, n_re.reshape(bt, g, p), n_im.reshape(bt, g, p)


def _attn_init(m_ref, l_ref, acc_ref):
    m_ref[...] = jnp.full(m_ref.shape, -jnp.inf, F32)
    l_ref[...] = jnp.zeros(l_ref.shape, F32)
    acc_ref[...] = jnp.zeros(acc_ref.shape, F32)


def _attn_update(q_of, k_of, v_of, visible, heads, m_ref, l_ref, acc_ref):
    low_lanes = lax.broadcasted_iota(jnp.int32, (1, LANES), 1) < (LANES // 2)
    if visible is not None:
        visible = jnp.concatenate([visible, visible], axis=0)
    for h in range(heads):
        qh, kh, vh = q_of(h), k_of(h), v_of(h)
        zero = jnp.zeros_like(qh)
        qm = jnp.concatenate([jnp.where(low_lanes, qh, zero), jnp.where(low_lanes, zero, qh)], axis=0)
        s = jnp.dot(qm, kh, preferred_element_type=F32)
        if visible is not None:
            s = jnp.where(visible, s, -jnp.inf)
        m_old = m_ref[h]
        m_new = jnp.maximum(m_old, jnp.max(s, axis=1, keepdims=True))
        alpha = jnp.exp2(m_old - m_new)
        ps = [jnp.exp2(s[:, j * LANES:(j + 1) * LANES] - m_new) for j in range(s.shape[1] // LANES)]
        lsum = ps[0]
        for pj in ps[1:]:
            lsum = lsum + pj
        l_ref[h] = alpha * l_ref[h] + lsum
        p = jnp.concatenate([pj.astype(BF16) for pj in ps], axis=1)
        acc_ref[h] = alpha * acc_ref[h] + jnp.dot(p, vh, preferred_element_type=F32)
        m_ref[h] = m_new


def _attn_finalize(lam_ref, sw_ref, o_ref, l_ref, acc_ref, heads, lam_init):
    lp = lam_ref[...]
    lam = (jnp.exp(jnp.sum(lp[0:1] * lp[1:2], axis=1, keepdims=True))
           - jnp.exp(jnp.sum(lp[2:3] * lp[3:4], axis=1, keepdims=True)) + lam_init)
    rows = o_ref.shape[1]
    for h in range(heads):
        oh = acc_ref[h] / jnp.sum(l_ref[h], axis=1, keepdims=True)
        o = oh[:rows] - lam * oh[rows:]
        o = o * lax.rsqrt(jnp.mean(o * o, axis=1, keepdims=True) + LN_EPS) * sw_ref[...] * (1.0 - lam_init)
        o_ref[0, :, h * LANES:(h + 1) * LANES] = o.astype(o_ref.dtype)


def _head_cols(ref):
    lead = (0,) * (len(ref.shape) - 2)
    return lambda h: ref[(*lead, slice(None), slice(h * LANES, (h + 1) * LANES))].astype(BF16)


def _head_rows(ref):
    lead = (0,) * (len(ref.shape) - 2)
    return lambda h: ref[(*lead, slice(h * LANES, (h + 1) * LANES), slice(None))].astype(BF16)


def _attn_kernel(qi_ref, ki_ref, fl_ref, q_ref, k_ref, v_ref, lam_ref, sw_ref, o_ref, m_ref, l_ref, acc_ref,
                 *, tq, tk, heads, q_pos0, lk, lam_init):
    step = pl.program_id(1)
    qi, ki, fl = qi_ref[step], ki_ref[step], fl_ref[step]

    @pl.when(ki == 0)
    def _():
        _attn_init(m_ref, l_ref, acc_ref)

    def accumulate(masked):
        visible = None
        if masked:
            q_pos = q_pos0 + qi * tq + lax.broadcasted_iota(jnp.int32, (tq, 1), 0)
            k_pos = ki * tk + lax.broadcasted_iota(jnp.int32, (1, tk), 1)
            visible = (k_pos < (q_pos // CHUNK + 1) * CHUNK) & (k_pos < lk)
        _attn_update(_head_cols(q_ref), _head_rows(k_ref), _head_cols(v_ref), visible, heads, m_ref, l_ref, acc_ref)

    @pl.when((fl & 1) == 0)
    def _():
        accumulate(False)

    @pl.when((fl & 1) != 0)
    def _():
        accumulate(True)

    @pl.when((fl & 2) != 0)
    def _():
        _attn_finalize(lam_ref, sw_ref, o_ref, l_ref, acc_ref, heads, lam_init)


def _decode_attn_kernel(q_ref, ck_ref, cv_ref, nk_ref, nv_ref, lam_ref, sw_ref, o_ref, m_ref, l_ref, acc_ref,
                        *, tq, heads, past, n_cache, lam_init):
    j = pl.program_id(1)

    @pl.when(j == 0)
    def _():
        _attn_init(m_ref, l_ref, acc_ref)

    @pl.when(j < n_cache)
    def _():
        _attn_update(_head_cols(q_ref), _head_rows(ck_ref), _head_cols(cv_ref), None, heads, m_ref, l_ref, acc_ref)

    @pl.when(j == n_cache)
    def _():
        nk = nk_ref.shape[2]
        t = lax.broadcasted_iota(jnp.int32, (tq, 1), 0)
        i = lax.broadcasted_iota(jnp.int32, (1, nk), 1)
        visible = (past + i < ((past + t) // CHUNK + 1) * CHUNK) & (i < tq)
        _attn_update(_head_cols(q_ref), _head_rows(nk_ref), _head_cols(nv_ref), visible, heads, m_ref, l_ref, acc_ref)
        _attn_finalize(lam_ref, sw_ref, o_ref, l_ref, acc_ref, heads, lam_init)


def _attn_schedule(L, lk, tq, tk, q_pos0):
    qi, ki, fl = [], [], []
    for a in range(L // tq):
        first_end = ((q_pos0 + a * tq) // CHUNK + 1) * CHUNK
        last_end = min(((q_pos0 + a * tq + tq - 1) // CHUNK + 1) * CHUNK, lk)
        nk = -(-last_end // tk)
        for b in range(nk):
            full = (b + 1) * tk <= min(first_end, lk)
            qi.append(a); ki.append(b); fl.append((0 if full else 1) | (2 if b == nk - 1 else 0))
    return tuple(jnp.asarray(np.asarray(t, np.int32)) for t in (qi, ki, fl))


def _attention(q, kt_all, v_all, lam_p, subln_w, bt, L, lk, q_pos0, lam_init):
    width = q.shape[-1]
    heads = width // LANES
    tq = min(ATTN_TILE, L)
    tk = ATTN_TILE
    lk_pad = kt_all.shape[2]
    qi, ki, fl = _attn_schedule(L, lk, tq, tk, q_pos0)
    grid_spec = pltpu.PrefetchScalarGridSpec(
        num_scalar_prefetch=3,
        grid=(bt, int(qi.shape[0])),
        in_specs=[pl.BlockSpec((1, tq, width), lambda b, s, qi, ki, fl: (b, qi[s], 0)),
                  pl.BlockSpec((1, width, tk), lambda b, s, qi, ki, fl: (b, 0, ki[s])),
                  pl.BlockSpec((1, tk, width), lambda b, s, qi, ki, fl: (b, ki[s], 0)),
                  pl.BlockSpec(lam_p.shape, lambda b, s, qi, ki, fl: (0, 0)),
                  pl.BlockSpec((1, LANES), lambda b, s, qi, ki, fl: (0, 0))],
        out_specs=pl.BlockSpec((1, tq, width), lambda b, s, qi, ki, fl: (b, qi[s], 0)),
        scratch_shapes=[pltpu.VMEM((heads, 2 * tq, LANES), F32)] * 3)
    assert lk_pad % tk == 0 and L % tq == 0
    out = pl.pallas_call(
        functools.partial(_attn_kernel, tq=tq, tk=tk, heads=heads, q_pos0=q_pos0, lk=lk, lam_init=lam_init),
        grid_spec=grid_spec,
        out_shape=jax.ShapeDtypeStruct((bt, L, width), BF16),
        compiler_params=_cparams(("parallel", "arbitrary")),
        name="diff_attention",
    )(qi, ki, fl, q.reshape(bt, L, width), kt_all, v_all, lam_p, subln_w.reshape(1, LANES))
    return out.reshape(bt * L, width)


def _decode_attention(q, cache_kt, cache_v, layer, new_k, new_v, lam_p, subln_w, bt, L, past, lam_init):
    width = q.shape[-1]
    heads = width // LANES
    tk = ATTN_TILE
    assert past % tk == 0 and past >= tk and L <= LANES and past % CHUNK == 0
    n_cache = past // tk
    new_kt = jnp.pad(jnp.swapaxes(new_k.reshape(bt, L, width), 1, 2), ((0, 0), (0, 0), (0, LANES - L)))
    new_v = jnp.pad(new_v.reshape(bt, L, width), ((0, 0), (0, LANES - L), (0, 0)))
    row_spec = pl.BlockSpec((1, L, width), lambda b, j: (b, 0, 0))
    scratch = pltpu.VMEM((heads, 2 * L, LANES), F32)
    out = pl.pallas_call(
        functools.partial(_decode_attn_kernel, tq=L, heads=heads, past=past, n_cache=n_cache, lam_init=lam_init),
        grid=(bt, n_cache + 1),
        in_specs=[row_spec,
                  pl.BlockSpec((1, 1, width, tk), lambda b, j: (layer, b, 0, jnp.minimum(j, n_cache - 1))),
                  pl.BlockSpec((1, 1, tk, width), lambda b, j: (layer, b, jnp.minimum(j, n_cache - 1), 0)),
                  pl.BlockSpec((1, width, LANES), lambda b, j: (b, 0, 0)),
                  pl.BlockSpec((1, LANES, width), lambda b, j: (b, 0, 0)),
                  pl.BlockSpec(lam_p.shape, lambda b, j: (0, 0)), pl.BlockSpec((1, LANES), lambda b, j: (0, 0))],
        out_specs=row_spec,
        out_shape=jax.ShapeDtypeStruct((bt, L, width), BF16),
        scratch_shapes=[scratch, scratch, scratch],
        compiler_params=_cparams(("parallel", "arbitrary")),
        name="decode_attention",
    )(q.reshape(bt, L, width), cache_kt, cache_v, new_kt, new_v, lam_p, subln_w.reshape(1, LANES))
    return out.reshape(bt * L, width)


def _merge_kernel(x_ref, g_ref, op_ref, os_ref, oc_ref, oa_ref, wb_ref, wo_ref, lg_ref, lb_ref, o_ref,
                  *, d_model, alpha, offs):
    merged = None
    for b, (o_b, (lo, hi)) in enumerate(zip((op_ref, os_ref, oc_ref, oa_ref), offs)):
        t = jnp.dot(o_b[...], wb_ref[lo:hi, :], preferred_element_type=F32)
        t = t * g_ref[:, b * d_model:(b + 1) * d_model].astype(F32)
        merged = t if merged is None else merged + t
    y = alpha * x_ref[...] + jnp.dot(merged.astype(BF16), wo_ref[...], preferred_element_type=F32)
    o_ref[...] = _layer_norm(y, lg_ref[...], lb_ref[...])


def _merge(x, gates, o_pool, o_ssm, o_conv, o_attn, w_branch, w_out, ln_g, ln_b, alpha):
    m, d = x.shape
    widths = [o_pool.shape[1], o_ssm.shape[1], o_conv.shape[1], o_attn.shape[1]]
    ends = np.cumsum(widths)
    offs = tuple((int(e - w), int(e)) for e, w in zip(ends, widths))
    row = lambda cols: pl.BlockSpec((ROW_TILE, cols), lambda i: (i, 0))
    return pl.pallas_call(
        functools.partial(_merge_kernel, d_model=d, alpha=alpha, offs=offs),
        grid=(m // ROW_TILE,),
        in_specs=[row(d), row(gates.shape[1])] + [row(w) for w in widths]
                 + [_const_spec(w_branch.shape), _const_spec(w_out.shape), _const_spec((1, d)), _const_spec((1, d))],
        out_specs=row(d),
        out_shape=jax.ShapeDtypeStruct((m, d), F32),
        compiler_params=_cparams(("parallel",)),
        name="merge_out_ln",
    )(x, gates, o_pool, o_ssm, o_conv, o_attn, w_branch, w_out, ln_g.reshape(1, d), ln_b.reshape(1, d))


def _ffn_kernel(x_ref, wu_ref, wd_ref, lg_ref, lb_ref, o_ref, *, alpha, chunk):
    x = x_ref[...]
    xb = x.astype(BF16)
    acc = alpha * x
    for c in range(wu_ref.shape[1] // chunk):
        hid = jnp.dot(xb, wu_ref[:, c * chunk:(c + 1) * chunk], preferred_element_type=F32)
        hid = jnp.square(jnp.maximum(hid, 0.0)).astype(BF16)
        acc = acc + jnp.dot(hid, wd_ref[c * chunk:(c + 1) * chunk, :], preferred_element_type=F32)
    o_ref[...] = _layer_norm(acc, lg_ref[...], lb_ref[...])


def _ffn(x, w_up, w_down, ln_g, ln_b, alpha):
    m, d = x.shape
    row = pl.BlockSpec((ROW_TILE, d), lambda i: (i, 0))
    return pl.pallas_call(
        functools.partial(_ffn_kernel, alpha=alpha, chunk=1024),
        grid=(m // ROW_TILE,),
        in_specs=[row, _const_spec(w_up.shape), _const_spec(w_down.shape), _const_spec((1, d)), _const_spec((1, d))],
        out_specs=row,
        out_shape=jax.ShapeDtypeStruct((m, d), F32),
        compiler_params=_cparams(("parallel",)),
        name="ffn_ln",
    )(x, w_up, w_down, ln_g.reshape(1, d), ln_b.reshape(1, d))


def kernel(x_prompt, x_sample, cache_k, cache_v, state_ssm_re, state_ssm_im, state_conv, state_pool, w_in, pool_w, pool_scale, ssm_a_re, ssm_a_im, ssm_log_dt, ssm_b_re, ssm_b_im, ssm_c_re, ssm_c_im, ssm_d, ssm_w_glu, conv_w, conv_b, lambda_q1, lambda_k1, lambda_q2, lambda_k2, subln_w, w_branch, w_out, ln1_g, ln1_b, w_up, w_down, ln2_g, ln2_b):
    depth = w_in.shape[0]
    bp, lp, d = x_prompt.shape
    bs, ls, _ = x_sample.shape
    past = cache_k.shape[2]
    heads, qk_dim = cache_k.shape[3], cache_k.shape[5]
    width_qk = heads * 2 * qk_dim
    pool_width, ssm_width, conv_width = pool_scale.shape[1], ssm_d.shape[1], conv_w.shape[2]
    n_mix = pool_width + ssm_width + 3 * conv_width
    alpha = float((2 * depth) ** 0.25)
    paths = ((bp, lp, 0, 0), (bs, ls, past, bp * lp))

    x = jnp.concatenate([x_prompt.reshape(bp * lp, d), x_sample.reshape(bs * ls, d)], axis=0)
    pos = jnp.concatenate([jnp.tile(jnp.arange(lp, dtype=jnp.int32), bp),
                           jnp.tile(past + jnp.arange(ls, dtype=jnp.int32), bs)])
    tables = _rope_tables(pos)

    cache_kt = jnp.transpose(cache_k, (0, 1, 3, 4, 5, 2)).reshape(depth, bs, width_qk, past)
    cache_vr = cache_v.reshape(depth, bs, past, width_qk)

    outs = [[[] for _ in range(6)] for _ in paths]
    kv_stack = None
    for l in range(depth):
        wl = w_in[l].astype(BF16)
        mix = _project(x, wl[:, :n_mix], F32, n_mix, sigmoid=False)
        gates = _project(x, wl[:, n_mix + 3 * width_qk:], BF16, 2048, sigmoid=True)
        ab_re, ab_im, bb_re, bb_im = _discretize(ssm_a_re[l], ssm_a_im[l], ssm_log_dt[l], ssm_b_re[l], ssm_b_im[l])
        ssm_w = _ssm_weights(ab_re, ab_im, bb_re, bb_im, ssm_c_re[l], ssm_c_im[l], halves=2)
        lam_p = jnp.stack([lambda_q1[l], lambda_k1[l], lambda_q2[l], lambda_k2[l]]).astype(F32)
        lam_init = 0.8 - 0.6 * math.exp(-0.3 * l)

        w_qkv = wl[:, n_mix:n_mix + 3 * width_qk]
        q_scale = float(qk_dim) ** -0.5 * math.log2(math.e)
        branch = []
        for pi, (bt, L, pos0, row_off) in enumerate(paths):
            if pi == 0:
                q, k_stack, v_stack, kbt, vb = _qkv_project_stacked(x, w_qkv, tables, bt, L, q_scale, l, depth, kv_stack)
                kv_stack = (k_stack, v_stack)
                hist_pool = jnp.zeros((bt, state_pool.shape[2], pool_width), F32)
                hist_conv = jnp.zeros((bt, state_conv.shape[2], conv_width), F32)
                h_re = h_im = jnp.zeros((bt,) + state_ssm_re.shape[2:], F32)
                o_attn = _attention(q, kbt, vb.reshape(bt, L, width_qk), lam_p, subln_w[l], bt, L, L, pos0, lam_init)
                kv_new = ()
            else:
                q, kf, vf, kb, vb = _qkv_project(x, w_qkv, tables, row_off, bt * L, q_scale)
                hist_pool, hist_conv, h_re, h_im = state_pool[l], state_conv[l], state_ssm_re[l], state_ssm_im[l]
                o_attn = _decode_attention(q, cache_kt, cache_vr, l, kb, vb, lam_p, subln_w[l], bt, L, past, lam_init)
                kv_new = (kf.reshape(bt, L, heads, 2, qk_dim), vf.reshape(bt, L, heads, 2 * qk_dim))
            o_pool, o_conv, new_pool, new_conv = _local_mixers(
                mix, hist_pool, hist_conv, pool_w[l], pool_scale[l], conv_w[l], conv_b[l], row_off, bt, L, pos0)
            o_ssm, new_re, new_im = _ssm_mixer(mix, h_re, h_im, ssm_w, ssm_d[l], ssm_w_glu[l], row_off, bt, L)
            branch.append((o_pool, o_ssm, o_conv, o_attn))
            for slot, val in zip(outs[pi], (new_re, new_im, new_conv, new_pool) + kv_new):
                slot.append(val)

        o_pool, o_ssm, o_conv, o_attn = (jnp.concatenate(parts, axis=0) for parts in zip(*branch))
        x = _merge(x, gates, o_pool, o_ssm, o_conv, o_attn, w_branch[l].astype(BF16), w_out[l].astype(BF16),
                   ln1_g[l], ln1_b[l], alpha)
        x = _ffn(x, w_up[l].astype(BF16), w_down[l].astype(BF16), ln2_g[l], ln2_b[l], alpha)

    y_prompt = x[:bp * lp].reshape(bp, lp, d)
    y_sample = x[bp * lp:].reshape(bs, ls, d)
    k_stack, v_stack = kv_stack
    k_prompt = jnp.transpose(k_stack.reshape(depth, bp, heads, 2, qk_dim, lp), (0, 1, 5, 2, 3, 4))
    v_prompt = v_stack.reshape(depth, bp, lp, heads, 2 * qk_dim)
    (p_re, p_im, p_conv, p_pool), (s_re, s_im, s_conv, s_pool, s_k, s_v) = (
        [jnp.stack(slot) for slot in path_outs if slot] for path_outs in outs)
    return (y_prompt, y_sample, k_prompt, v_prompt, p_re, p_im, p_conv, p_pool,
            s_k, s_v, s_re, s_im, s_conv, s_pool)
```

```python
import functools
import math

import numpy as np
import jax
import jax.numpy as jnp
from jax import lax
from jax.experimental import pallas as pl
from jax.experimental.pallas import tpu as pltpu

F32 = jnp.float32
BF16 = jnp.bfloat16

LANES = 128
CHUNK = 64
POOL_WINDOWS = (2, 4, 8, 16)
POOL_PAD = 16
CONV_PAD = 8
ROT_DIM = 16
ROPE_THETA = 500000.0
LN_EPS = 1e-5
VMEM_LIMIT = 56 * 1024 * 1024

ROW_TILE = 512
ATTN_TILE = 512
ATTN_Q_TILE = 1024
SSM_TILE = 128


def _cparams(sem):
    return pltpu.CompilerParams(dimension_semantics=sem, vmem_limit_bytes=VMEM_LIMIT)


def _const_spec(shape):
    zeros = (0,) * len(shape)
    return pl.BlockSpec(shape, lambda *_: zeros, pipeline_mode=pl.Buffered(1))


def _layer_norm(y, g, b):
    mu = jnp.mean(y, axis=-1, keepdims=True)
    d = y - mu
    var = jnp.mean(d * d, axis=-1, keepdims=True)
    return d * lax.rsqrt(var + LN_EPS) * g + b


def _proj_kernel(x_ref, w_ref, o_ref, *, sigmoid):
    y = jnp.dot(x_ref[...].astype(BF16), w_ref[...], preferred_element_type=F32)
    if sigmoid:
        y = jax.nn.sigmoid(y)
    o_ref[...] = y.astype(o_ref.dtype)


def _project(x, w, out_dtype, tn, sigmoid):
    m, k = x.shape
    n = w.shape[1]
    return pl.pallas_call(
        functools.partial(_proj_kernel, sigmoid=sigmoid),
        grid=(m // ROW_TILE, n // tn),
        in_specs=[pl.BlockSpec((ROW_TILE, k), lambda i, j: (i, 0)),
                  pl.BlockSpec((k, tn), lambda i, j: (0, j))],
        out_specs=pl.BlockSpec((ROW_TILE, tn), lambda i, j: (i, j)),
        out_shape=jax.ShapeDtypeStruct((m, n), out_dtype),
        compiler_params=_cparams(("parallel", "arbitrary")),
        name="proj_sigmoid" if sigmoid else "proj_plain",
    )(x, w)


def _rope(y, cos, sin_up, sin_dn):
    outs = []
    for c in range(y.shape[1] // LANES):
        yc = y[:, c * LANES:(c + 1) * LANES]
        outs.append(yc * cos + pltpu.roll(yc, LANES - ROT_DIM // 2, 1) * sin_up
                    + pltpu.roll(yc, ROT_DIM // 2, 1) * sin_dn)
    return jnp.concatenate(outs, axis=1)


def _qkv_kernel(x_ref, w_ref, cos_ref, su_ref, sd_ref, *refs, width, q_scale, stacked):
    q_ref, kf_ref, vf_ref, kb_ref, vb_ref = refs[-5:]
    xb = x_ref[...].astype(BF16)
    cos, su, sd = cos_ref[...], su_ref[...], sd_ref[...]
    q = jnp.dot(xb, w_ref[:, 0:width], preferred_element_type=F32)
    q_ref[...] = (_rope(q, cos, su, sd) * q_scale).astype(BF16)
    k = _rope(jnp.dot(xb, w_ref[:, width:2 * width], preferred_element_type=F32), cos, su, sd)
    v = jnp.dot(xb, w_ref[:, 2 * width:3 * width], preferred_element_type=F32)
    if stacked:
        kt = k.T
        kf_ref[0, 0] = kt
        kb_ref[0] = kt.astype(BF16)
        vf_ref[0] = v
    else:
        kf_ref[...] = k
        kb_ref[...] = k.astype(BF16)
        vf_ref[...] = v
    vb_ref[...] = v.astype(BF16)


def _qkv_project(x_all, w_qkv, tables, row_off, rows, q_scale):
    k = x_all.shape[1]
    width = w_qkv.shape[1] // 3
    tm = min(ROW_TILE, rows)
    blk0 = row_off // tm
    row_spec = lambda cols: pl.BlockSpec((tm, cols), lambda i: (i, 0))
    tab_spec = pl.BlockSpec((tm, LANES), lambda i: (blk0 + i, 0))
    return pl.pallas_call(
        functools.partial(_qkv_kernel, width=width, q_scale=q_scale, stacked=False),
        grid=(rows // tm,),
        in_specs=[pl.BlockSpec((tm, k), lambda i: (blk0 + i, 0)), _const_spec(w_qkv.shape),
                  tab_spec, tab_spec, tab_spec],
        out_specs=[row_spec(width)] * 5,
        out_shape=[jax.ShapeDtypeStruct((rows, width), BF16),
                   jax.ShapeDtypeStruct((rows, width), F32), jax.ShapeDtypeStruct((rows, width), F32),
                   jax.ShapeDtypeStruct((rows, width), BF16), jax.ShapeDtypeStruct((rows, width), BF16)],
        compiler_params=_cparams(("parallel",)),
        name="proj_qkv",
    )(x_all, w_qkv, *tables)


def _qkv_project_stacked(x_all, w_qkv, tables, bt, L, q_scale, layer, depth, prev):
    k = x_all.shape[1]
    width = w_qkv.shape[1] // 3
    rows = bt * L
    tm = min(ROW_TILE, L)
    nl = L // tm
    tab_spec = pl.BlockSpec((tm, LANES), lambda i: (i, 0))
    row_spec = pl.BlockSpec((tm, width), lambda i: (i, 0))
    any_spec = pl.BlockSpec(memory_space=pl.ANY)
    n_prev = 0 if prev is None else 2
    return pl.pallas_call(
        functools.partial(_qkv_kernel, width=width, q_scale=q_scale, stacked=True),
        grid=(rows // tm,),
        in_specs=[pl.BlockSpec((tm, k), lambda i: (i, 0)), _const_spec(w_qkv.shape),
                  tab_spec, tab_spec, tab_spec] + [any_spec] * n_prev,
        out_specs=[row_spec,
                   pl.BlockSpec((1, 1, width, tm), lambda i: (layer, i // nl, 0, i % nl)),
                   pl.BlockSpec((1, tm, width), lambda i: (layer, i, 0)),
                   pl.BlockSpec((1, width, tm), lambda i: (i // nl, 0, i % nl)),
                   row_spec],
        out_shape=[jax.ShapeDtypeStruct((rows, width), BF16),
                   jax.ShapeDtypeStruct((depth, bt, width, L), F32), jax.ShapeDtypeStruct((depth, rows, width), F32),
                   jax.ShapeDtypeStruct((bt, width, L), BF16), jax.ShapeDtypeStruct((rows, width), BF16)],
        input_output_aliases={} if prev is None else {5: 1, 6: 2},
        compiler_params=_cparams(("parallel",)),
        name="proj_qkv_stacked",
    )(x_all, w_qkv, *tables, *(() if prev is None else prev))


def _rope_tables(pos):
    half = ROT_DIM // 2
    inv = ROPE_THETA ** (-jnp.arange(0, ROT_DIM, 2, dtype=F32) / ROT_DIM)
    ang = pos.astype(F32)[:, None] * inv[None, :]
    cos, sin = jnp.cos(ang), jnp.sin(ang)
    n = pos.shape[0]
    ones = jnp.ones((n, 64 - ROT_DIM), F32)
    zeros = jnp.zeros((n, 64 - half), F32)
    cos64 = jnp.concatenate([cos, cos, ones], axis=1)
    up64 = jnp.concatenate([-sin, zeros], axis=1)
    dn64 = jnp.concatenate([jnp.zeros((n, half), F32), sin, jnp.zeros((n, 64 - ROT_DIM), F32)], axis=1)
    return tuple(jnp.concatenate([t, t], axis=1) for t in (cos64, up64, dn64))


def _local_kernel(z_ref, h_ref, b_ref, c_ref, hp_ref, hc_ref, pw_ref, ps_ref, cw_ref, cb_ref,
                  op_ref, oc_ref, np_ref, nc_ref, zbuf, cbuf, *, tl, pos0):
    i = pl.program_id(1)
    last = pl.num_programs(1) - 1

    @pl.when(i == 0)
    def _():
        zbuf[0:POOL_PAD, :] = hp_ref[0]
        cbuf[0:CONV_PAD, :] = hc_ref[0]

    @pl.when(i > 0)
    def _():
        zbuf[0:POOL_PAD, :] = zbuf[tl:tl + POOL_PAD, :]
        cbuf[0:CONV_PAD, :] = cbuf[tl:tl + CONV_PAD, :]

    z = z_ref[...]
    zbuf[POOL_PAD:POOL_PAD + tl, :] = z
    cbuf[CONV_PAD:CONV_PAD + tl, :] = c_ref[...] * h_ref[...]

    pos = pos0 + i * tl + lax.broadcasted_iota(jnp.int32, (tl, 1), 0)
    gw = z.shape[1] // len(POOL_WINDOWS)
    outs = []
    for g, w in enumerate(POOL_WINDOWS):
        cols = slice(g * gw, (g + 1) * gw)
        s = z[:, cols]
        for j in range(1, w):
            s = s + zbuf[POOL_PAD - j:POOL_PAD - j + tl, cols]
        cnt = jnp.minimum(w, pos + 1).astype(F32)
        u = s / cnt - z[:, cols]
        outs.append(jnp.dot(u.astype(BF16), pw_ref[g], preferred_element_type=F32))
    op_ref[...] = (jnp.concatenate(outs, axis=1) * ps_ref[...]).astype(op_ref.dtype)

    y = cb_ref[...] + cbuf[CONV_PAD - 2:CONV_PAD - 2 + tl, :] * cw_ref[0:1, :]
    y = y + cbuf[CONV_PAD - 1:CONV_PAD - 1 + tl, :] * cw_ref[1:2, :]
    y = y + cbuf[CONV_PAD:CONV_PAD + tl, :] * cw_ref[2:3, :]
    oc_ref[...] = (b_ref[...] * y).astype(oc_ref.dtype)

    @pl.when(i == last)
    def _():
        np_ref[0] = zbuf[tl + 1:tl + POOL_PAD, :]
        nc_ref[0] = cbuf[tl + CONV_PAD - 2:tl + CONV_PAD, :]


def _local_mixers(mix, hist_pool, hist_conv, pool_w, pool_scale, conv_w, conv_b, row_off, bt, L, pos0):
    width = pool_scale.shape[-1]
    tl = min(ROW_TILE, L)
    nl = L // tl
    blk0 = row_off // tl
    col_spec = lambda cb: pl.BlockSpec((tl, width), lambda b, i: (blk0 + b * nl + i, cb))
    out_spec = pl.BlockSpec((tl, width), lambda b, i: (b * nl + i, 0))
    hp = jnp.pad(hist_pool, ((0, 0), (POOL_PAD - hist_pool.shape[1], 0), (0, 0)))
    hc = jnp.pad(hist_conv, ((0, 0), (CONV_PAD - hist_conv.shape[1], 0), (0, 0)))
    n_hp, n_hc = hist_pool.shape[1], hist_conv.shape[1]
    return pl.pallas_call(
        functools.partial(_local_kernel, tl=tl, pos0=pos0),
        grid=(bt, nl),
        in_specs=[col_spec(0), col_spec(2), col_spec(3), col_spec(4),
                  pl.BlockSpec((1, POOL_PAD, width), lambda b, i: (b, 0, 0)),
                  pl.BlockSpec((1, CONV_PAD, width), lambda b, i: (b, 0, 0)),
                  _const_spec(pool_w.shape), _const_spec((1, width)),
                  _const_spec(conv_w.shape), _const_spec((1, width))],
        out_specs=[out_spec, out_spec,
                   pl.BlockSpec((1, n_hp, width), lambda b, i: (b, 0, 0)),
                   pl.BlockSpec((1, n_hc, width), lambda b, i: (b, 0, 0))],
        out_shape=[jax.ShapeDtypeStruct((bt * L, width), BF16), jax.ShapeDtypeStruct((bt * L, width), BF16),
                   jax.ShapeDtypeStruct((bt, n_hp, width), F32), jax.ShapeDtypeStruct((bt, n_hc, width), F32)],
        scratch_shapes=[pltpu.VMEM((POOL_PAD + tl, width), F32), pltpu.VMEM((CONV_PAD + tl, width), F32)],
        compiler_params=_cparams(("parallel", "arbitrary")),
        name="local_mixers",
    )(mix, mix, mix, mix, hp, hc, pool_w.astype(BF16), pool_scale.reshape(1, width),
      conv_w, conv_b.reshape(1, width))


def _discretize_kernel(are_ref, aim_ref, ldt_ref, bre_ref, bim_ref, abr_ref, abi_ref, bbr_ref, bbi_ref):
    a_re, a_im = are_ref[...], aim_ref[...]
    dt = jnp.exp(ldt_ref[...])
    mag = jnp.exp(a_re * dt)
    ab_re = mag * jnp.cos(a_im * dt)
    ab_im = mag * jnp.sin(a_im * dt)
    den = a_re * a_re + a_im * a_im
    cr = ((ab_re - 1.0) * a_re + ab_im * a_im) / den
    ci = (ab_im * a_re - (ab_re - 1.0) * a_im) / den
    b_re, b_im = bre_ref[...], bim_ref[...]
    abr_ref[...] = ab_re
    abi_ref[...] = ab_im
    bbr_ref[...] = cr * b_re - ci * b_im
    bbi_ref[...] = cr * b_im + ci * b_re


def _discretize(a_re, a_im, log_dt, b_re, b_im):
    g, p = a_re.shape
    n = b_re.shape[-1]
    col = lambda t: t.reshape(g * p, 1)
    ldt = jnp.broadcast_to(log_dt[:, None], (g, p))
    shapes = [jax.ShapeDtypeStruct((g * p, 1), F32)] * 2 + [jax.ShapeDtypeStruct((g * p, n), F32)] * 2
    return pl.pallas_call(_discretize_kernel, out_shape=shapes, name="ssm_discretize")(
        col(a_re), col(a_im), col(ldt), b_re.reshape(g * p, n), b_im.reshape(g * p, n))


def _shift_rows(x, d):
    rows = x.shape[0]
    if d % 8 == 0:
        return jnp.concatenate([jnp.zeros((d, x.shape[1]), x.dtype), x[:rows - d]], axis=0)
    keep = lax.broadcasted_iota(jnp.int32, (rows, 1), 0) >= d
    return jnp.where(keep, pltpu.roll(x, d, 0), 0.0)


def _ssm_kernel(u_ref, hre_ref, him_ref, are_ref, aim_ref, bcat_ref, ccat_ref, d_ref, wg_ref,
                o_ref, nre_ref, nim_ref, s_ref, cre_ref, cim_ref, *, tl, halves, chunks):
    i = pl.program_id(1)

    @pl.when(i == 0)
    def _():
        cre_ref[...] = hre_ref[0]
        cim_ref[...] = him_ref[0]

    u = u_ref[...]
    ub = u.astype(BF16)
    kw = ub.shape[1] // halves
    per_half = 2 * chunks
    for h in range(halves):
        bu = jnp.dot(ub[:, h * kw:(h + 1) * kw], bcat_ref[h], preferred_element_type=F32)
        for q in range(per_half):
            s_ref[h * per_half + q] = bu[:, q * LANES:(q + 1) * LANES]

    row0 = lax.broadcasted_iota(jnp.int32, (tl, 1), 0) == 0

    def scan_chunk(j, carry):
        ire = (j // chunks) * per_half + (j % chunks)
        iim = ire + chunks
        xr, xi = s_ref[ire], s_ref[iim]
        ar, ai = are_ref[j], aim_ref[j]
        pr, pi = cre_ref[j], cim_ref[j]
        xr = xr + jnp.where(row0, ar * pr - ai * pi, 0.0)
        xi = xi + jnp.where(row0, ar * pi + ai * pr, 0.0)
        d = 1
        while d < tl:
            sr, si = _shift_rows(xr, d), _shift_rows(xi, d)
            xr, xi = xr + ar * sr - ai * si, xi + ar * si + ai * sr
            ar, ai = ar * ar - ai * ai, 2.0 * ar * ai
            d *= 2
        s_ref[ire] = xr
        s_ref[iim] = xi
        cre_ref[j] = xr[tl - 1:tl]
        cim_ref[j] = xi[tl - 1:tl]
        return carry

    lax.fori_loop(0, halves * chunks, scan_chunk, 0)

    ys = []
    for h in range(halves):
        st = jnp.concatenate([s_ref[h * per_half + q] for q in range(per_half)], axis=1)
        ys.append(jnp.dot(st.astype(BF16), ccat_ref[h], preferred_element_type=F32))
    y = jnp.concatenate(ys, axis=1) + d_ref[...] * u
    v = 0.5 * y * (1.0 + jnp.tanh(math.sqrt(2.0 / math.pi) * (y + 0.044715 * (y * y * y))))
    gate = jax.nn.sigmoid(jnp.dot(v.astype(BF16), wg_ref[...], preferred_element_type=F32))
    o_ref[...] = (v * gate).astype(o_ref.dtype)

    @pl.when(i == pl.num_programs(1) - 1)
    def _():
        nre_ref[0] = cre_ref[...]
        nim_ref[0] = cim_ref[...]


def _ssm_weights(ab_re, ab_im, bb_re, bb_im, c_re, c_im, halves):
    g, n, p = c_re.shape
    gh = g // halves
    eye = jnp.eye(gh, dtype=F32)
    bcat, ccat = [], []
    for h in range(halves):
        sl = slice(h * gh, (h + 1) * gh)
        dense_b = lambda t: jnp.einsum('gpn,gk->gnkp', t.reshape(g, p, n)[sl], eye).reshape(gh * n, gh * p)
        dense_c = lambda t: jnp.einsum('gnp,gk->gpkn', t[sl], eye).reshape(gh * p, gh * n)
        bcat.append(jnp.concatenate([dense_b(bb_re), dense_b(bb_im)], axis=1))
        ccat.append(jnp.concatenate([dense_c(c_re), -dense_c(c_im)], axis=0))
    nch = g * p // LANES
    return (ab_re.reshape(nch, 1, LANES), ab_im.reshape(nch, 1, LANES),
            jnp.stack(bcat).astype(BF16), jnp.stack(ccat).astype(BF16))


def _ssm_mixer(mix, h_re, h_im, ssm_w, d_skip, w_glu, row_off, bt, L):
    a_re, a_im, bcat, ccat = ssm_w
    halves = bcat.shape[0]
    width = d_skip.shape[-1]
    nch = a_re.shape[0]
    chunks = nch // halves
    g, p = h_re.shape[1], h_re.shape[2]
    tl = min(SSM_TILE, L)
    nl = L // tl
    blk0 = row_off // tl
    state_spec = pl.BlockSpec((1, nch, 1, LANES), lambda b, i: (b, 0, 0, 0))
    o, n_re, n_im = pl.pallas_call(
        functools.partial(_ssm_kernel, tl=tl, halves=halves, chunks=chunks),
        grid=(bt, nl),
        in_specs=[pl.BlockSpec((tl, width), lambda b, i: (blk0 + b * nl + i, 1)),
                  state_spec, state_spec,
                  _const_spec(a_re.shape), _const_spec(a_im.shape),
                  _const_spec(bcat.shape), _const_spec(ccat.shape),
                  _const_spec((1, width)), _const_spec(w_glu.shape)],
        out_specs=[pl.BlockSpec((tl, width), lambda b, i: (b * nl + i, 0)), state_spec, state_spec],
        out_shape=[jax.ShapeDtypeStruct((bt * L, width), BF16),
                   jax.ShapeDtypeStruct((bt, nch, 1, LANES), F32), jax.ShapeDtypeStruct((bt, nch, 1, LANES), F32)],
        scratch_shapes=[pltpu.VMEM((2 * nch, tl, LANES), F32),
                        pltpu.VMEM((nch, 1, LANES), F32), pltpu.VMEM((nch, 1, LANES), F32)],
        compiler_params=_cparams(("parallel", "arbitrary")),
        name="ssm_mixer",
    )(mix, h_re.astype(F32).reshape(bt, nch, 1, LANES), h_im.astype(F32).reshape(bt, nch, 1, LANES),
      a_re, a_im, bcat, ccat, d_skip.reshape(1, width), w_glu.astype(BF16))
    return o, n_re.reshape(bt, g, p), n_im.reshape(bt, g, p)


def _attn_init(m_ref, l_ref, acc_ref):
    m_ref[...] = jnp.full(m_ref.shape, -jnp.inf, F32)
    l_ref[...] = jnp.zeros(l_ref.shape, F32)
    acc_ref[...] = jnp.zeros(acc_ref.shape, F32)


def _attn_update(q_of, k_of, v_of, visible, heads, m_ref, l_ref, acc_ref):
    low_lanes = lax.broadcasted_iota(jnp.int32, (1, LANES), 1) < (LANES // 2)
    if visible is not None:
        visible = jnp.concatenate([visible, visible], axis=0)
    for h in range(heads):
        qh, kh, vh = q_of(h), k_of(h), v_of(h)
        zero = jnp.zeros_like(qh)
        qm = jnp.concatenate([jnp.where(low_lanes, qh, zero), jnp.where(low_lanes, zero, qh)], axis=0)
        s = jnp.dot(qm, kh, preferred_element_type=F32)
        if visible is not None:
            s = jnp.where(visible, s, -jnp.inf)
        m_old = m_ref[h]
        m_new = jnp.maximum(m_old, jnp.max(s, axis=1, keepdims=True))
        alpha = jnp.exp2(m_old - m_new)
        ps = [jnp.exp2(s[:, j * LANES:(j + 1) * LANES] - m_new) for j in range(s.shape[1] // LANES)]
        lsum = ps[0]
        for pj in ps[1:]:
            lsum = lsum + pj
        l_ref[h] = alpha * l_ref[h] + lsum
        p = jnp.concatenate([pj.astype(BF16) for pj in ps], axis=1)
        acc_ref[h] = alpha * acc_ref[h] + jnp.dot(p, vh, preferred_element_type=F32)
        m_ref[h] = m_new


def _attn_finalize(lam_ref, sw_ref, o_ref, l_ref, acc_ref, heads, lam_init):
    lp = lam_ref[...]
    lam = (jnp.exp(jnp.sum(lp[0:1] * lp[1:2], axis=1, keepdims=True))
           - jnp.exp(jnp.sum(lp[2:3] * lp[3:4], axis=1, keepdims=True)) + lam_init)
    rows = o_ref.shape[1]
    for h in range(heads):
        oh = acc_ref[h] / jnp.sum(l_ref[h], axis=1, keepdims=True)
        o = oh[:rows] - lam * oh[rows:]
        o = o * lax.rsqrt(jnp.mean(o * o, axis=1, keepdims=True) + LN_EPS) * sw_ref[...] * (1.0 - lam_init)
        o_ref[0, :, h * LANES:(h + 1) * LANES] = o.astype(o_ref.dtype)


def _head_cols(ref):
    lead = (0,) * (len(ref.shape) - 2)
    return lambda h: ref[(*lead, slice(None), slice(h * LANES, (h + 1) * LANES))].astype(BF16)


def _head_rows(ref):
    lead = (0,) * (len(ref.shape) - 2)
    return lambda h: ref[(*lead, slice(h * LANES, (h + 1) * LANES), slice(None))].astype(BF16)


def _attn_kernel(qi_ref, ki_ref, fl_ref, q_ref, k_ref, v_ref, lam_ref, sw_ref, o_ref, m_ref, l_ref, acc_ref,
                 *, tq, tk, heads, q_pos0, lk, lam_init):
    step = pl.program_id(1)
    qi, ki, fl = qi_ref[step], ki_ref[step], fl_ref[step]

    @pl.when(ki == 0)
    def _():
        _attn_init(m_ref, l_ref, acc_ref)

    def accumulate(masked):
        visible = None
        if masked:
            q_pos = q_pos0 + qi * tq + lax.broadcasted_iota(jnp.int32, (tq, 1), 0)
            k_pos = ki * tk + lax.broadcasted_iota(jnp.int32, (1, tk), 1)
            visible = (k_pos < (q_pos // CHUNK + 1) * CHUNK) & (k_pos < lk)
        _attn_update(_head_cols(q_ref), _head_rows(k_ref), _head_cols(v_ref), visible, heads, m_ref, l_ref, acc_ref)

    @pl.when((fl & 1) == 0)
    def _():
        accumulate(False)

    @pl.when((fl & 1) != 0)
    def _():
        accumulate(True)

    @pl.when((fl & 2) != 0)
    def _():
        _attn_finalize(lam_ref, sw_ref, o_ref, l_ref, acc_ref, heads, lam_init)


def _decode_attn_kernel(q_ref, ck_ref, cv_ref, nk_ref, nv_ref, lam_ref, sw_ref, o_ref, m_ref, l_ref, acc_ref,
                        *, tq, heads, past, n_cache, lam_init):
    j = pl.program_id(1)

    @pl.when(j == 0)
    def _():
        _attn_init(m_ref, l_ref, acc_ref)

    @pl.when(j < n_cache)
    def _():
        cached_v = lambda h: cv_ref[0, 0, :, h, :].astype(BF16)
        _attn_update(_head_cols(q_ref), _head_rows(ck_ref), cached_v, None, heads, m_ref, l_ref, acc_ref)

    @pl.when(j == n_cache)
    def _():
        nk = nk_ref.shape[2]
        t = lax.broadcasted_iota(jnp.int32, (tq, 1), 0)
        i = lax.broadcasted_iota(jnp.int32, (1, nk), 1)
        visible = (past + i < ((past + t) // CHUNK + 1) * CHUNK) & (i < tq)
        _attn_update(_head_cols(q_ref), _head_rows(nk_ref), _head_cols(nv_ref), visible, heads, m_ref, l_ref, acc_ref)
        _attn_finalize(lam_ref, sw_ref, o_ref, l_ref, acc_ref, heads, lam_init)


def _attn_schedule(L, lk, tq, tk, q_pos0):
    qi, ki, fl = [], [], []
    for a in range(L // tq):
        first_end = ((q_pos0 + a * tq) // CHUNK + 1) * CHUNK
        last_end = min(((q_pos0 + a * tq + tq - 1) // CHUNK + 1) * CHUNK, lk)
        nk = -(-last_end // tk)
        for b in range(nk):
            full = (b + 1) * tk <= min(first_end, lk)
            qi.append(a); ki.append(b); fl.append((0 if full else 1) | (2 if b == nk - 1 else 0))
    return tuple(jnp.asarray(np.asarray(t, np.int32)) for t in (qi, ki, fl))


def _attention(q, kt_all, v_all, lam_p, subln_w, bt, L, lk, q_pos0, lam_init):
    width = q.shape[-1]
    heads = width // LANES
    tq = min(ATTN_Q_TILE, L)
    tk = ATTN_TILE
    lk_pad = kt_all.shape[2]
    qi, ki, fl = _attn_schedule(L, lk, tq, tk, q_pos0)
    grid_spec = pltpu.PrefetchScalarGridSpec(
        num_scalar_prefetch=3,
        grid=(bt, int(qi.shape[0])),
        in_specs=[pl.BlockSpec((1, tq, width), lambda b, s, qi, ki, fl: (b, qi[s], 0)),
                  pl.BlockSpec((1, width, tk), lambda b, s, qi, ki, fl: (b, 0, ki[s])),
                  pl.BlockSpec((1, tk, width), lambda b, s, qi, ki, fl: (b, ki[s], 0)),
                  pl.BlockSpec(lam_p.shape, lambda b, s, qi, ki, fl: (0, 0)),
                  pl.BlockSpec((1, LANES), lambda b, s, qi, ki, fl: (0, 0))],
        out_specs=pl.BlockSpec((1, tq, width), lambda b, s, qi, ki, fl: (b, qi[s], 0)),
        scratch_shapes=[pltpu.VMEM((heads, 2 * tq, LANES), F32)] * 3)
    assert lk_pad % tk == 0 and L % tq == 0
    out = pl.pallas_call(
        functools.partial(_attn_kernel, tq=tq, tk=tk, heads=heads, q_pos0=q_pos0, lk=lk, lam_init=lam_init),
        grid_spec=grid_spec,
        out_shape=jax.ShapeDtypeStruct((bt, L, width), BF16),
        compiler_params=_cparams(("parallel", "arbitrary")),
        name="diff_attention",
    )(qi, ki, fl, q.reshape(bt, L, width), kt_all, v_all, lam_p, subln_w.reshape(1, LANES))
    return out.reshape(bt * L, width)


def _decode_attention(q, cache_kt, cache_v, layer, new_k, new_v, lam_p, subln_w, bt, L, past, lam_init):
    width = q.shape[-1]
    heads = width // LANES
    tk = ATTN_TILE
    assert past % tk == 0 and past >= tk and L <= LANES and past % CHUNK == 0
    n_cache = past // tk
    new_kt = jnp.pad(jnp.swapaxes(new_k.reshape(bt, L, width), 1, 2), ((0, 0), (0, 0), (0, LANES - L)))
    new_v = jnp.pad(new_v.reshape(bt, L, width), ((0, 0), (0, LANES - L), (0, 0)))
    row_spec = pl.BlockSpec((1, L, width), lambda b, j: (b, 0, 0))
    scratch = pltpu.VMEM((heads, 2 * L, LANES), F32)
    out = pl.pallas_call(
        functools.partial(_decode_attn_kernel, tq=L, heads=heads, past=past, n_cache=n_cache, lam_init=lam_init),
        grid=(bt, n_cache + 1),
        in_specs=[row_spec,
                  pl.BlockSpec((1, 1, width, tk), lambda b, j: (layer, b, 0, jnp.minimum(j, n_cache - 1))),
                  pl.BlockSpec((1, 1, tk, heads, LANES), lambda b, j: (layer, b, jnp.minimum(j, n_cache - 1), 0, 0)),
                  pl.BlockSpec((1, width, LANES), lambda b, j: (b, 0, 0)),
                  pl.BlockSpec((1, LANES, width), lambda b, j: (b, 0, 0)),
                  pl.BlockSpec(lam_p.shape, lambda b, j: (0, 0)), pl.BlockSpec((1, LANES), lambda b, j: (0, 0))],
        out_specs=row_spec,
        out_shape=jax.ShapeDtypeStruct((bt, L, width), BF16),
        scratch_shapes=[scratch, scratch, scratch],
        compiler_params=_cparams(("parallel", "arbitrary")),
        name="decode_attention",
    )(q.reshape(bt, L, width), cache_kt, cache_v, new_kt, new_v, lam_p, subln_w.reshape(1, LANES))
    return out.reshape(bt * L, width)


def _merge_kernel(x_ref, g_ref, op_ref, os_ref, oc_ref, oa_ref, wb_ref, wo_ref, lg_ref, lb_ref, o_ref,
                  *, d_model, alpha, offs):
    merged = None
    for b, (o_b, (lo, hi)) in enumerate(zip((op_ref, os_ref, oc_ref, oa_ref), offs)):
        t = jnp.dot(o_b[...], wb_ref[lo:hi, :], preferred_element_type=F32)
        t = t * g_ref[:, b * d_model:(b + 1) * d_model].astype(F32)
        merged = t if merged is None else merged + t
    y = alpha * x_ref[...] + jnp.dot(merged.astype(BF16), wo_ref[...], preferred_element_type=F32)
    o_ref[...] = _layer_norm(y, lg_ref[...], lb_ref[...])


def _merge(x, gates, o_pool, o_ssm, o_conv, o_attn, w_branch, w_out, ln_g, ln_b, alpha):
    m, d = x.shape
    widths = [o_pool.shape[1], o_ssm.shape[1], o_conv.shape[1], o_attn.shape[1]]
    ends = np.cumsum(widths)
    offs = tuple((int(e - w), int(e)) for e, w in zip(ends, widths))
    row = lambda cols: pl.BlockSpec((ROW_TILE, cols), lambda i: (i, 0))
    return pl.pallas_call(
        functools.partial(_merge_kernel, d_model=d, alpha=alpha, offs=offs),
        grid=(m // ROW_TILE,),
        in_specs=[row(d), row(gates.shape[1])] + [row(w) for w in widths]
                 + [_const_spec(w_branch.shape), _const_spec(w_out.shape), _const_spec((1, d)), _const_spec((1, d))],
        out_specs=row(d),
        out_shape=jax.ShapeDtypeStruct((m, d), F32),
        compiler_params=_cparams(("parallel",)),
        name="merge_out_ln",
    )(x, gates, o_pool, o_ssm, o_conv, o_attn, w_branch, w_out, ln_g.reshape(1, d), ln_b.reshape(1, d))


def _ffn_kernel(x_ref, wu_ref, wd_ref, lg_ref, lb_ref, o_ref, *, alpha, chunk):
    x = x_ref[...]
    xb = x.astype(BF16)
    acc = alpha * x
    for c in range(wu_ref.shape[1] // chunk):
        hid = jnp.dot(xb, wu_ref[:, c * chunk:(c + 1) * chunk], preferred_element_type=F32)
        hid = jnp.square(jnp.maximum(hid, 0.0)).astype(BF16)
        acc = acc + jnp.dot(hid, wd_ref[c * chunk:(c + 1) * chunk, :], preferred_element_type=F32)
    o_ref[...] = _layer_norm(acc, lg_ref[...], lb_ref[...])


def _ffn(x, w_up, w_down, ln_g, ln_b, alpha):
    m, d = x.shape
    row = pl.BlockSpec((ROW_TILE, d), lambda i: (i, 0))
    return pl.pallas_call(
        functools.partial(_ffn_kernel, alpha=alpha, chunk=1024),
        grid=(m // ROW_TILE,),
        in_specs=[row, _const_spec(w_up.shape), _const_spec(w_down.shape), _const_spec((1, d)), _const_spec((1, d))],
        out_specs=row,
        out_shape=jax.ShapeDtypeStruct((m, d), F32),
        compiler_params=_cparams(("parallel",)),
        name="ffn_ln",
    )(x, w_up, w_down, ln_g.reshape(1, d), ln_b.reshape(1, d))


def kernel(x_prompt, x_sample, cache_k, cache_v, state_ssm_re, state_ssm_im, state_conv, state_pool, w_in, pool_w, pool_scale, ssm_a_re, ssm_a_im, ssm_log_dt, ssm_b_re, ssm_b_im, ssm_c_re, ssm_c_im, ssm_d, ssm_w_glu, conv_w, conv_b, lambda_q1, lambda_k1, lambda_q2, lambda_k2, subln_w, w_branch, w_out, ln1_g, ln1_b, w_up, w_down, ln2_g, ln2_b):
    depth = w_in.shape[0]
    bp, lp, d = x_prompt.shape
    bs, ls, _ = x_sample.shape
    past = cache_k.shape[2]
    heads, qk_dim = cache_k.shape[3], cache_k.shape[5]
    width_qk = heads * 2 * qk_dim
    pool_width, ssm_width, conv_width = pool_scale.shape[1], ssm_d.shape[1], conv_w.shape[2]
    n_mix = pool_width + ssm_width + 3 * conv_width
    alpha = float((2 * depth) ** 0.25)
    paths = ((bp, lp, 0, 0), (bs, ls, past, bp * lp))

    x = jnp.concatenate([x_prompt.reshape(bp * lp, d), x_sample.reshape(bs * ls, d)], axis=0)
    pos = jnp.concatenate([jnp.tile(jnp.arange(lp, dtype=jnp.int32), bp),
                           jnp.tile(past + jnp.arange(ls, dtype=jnp.int32), bs)])
    tables = _rope_tables(pos)

    cache_kt = jnp.transpose(cache_k, (0, 1, 3, 4, 5, 2)).reshape(depth, bs, width_qk, past)

    outs = [[[] for _ in range(6)] for _ in paths]
    kv_stack = None
    for l in range(depth):
        wl = w_in[l].astype(BF16)
        mix = _project(x, wl[:, :n_mix], F32, n_mix, sigmoid=False)
        gates = _project(x, wl[:, n_mix + 3 * width_qk:], BF16, 2048, sigmoid=True)
        ab_re, ab_im, bb_re, bb_im = _discretize(ssm_a_re[l], ssm_a_im[l], ssm_log_dt[l], ssm_b_re[l], ssm_b_im[l])
        ssm_w = _ssm_weights(ab_re, ab_im, bb_re, bb_im, ssm_c_re[l], ssm_c_im[l], halves=2)
        lam_p = jnp.stack([lambda_q1[l], lambda_k1[l], lambda_q2[l], lambda_k2[l]]).astype(F32)
        lam_init = 0.8 - 0.6 * math.exp(-0.3 * l)

        w_qkv = wl[:, n_mix:n_mix + 3 * width_qk]
        q_scale = float(qk_dim) ** -0.5 * math.log2(math.e)
        branch = []
        for pi, (bt, L, pos0, row_off) in enumerate(paths):
            if pi == 0:
                q, k_stack, v_stack, kbt, vb = _qkv_project_stacked(x, w_qkv, tables, bt, L, q_scale, l, depth, kv_stack)
                kv_stack = (k_stack, v_stack)
                hist_pool = jnp.zeros((bt, state_pool.shape[2], pool_width), F32)
                hist_conv = jnp.zeros((bt, state_conv.shape[2], conv_width), F32)
                h_re = h_im = jnp.zeros((bt,) + state_ssm_re.shape[2:], F32)
                o_attn = _attention(q, kbt, vb.reshape(bt, L, width_qk), lam_p, subln_w[l], bt, L, L, pos0, lam_init)
                kv_new = ()
            else:
                q, kf, vf, kb, vb = _qkv_project(x, w_qkv, tables, row_off, bt * L, q_scale)
                hist_pool, hist_conv, h_re, h_im = state_pool[l], state_conv[l], state_ssm_re[l], state_ssm_im[l]
                o_attn = _decode_attention(q, cache_kt, cache_v, l, kb, vb, lam_p, subln_w[l], bt, L, past, lam_init)
                kv_new = (kf.reshape(bt, L, heads, 2, qk_dim), vf.reshape(bt, L, heads, 2 * qk_dim))
            o_pool, o_conv, new_pool, new_conv = _local_mixers(
                mix, hist_pool, hist_conv, pool_w[l], pool_scale[l], conv_w[l], conv_b[l], row_off, bt, L, pos0)
            o_ssm, new_re, new_im = _ssm_mixer(mix, h_re, h_im, ssm_w, ssm_d[l], ssm_w_glu[l], row_off, bt, L)
            branch.append((o_pool, o_ssm, o_conv, o_attn))
            for slot, val in zip(outs[pi], (new_re, new_im, new_conv, new_pool) + kv_new):
                slot.append(val)

        o_pool, o_ssm, o_conv, o_attn = (jnp.concatenate(parts, axis=0) for parts in zip(*branch))
        x = _merge(x, gates, o_pool, o_ssm, o_conv, o_attn, w_branch[l].astype(BF16), w_out[l].astype(BF16),
                   ln1_g[l], ln1_b[l], alpha)
        x = _ffn(x, w_up[l].astype(BF16), w_down[l].astype(BF16), ln2_g[l], ln2_b[l], alpha)

    y_prompt = x[:bp * lp].reshape(bp, lp, d)
    y_sample = x[bp * lp:].reshape(bs, ls, d)
    k_stack, v_stack = kv_stack
    k_prompt = jnp.transpose(k_stack.reshape(depth, bp, heads, 2, qk_dim, lp), (0, 1, 5, 2, 3, 4))
    v_prompt = v_stack.reshape(depth, bp, lp, heads, 2 * qk_dim)
    (p_re, p_im, p_conv, p_pool), (s_re, s_im, s_conv, s_pool, s_k, s_v) = (
        [jnp.stack(slot) for slot in path_outs if slot] for path_outs in outs)
    return (y_prompt, y_sample, k_prompt, v_prompt, p_re, p_im, p_conv, p_pool,
            s_k, s_v, s_re, s_im, s_conv, s_pool)
```

```python
import functools
import math

import numpy as np
import jax
import jax.numpy as jnp
from jax import lax
from jax.experimental import pallas as pl
from jax.experimental.pallas import tpu as pltpu

F32 = jnp.float32
BF16 = jnp.bfloat16

LANES = 128
SUBLANES = 8
SCAN_SHIFTS = (1, 2, 4)
CHUNK = 64
POOL_WINDOWS = (2, 4, 8, 16)
POOL_PAD = 16
CONV_PAD = 8
ROT_DIM = 16
ROPE_THETA = 500000.0
LN_EPS = 1e-5
VMEM_LIMIT = 56 * 1024 * 1024

ROW_TILE = 512
ATTN_TILE = 512
ATTN_Q_TILE = 1024
SSM_TILE = 128


def _cparams(sem):
    return pltpu.CompilerParams(dimension_semantics=sem, vmem_limit_bytes=VMEM_LIMIT)


def _const_spec(shape):
    zeros = (0,) * len(shape)
    return pl.BlockSpec(shape, lambda *_: zeros, pipeline_mode=pl.Buffered(1))


def _layer_norm(y, g, b):
    mu = jnp.mean(y, axis=-1, keepdims=True)
    d = y - mu
    var = jnp.mean(d * d, axis=-1, keepdims=True)
    return d * lax.rsqrt(var + LN_EPS) * g + b


def _proj_kernel(x_ref, w_ref, o_ref, *, sigmoid):
    y = jnp.dot(x_ref[...].astype(BF16), w_ref[...], preferred_element_type=F32)
    if sigmoid:
        y = jax.nn.sigmoid(y)
    o_ref[...] = y.astype(o_ref.dtype)


def _project(x, w, out_dtype, sigmoid):
    m, k = x.shape
    n = w.shape[1]
    return pl.pallas_call(
        functools.partial(_proj_kernel, sigmoid=sigmoid),
        grid=(m // ROW_TILE,),
        in_specs=[pl.BlockSpec((ROW_TILE, k), lambda i: (i, 0)), _const_spec(w.shape)],
        out_specs=pl.BlockSpec((ROW_TILE, n), lambda i: (i, 0)),
        out_shape=jax.ShapeDtypeStruct((m, n), out_dtype),
        compiler_params=_cparams(("parallel",)),
        name="proj_sigmoid" if sigmoid else "proj_plain",
    )(x, w)


def _rope(y, cos, sin_up, sin_dn):
    outs = []
    for c in range(y.shape[1] // LANES):
        yc = y[:, c * LANES:(c + 1) * LANES]
        outs.append(yc * cos + pltpu.roll(yc, LANES - ROT_DIM // 2, 1) * sin_up
                    + pltpu.roll(yc, ROT_DIM // 2, 1) * sin_dn)
    return jnp.concatenate(outs, axis=1)


def _qkv_kernel(x_ref, w_ref, cos_ref, su_ref, sd_ref, *refs, width, q_scale, stacked):
    q_ref, kf_ref, vf_ref, kb_ref, vb_ref = refs[-5:]
    xb = x_ref[...].astype(BF16)
    cos, su, sd = cos_ref[...], su_ref[...], sd_ref[...]
    q = jnp.dot(xb, w_ref[:, 0:width], preferred_element_type=F32)
    q_ref[...] = (_rope(q, cos, su, sd) * q_scale).astype(BF16)
    k = _rope(jnp.dot(xb, w_ref[:, width:2 * width], preferred_element_type=F32), cos, su, sd)
    v = jnp.dot(xb, w_ref[:, 2 * width:3 * width], preferred_element_type=F32)
    if stacked:
        kt = k.T
        kf_ref[0, 0] = kt
        kb_ref[0] = kt.astype(BF16)
        vf_ref[0] = v
    else:
        kf_ref[...] = k
        kb_ref[...] = k.astype(BF16)
        vf_ref[...] = v
    vb_ref[...] = v.astype(BF16)


def _qkv_project(x_all, w_qkv, tables, row_off, rows, q_scale):
    k = x_all.shape[1]
    width = w_qkv.shape[1] // 3
    tm = min(ROW_TILE, rows)
    blk0 = row_off // tm
    row_spec = lambda cols: pl.BlockSpec((tm, cols), lambda i: (i, 0))
    tab_spec = pl.BlockSpec((tm, LANES), lambda i: (i, 0))
    return pl.pallas_call(
        functools.partial(_qkv_kernel, width=width, q_scale=q_scale, stacked=False),
        grid=(rows // tm,),
        in_specs=[pl.BlockSpec((tm, k), lambda i: (blk0 + i, 0)), _const_spec(w_qkv.shape),
                  tab_spec, tab_spec, tab_spec],
        out_specs=[row_spec(width)] * 5,
        out_shape=[jax.ShapeDtypeStruct((rows, width), BF16),
                   jax.ShapeDtypeStruct((rows, width), F32), jax.ShapeDtypeStruct((rows, width), F32),
                   jax.ShapeDtypeStruct((rows, width), BF16), jax.ShapeDtypeStruct((rows, width), BF16)],
        compiler_params=_cparams(("parallel",)),
        name="proj_qkv",
    )(x_all, w_qkv, *tables)


def _qkv_project_stacked(x_all, w_qkv, tables, bt, L, q_scale, layer, depth, prev):
    k = x_all.shape[1]
    width = w_qkv.shape[1] // 3
    rows = bt * L
    tm = min(ROW_TILE, L)
    nl = L // tm
    tab_spec = pl.BlockSpec((tm, LANES), lambda i: (i % nl, 0))
    row_spec = pl.BlockSpec((tm, width), lambda i: (i, 0))
    any_spec = pl.BlockSpec(memory_space=pl.ANY)
    n_prev = 0 if prev is None else 2
    return pl.pallas_call(
        functools.partial(_qkv_kernel, width=width, q_scale=q_scale, stacked=True),
        grid=(rows // tm,),
        in_specs=[pl.BlockSpec((tm, k), lambda i: (i, 0)), _const_spec(w_qkv.shape),
                  tab_spec, tab_spec, tab_spec] + [any_spec] * n_prev,
        out_specs=[row_spec,
                   pl.BlockSpec((1, 1, width, tm), lambda i: (layer, i // nl, 0, i % nl)),
                   pl.BlockSpec((1, tm, width), lambda i: (layer, i, 0)),
                   pl.BlockSpec((1, width, tm), lambda i: (i // nl, 0, i % nl)),
                   row_spec],
        out_shape=[jax.ShapeDtypeStruct((rows, width), BF16),
                   jax.ShapeDtypeStruct((depth, bt, width, L), F32), jax.ShapeDtypeStruct((depth, rows, width), F32),
                   jax.ShapeDtypeStruct((bt, width, L), BF16), jax.ShapeDtypeStruct((rows, width), BF16)],
        input_output_aliases={} if prev is None else {5: 1, 6: 2},
        compiler_params=_cparams(("parallel",)),
        name="proj_qkv_stacked",
    )(x_all, w_qkv, *tables, *(() if prev is None else prev))


def _rope_tables(pos):
    half = ROT_DIM // 2
    inv = ROPE_THETA ** (-jnp.arange(0, ROT_DIM, 2, dtype=F32) / ROT_DIM)
    ang = pos.astype(F32)[:, None] * inv[None, :]
    cos, sin = jnp.cos(ang), jnp.sin(ang)
    n = pos.shape[0]
    ones = jnp.ones((n, 64 - ROT_DIM), F32)
    zeros = jnp.zeros((n, 64 - half), F32)
    cos64 = jnp.concatenate([cos, cos, ones], axis=1)
    up64 = jnp.concatenate([-sin, zeros], axis=1)
    dn64 = jnp.concatenate([jnp.zeros((n, half), F32), sin, jnp.zeros((n, 64 - ROT_DIM), F32)], axis=1)
    return tuple(jnp.concatenate([t, t], axis=1) for t in (cos64, up64, dn64))


def _local_kernel(z_ref, h_ref, b_ref, c_ref, hp_ref, hc_ref, pw_ref, ps_ref, cw_ref, cb_ref, *refs, tl, pos0):
    op_ref, oc_ref, np_ref, nc_ref, zbuf, cbuf = refs[-6:]
    i = pl.program_id(1)
    last = pl.num_programs(1) - 1

    @pl.when(i == 0)
    def _():
        zbuf[0:POOL_PAD, :] = hp_ref[0]
        cbuf[0:CONV_PAD, :] = hc_ref[0]

    @pl.when(i > 0)
    def _():
        zbuf[0:POOL_PAD, :] = zbuf[tl:tl + POOL_PAD, :]
        cbuf[0:CONV_PAD, :] = cbuf[tl:tl + CONV_PAD, :]

    z = z_ref[...]
    zbuf[POOL_PAD:POOL_PAD + tl, :] = z
    cbuf[CONV_PAD:CONV_PAD + tl, :] = c_ref[...] * h_ref[...]

    pos = pos0 + i * tl + lax.broadcasted_iota(jnp.int32, (tl, 1), 0)
    gw = z.shape[1] // len(POOL_WINDOWS)
    outs = []
    for g, w in enumerate(POOL_WINDOWS):
        cols = slice(g * gw, (g + 1) * gw)
        s = z[:, cols]
        for j in range(1, w):
            s = s + zbuf[POOL_PAD - j:POOL_PAD - j + tl, cols]
        cnt = jnp.minimum(w, pos + 1).astype(F32)
        u = s / cnt - z[:, cols]
        outs.append(jnp.dot(u.astype(BF16), pw_ref[g], preferred_element_type=F32))
    op_ref[...] = (jnp.concatenate(outs, axis=1) * ps_ref[...]).astype(op_ref.dtype)

    y = cb_ref[...] + cbuf[CONV_PAD - 2:CONV_PAD - 2 + tl, :] * cw_ref[0:1, :]
    y = y + cbuf[CONV_PAD - 1:CONV_PAD - 1 + tl, :] * cw_ref[1:2, :]
    y = y + cbuf[CONV_PAD:CONV_PAD + tl, :] * cw_ref[2:3, :]
    oc_ref[...] = (b_ref[...] * y).astype(oc_ref.dtype)

    @pl.when(i == last)
    def _():
        np_ref[0] = zbuf[tl + 1:tl + POOL_PAD, :]
        nc_ref[0] = cbuf[tl + CONV_PAD - 2:tl + CONV_PAD, :]


def _alias_args(prev, n_inputs):
    if prev is None:
        return [], {}, ()
    return ([pl.BlockSpec(memory_space=pl.ANY)] * len(prev),
            {n_inputs + k: k for k in range(len(prev))}, tuple(prev))


def _local_mixers(mix, hist_pool, hist_conv, pool_w, pool_scale, conv_w, conv_b, row_off, bt, L, pos0, prev):
    width = pool_scale.shape[-1]
    tl = min(ROW_TILE, L)
    nl = L // tl
    blk0 = row_off // tl
    col_spec = lambda cb: pl.BlockSpec((tl, width), lambda b, i: (blk0 + b * nl + i, cb))
    out_spec = pl.BlockSpec((tl, width), lambda b, i: (blk0 + b * nl + i, 0))
    alias_specs, alias_map, alias_in = _alias_args(prev, 10)
    hp = jnp.pad(hist_pool, ((0, 0), (POOL_PAD - hist_pool.shape[1], 0), (0, 0)))
    hc = jnp.pad(hist_conv, ((0, 0), (CONV_PAD - hist_conv.shape[1], 0), (0, 0)))
    n_hp, n_hc = hist_pool.shape[1], hist_conv.shape[1]
    return pl.pallas_call(
        functools.partial(_local_kernel, tl=tl, pos0=pos0),
        grid=(bt, nl),
        in_specs=[col_spec(0), col_spec(2), col_spec(3), col_spec(4),
                  pl.BlockSpec((1, POOL_PAD, width), lambda b, i: (b, 0, 0)),
                  pl.BlockSpec((1, CONV_PAD, width), lambda b, i: (b, 0, 0)),
                  _const_spec(pool_w.shape), _const_spec((1, width)),
                  _const_spec(conv_w.shape), _const_spec((1, width))] + alias_specs,
        out_specs=[out_spec, out_spec,
                   pl.BlockSpec((1, n_hp, width), lambda b, i: (b, 0, 0)),
                   pl.BlockSpec((1, n_hc, width), lambda b, i: (b, 0, 0))],
        out_shape=[jax.ShapeDtypeStruct((mix.shape[0], width), BF16), jax.ShapeDtypeStruct((mix.shape[0], width), BF16),
                   jax.ShapeDtypeStruct((bt, n_hp, width), F32), jax.ShapeDtypeStruct((bt, n_hc, width), F32)],
        scratch_shapes=[pltpu.VMEM((POOL_PAD + tl, width), F32), pltpu.VMEM((CONV_PAD + tl, width), F32)],
        input_output_aliases=alias_map,
        compiler_params=_cparams(("parallel", "arbitrary")),
        name="local_mixers",
    )(mix, mix, mix, mix, hp, hc, pool_w.astype(BF16), pool_scale.reshape(1, width),
      conv_w, conv_b.reshape(1, width), *alias_in)


def _discretize_kernel(are_ref, aim_ref, ldt_ref, bre_ref, bim_ref, abr_ref, abi_ref, bbr_ref, bbi_ref):
    a_re, a_im = are_ref[...], aim_ref[...]
    dt = jnp.exp(ldt_ref[...])
    mag = jnp.exp(a_re * dt)
    ab_re = mag * jnp.cos(a_im * dt)
    ab_im = mag * jnp.sin(a_im * dt)
    den = a_re * a_re + a_im * a_im
    cr = ((ab_re - 1.0) * a_re + ab_im * a_im) / den
    ci = (ab_im * a_re - (ab_re - 1.0) * a_im) / den
    b_re, b_im = bre_ref[...], bim_ref[...]
    abr_ref[...] = ab_re
    abi_ref[...] = ab_im
    bbr_ref[...] = cr * b_re - ci * b_im
    bbi_ref[...] = cr * b_im + ci * b_re


def _discretize(a_re, a_im, log_dt, b_re, b_im):
    g, p = a_re.shape
    n = b_re.shape[-1]
    col = lambda t: t.reshape(g * p, 1)
    ldt = jnp.broadcast_to(log_dt[:, None], (g, p))
    shapes = [jax.ShapeDtypeStruct((g * p, 1), F32)] * 2 + [jax.ShapeDtypeStruct((g * p, n), F32)] * 2
    return pl.pallas_call(_discretize_kernel, out_shape=shapes, name="ssm_discretize")(
        col(a_re), col(a_im), col(ldt), b_re.reshape(g * p, n), b_im.reshape(g * p, n))


def _scan_tables(are_ref, aim_ref, tab_ref):
    sub = lax.broadcasted_iota(jnp.int32, (SUBLANES, LANES), 0)
    for j in range(tab_ref.shape[0]):
        ar, ai = are_ref[j], aim_ref[j]
        powers = [(ar, ai)]
        for _ in range(SUBLANES - 1):
            pr, pi = powers[-1]
            powers.append((pr * ar - pi * ai, pr * ai + pi * ar))
        for t, d in enumerate(SCAN_SHIFTS):
            dr, di = powers[d - 1]
            tab_ref[j, 2 * t] = jnp.where(sub >= d, dr, 0.0)
            tab_ref[j, 2 * t + 1] = jnp.where(sub >= d, di, 0.0)
        tab_ref[j, 2 * len(SCAN_SHIFTS)] = jnp.concatenate([p[0] for p in powers], axis=0)
        tab_ref[j, 2 * len(SCAN_SHIFTS) + 1] = jnp.concatenate([p[1] for p in powers], axis=0)


def _ssm_kernel(u_ref, hre_ref, him_ref, are_ref, aim_ref, bcat_ref, ccat_ref, d_ref, wg_ref, *refs,
                tl, halves, chunks):
    o_ref, nre_ref, nim_ref, s_ref, cre_ref, cim_ref, tab_ref = refs[-7:]
    i = pl.program_id(1)

    @pl.when(i == 0)
    def _():
        cre_ref[...] = hre_ref[0]
        cim_ref[...] = him_ref[0]

    u = u_ref[...]
    ub = u.astype(BF16)
    kw = ub.shape[1] // halves
    per_half = 2 * chunks
    for h in range(halves):
        bu = jnp.dot(ub[:, h * kw:(h + 1) * kw], bcat_ref[h], preferred_element_type=F32)
        for q in range(per_half):
            s_ref[h * per_half + q] = bu[:, q * LANES:(q + 1) * LANES]

    @pl.when(i == 0)
    def _():
        _scan_tables(are_ref, aim_ref, tab_ref)

    def scan_chunk(j, carry):
        ire = (j // chunks) * per_half + (j % chunks)
        iim = ire + chunks
        cr = jnp.broadcast_to(cre_ref[j], (SUBLANES, LANES))
        ci = jnp.broadcast_to(cim_ref[j], (SUBLANES, LANES))
        steps = [(d, tab_ref[j, 2 * t], tab_ref[j, 2 * t + 1]) for t, d in enumerate(SCAN_SHIFTS)]
        pw_re, pw_im = tab_ref[j, 2 * len(SCAN_SHIFTS)], tab_ref[j, 2 * len(SCAN_SHIFTS) + 1]
        for r in range(tl // SUBLANES):
            rows = pl.ds(SUBLANES * r, SUBLANES)
            xr, xi = s_ref[ire, rows, :], s_ref[iim, rows, :]
            for d, mr, mi in steps:
                sr, si = pltpu.roll(xr, d, 0), pltpu.roll(xi, d, 0)
                xr, xi = xr + mr * sr - mi * si, xi + mr * si + mi * sr
            xr, xi = xr + pw_re * cr - pw_im * ci, xi + pw_re * ci + pw_im * cr
            s_ref[ire, rows, :] = xr
            s_ref[iim, rows, :] = xi
            cr = jnp.broadcast_to(xr[SUBLANES - 1:SUBLANES], (SUBLANES, LANES))
            ci = jnp.broadcast_to(xi[SUBLANES - 1:SUBLANES], (SUBLANES, LANES))
        cre_ref[j] = cr[0:1]
        cim_ref[j] = ci[0:1]
        return carry

    for j in range(halves * chunks):
        scan_chunk(j, 0)

    ys = []
    for h in range(halves):
        st = jnp.concatenate([s_ref[h * per_half + q] for q in range(per_half)], axis=1)
        ys.append(jnp.dot(st.astype(BF16), ccat_ref[h], preferred_element_type=F32))
    y = jnp.concatenate(ys, axis=1) + d_ref[...] * u
    v = 0.5 * y * (1.0 + jnp.tanh(math.sqrt(2.0 / math.pi) * (y + 0.044715 * (y * y * y))))
    gate = jax.nn.sigmoid(jnp.dot(v.astype(BF16), wg_ref[...], preferred_element_type=F32))
    o_ref[...] = (v * gate).astype(o_ref.dtype)

    @pl.when(i == pl.num_programs(1) - 1)
    def _():
        nre_ref[0] = cre_ref[...]
        nim_ref[0] = cim_ref[...]


def _ssm_weights(ab_re, ab_im, bb_re, bb_im, c_re, c_im, halves):
    g, n, p = c_re.shape
    gh = g // halves
    eye = jnp.eye(gh, dtype=F32)
    bcat, ccat = [], []
    for h in range(halves):
        sl = slice(h * gh, (h + 1) * gh)
        dense_b = lambda t: jnp.einsum('gpn,gk->gnkp', t.reshape(g, p, n)[sl], eye).reshape(gh * n, gh * p)
        dense_c = lambda t: jnp.einsum('gnp,gk->gpkn', t[sl], eye).reshape(gh * p, gh * n)
        bcat.append(jnp.concatenate([dense_b(bb_re), dense_b(bb_im)], axis=1))
        ccat.append(jnp.concatenate([dense_c(c_re), -dense_c(c_im)], axis=0))
    nch = g * p // LANES
    return (ab_re.reshape(nch, 1, LANES), ab_im.reshape(nch, 1, LANES),
            jnp.stack(bcat).astype(BF16), jnp.stack(ccat).astype(BF16))


def _ssm_mixer(mix, h_re, h_im, ssm_w, d_skip, w_glu, row_off, bt, L, prev):
    a_re, a_im, bcat, ccat = ssm_w
    alias_specs, alias_map, alias_in = _alias_args(prev, 9)
    halves = bcat.shape[0]
    width = d_skip.shape[-1]
    nch = a_re.shape[0]
    chunks = nch // halves
    g, p = h_re.shape[1], h_re.shape[2]
    tl = min(SSM_TILE, L)
    nl = L // tl
    blk0 = row_off // tl
    state_spec = pl.BlockSpec((1, nch, 1, LANES), lambda b, i: (b, 0, 0, 0))
    o, n_re, n_im = pl.pallas_call(
        functools.partial(_ssm_kernel, tl=tl, halves=halves, chunks=chunks),
        grid=(bt, nl),
        in_specs=[pl.BlockSpec((tl, width), lambda b, i: (blk0 + b * nl + i, 1)),
                  state_spec, state_spec,
                  _const_spec(a_re.shape), _const_spec(a_im.shape),
                  _const_spec(bcat.shape), _const_spec(ccat.shape),
                  _const_spec((1, width)), _const_spec(w_glu.shape)] + alias_specs,
        out_specs=[pl.BlockSpec((tl, width), lambda b, i: (blk0 + b * nl + i, 0)), state_spec, state_spec],
        out_shape=[jax.ShapeDtypeStruct((mix.shape[0], width), BF16),
                   jax.ShapeDtypeStruct((bt, nch, 1, LANES), F32), jax.ShapeDtypeStruct((bt, nch, 1, LANES), F32)],
        scratch_shapes=[pltpu.VMEM((2 * nch, tl, LANES), F32),
                        pltpu.VMEM((nch, 1, LANES), F32), pltpu.VMEM((nch, 1, LANES), F32),
                        pltpu.VMEM((nch, 2 * len(SCAN_SHIFTS) + 2, SUBLANES, LANES), F32)],
        input_output_aliases=alias_map,
        compiler_params=_cparams(("parallel", "arbitrary")),
        name="ssm_mixer",
    )(mix, h_re.astype(F32).reshape(bt, nch, 1, LANES), h_im.astype(F32).reshape(bt, nch, 1, LANES),
      a_re, a_im, bcat, ccat, d_skip.reshape(1, width), w_glu.astype(BF16), *alias_in)
    return o, n_re.reshape(bt, g, p), n_im.reshape(bt, g, p)


def _attn_init(m_ref, l_ref, acc_ref):
    m_ref[...] = jnp.full(m_ref.shape, -jnp.inf, F32)
    l_ref[...] = jnp.zeros(l_ref.shape, F32)
    acc_ref[...] = jnp.zeros(acc_ref.shape, F32)


def _attn_update(q_of, k_of, v_of, visible, heads, m_ref, l_ref, acc_ref):
    low_lanes = lax.broadcasted_iota(jnp.int32, (1, LANES), 1) < (LANES // 2)
    if visible is not None:
        visible = jnp.concatenate([visible, visible], axis=0)
    for h in range(heads):
        qh, kh, vh = q_of(h), k_of(h), v_of(h)
        zero = jnp.zeros_like(qh)
        qm = jnp.concatenate([jnp.where(low_lanes, qh, zero), jnp.where(low_lanes, zero, qh)], axis=0)
        s = jnp.dot(qm, kh, preferred_element_type=F32)
        if visible is not None:
            s = jnp.where(visible, s, -jnp.inf)
        m_old = m_ref[h]
        m_new = jnp.maximum(m_old, jnp.max(s, axis=1, keepdims=True))
        alpha = jnp.exp2(m_old - m_new)
        ps = [jnp.exp2(s[:, j * LANES:(j + 1) * LANES] - m_new) for j in range(s.shape[1] // LANES)]
        lsum = ps[0]
        for pj in ps[1:]:
            lsum = lsum + pj
        l_ref[h] = alpha * l_ref[h] + lsum
        p = jnp.concatenate([pj.astype(BF16) for pj in ps], axis=1)
        acc_ref[h] = alpha * acc_ref[h] + jnp.dot(p, vh, preferred_element_type=F32)
        m_ref[h] = m_new


def _attn_finalize(lam_ref, sw_ref, o_ref, l_ref, acc_ref, heads, lam_init):
    lp = lam_ref[...]
    lam = (jnp.exp(jnp.sum(lp[0:1] * lp[1:2], axis=1, keepdims=True))
           - jnp.exp(jnp.sum(lp[2:3] * lp[3:4], axis=1, keepdims=True)) + lam_init)
    rows = o_ref.shape[0]
    for h in range(heads):
        oh = acc_ref[h] / jnp.sum(l_ref[h], axis=1, keepdims=True)
        o = oh[:rows] - lam * oh[rows:]
        o = o * lax.rsqrt(jnp.mean(o * o, axis=1, keepdims=True) + LN_EPS) * sw_ref[...] * (1.0 - lam_init)
        o_ref[:, h * LANES:(h + 1) * LANES] = o.astype(o_ref.dtype)


def _head_cols(ref):
    lead = (0,) * (len(ref.shape) - 2)
    return lambda h: ref[(*lead, slice(None), slice(h * LANES, (h + 1) * LANES))].astype(BF16)


def _head_rows(ref):
    lead = (0,) * (len(ref.shape) - 2)
    return lambda h: ref[(*lead, slice(h * LANES, (h + 1) * LANES), slice(None))].astype(BF16)


def _attn_kernel(qi_ref, ki_ref, fl_ref, q_ref, k_ref, v_ref, lam_ref, sw_ref, o_ref, m_ref, l_ref, acc_ref,
                 *, tq, tk, heads, q_pos0, lk, lam_init):
    step = pl.program_id(1)
    qi, ki, fl = qi_ref[step], ki_ref[step], fl_ref[step]

    @pl.when(ki == 0)
    def _():
        _attn_init(m_ref, l_ref, acc_ref)

    def accumulate(masked):
        visible = None
        if masked:
            q_pos = q_pos0 + qi * tq + lax.broadcasted_iota(jnp.int32, (tq, 1), 0)
            k_pos = ki * tk + lax.broadcasted_iota(jnp.int32, (1, tk), 1)
            visible = (k_pos < (q_pos // CHUNK + 1) * CHUNK) & (k_pos < lk)
        _attn_update(_head_cols(q_ref), _head_rows(k_ref), _head_cols(v_ref), visible, heads, m_ref, l_ref, acc_ref)

    @pl.when((fl & 1) == 0)
    def _():
        accumulate(False)

    @pl.when((fl & 1) != 0)
    def _():
        accumulate(True)

    @pl.when((fl & 2) != 0)
    def _():
        _attn_finalize(lam_ref, sw_ref, o_ref, l_ref, acc_ref, heads, lam_init)


def _decode_attn_kernel(q_ref, ck_ref, cv_ref, nk_ref, nv_ref, lam_ref, sw_ref, joint_ref, o_ref, m_ref, l_ref,
                        acc_ref, *, tq, heads, past, n_cache, lam_init):
    j = pl.program_id(1)

    @pl.when(j == 0)
    def _():
        _attn_init(m_ref, l_ref, acc_ref)

    @pl.when(j < n_cache)
    def _():
        cached_v = lambda h: cv_ref[0, 0, :, h, :].astype(BF16)
        _attn_update(_head_cols(q_ref), _head_rows(ck_ref), cached_v, None, heads, m_ref, l_ref, acc_ref)

    @pl.when(j == n_cache)
    def _():
        nk = nk_ref.shape[2]
        t = lax.broadcasted_iota(jnp.int32, (tq, 1), 0)
        i = lax.broadcasted_iota(jnp.int32, (1, nk), 1)
        visible = (past + i < ((past + t) // CHUNK + 1) * CHUNK) & (i < tq)
        _attn_update(_head_cols(q_ref), _head_rows(nk_ref), _head_cols(nv_ref), visible, heads, m_ref, l_ref, acc_ref)
        _attn_finalize(lam_ref, sw_ref, o_ref, l_ref, acc_ref, heads, lam_init)


def _attn_schedule(L, lk, tq, tk, q_pos0):
    qi, ki, fl = [], [], []
    for a in range(L // tq):
        first_end = ((q_pos0 + a * tq) // CHUNK + 1) * CHUNK
        last_end = min(((q_pos0 + a * tq + tq - 1) // CHUNK + 1) * CHUNK, lk)
        nk = -(-last_end // tk)
        for b in range(nk):
            full = (b + 1) * tk <= min(first_end, lk)
            qi.append(a); ki.append(b); fl.append((0 if full else 1) | (2 if b == nk - 1 else 0))
    return tuple(jnp.asarray(np.asarray(t, np.int32)) for t in (qi, ki, fl))


def _attention(q, kt_all, v_all, lam_p, subln_w, bt, L, lk, q_pos0, lam_init, total_rows):
    width = q.shape[-1]
    heads = width // LANES
    tq = min(ATTN_Q_TILE, L)
    tk = ATTN_TILE
    lk_pad = kt_all.shape[2]
    nq = L // tq
    qi, ki, fl = _attn_schedule(L, lk, tq, tk, q_pos0)
    grid_spec = pltpu.PrefetchScalarGridSpec(
        num_scalar_prefetch=3,
        grid=(bt, int(qi.shape[0])),
        in_specs=[pl.BlockSpec((1, tq, width), lambda b, s, qi, ki, fl: (b, qi[s], 0)),
                  pl.BlockSpec((1, width, tk), lambda b, s, qi, ki, fl: (b, 0, ki[s])),
                  pl.BlockSpec((1, tk, width), lambda b, s, qi, ki, fl: (b, ki[s], 0)),
                  pl.BlockSpec(lam_p.shape, lambda b, s, qi, ki, fl: (0, 0)),
                  pl.BlockSpec((1, LANES), lambda b, s, qi, ki, fl: (0, 0))],
        out_specs=pl.BlockSpec((tq, width), lambda b, s, qi, ki, fl: (b * nq + qi[s], 0)),
        scratch_shapes=[pltpu.VMEM((heads, 2 * tq, LANES), F32)] * 3)
    assert lk_pad % tk == 0 and L % tq == 0
    return pl.pallas_call(
        functools.partial(_attn_kernel, tq=tq, tk=tk, heads=heads, q_pos0=q_pos0, lk=lk, lam_init=lam_init),
        grid_spec=grid_spec,
        out_shape=jax.ShapeDtypeStruct((total_rows, width), BF16),
        compiler_params=_cparams(("parallel", "arbitrary")),
        name="diff_attention",
    )(qi, ki, fl, q.reshape(bt, L, width), kt_all, v_all, lam_p, subln_w.reshape(1, LANES))


def _decode_attention(q, cache_kt, cache_v, layer, new_k, new_v, lam_p, subln_w, bt, L, past, lam_init, row_off, joint):
    width = q.shape[-1]
    heads = width // LANES
    tk = ATTN_TILE
    assert past % tk == 0 and past >= tk and L <= LANES and past % CHUNK == 0 and row_off % L == 0
    n_cache = past // tk
    new_kt = jnp.pad(jnp.swapaxes(new_k.reshape(bt, L, width), 1, 2), ((0, 0), (0, 0), (0, LANES - L)))
    new_v = jnp.pad(new_v.reshape(bt, L, width), ((0, 0), (0, LANES - L), (0, 0)))
    row_spec = pl.BlockSpec((1, L, width), lambda b, j: (b, 0, 0))
    scratch = pltpu.VMEM((heads, 2 * L, LANES), F32)
    return pl.pallas_call(
        functools.partial(_decode_attn_kernel, tq=L, heads=heads, past=past, n_cache=n_cache, lam_init=lam_init),
        grid=(bt, n_cache + 1),
        in_specs=[row_spec,
                  pl.BlockSpec((1, 1, width, tk), lambda b, j: (layer, b, 0, jnp.minimum(j, n_cache - 1))),
                  pl.BlockSpec((1, 1, tk, heads, LANES), lambda b, j: (layer, b, jnp.minimum(j, n_cache - 1), 0, 0)),
                  pl.BlockSpec((1, width, LANES), lambda b, j: (b, 0, 0)),
                  pl.BlockSpec((1, LANES, width), lambda b, j: (b, 0, 0)),
                  pl.BlockSpec(lam_p.shape, lambda b, j: (0, 0)), pl.BlockSpec((1, LANES), lambda b, j: (0, 0)),
                  pl.BlockSpec(memory_space=pl.ANY)],
        out_specs=pl.BlockSpec((L, width), lambda b, j: (row_off // L + b, 0)),
        out_shape=jax.ShapeDtypeStruct(joint.shape, BF16),
        scratch_shapes=[scratch, scratch, scratch],
        input_output_aliases={7: 0},
        compiler_params=_cparams(("parallel", "arbitrary")),
        name="decode_attention",
    )(q.reshape(bt, L, width), cache_kt, cache_v, new_kt, new_v, lam_p, subln_w.reshape(1, LANES), joint)


def _merge_kernel(x_ref, g_ref, op_ref, os_ref, oc_ref, oa_ref, wb_ref, wo_ref, lg_ref, lb_ref, o_ref,
                  *, d_model, alpha, offs):
    merged = None
    for b, (o_b, (lo, hi)) in enumerate(zip((op_ref, os_ref, oc_ref, oa_ref), offs)):
        t = jnp.dot(o_b[...], wb_ref[lo:hi, :], preferred_element_type=F32)
        t = t * g_ref[:, b * d_model:(b + 1) * d_model].astype(F32)
        merged = t if merged is None else merged + t
    y = alpha * x_ref[...] + jnp.dot(merged.astype(BF16), wo_ref[...], preferred_element_type=F32)
    o_ref[...] = _layer_norm(y, lg_ref[...], lb_ref[...])


def _merge(x, gates, o_pool, o_ssm, o_conv, o_attn, w_branch, w_out, ln_g, ln_b, alpha):
    m, d = x.shape
    widths = [o_pool.shape[1], o_ssm.shape[1], o_conv.shape[1], o_attn.shape[1]]
    ends = np.cumsum(widths)
    offs = tuple((int(e - w), int(e)) for e, w in zip(ends, widths))
    row = lambda cols: pl.BlockSpec((ROW_TILE, cols), lambda i: (i, 0))
    return pl.pallas_call(
        functools.partial(_merge_kernel, d_model=d, alpha=alpha, offs=offs),
        grid=(m // ROW_TILE,),
        in_specs=[row(d), row(gates.shape[1])] + [row(w) for w in widths]
                 + [_const_spec(w_branch.shape), _const_spec(w_out.shape), _const_spec((1, d)), _const_spec((1, d))],
        out_specs=row(d),
        out_shape=jax.ShapeDtypeStruct((m, d), F32),
        compiler_params=_cparams(("parallel",)),
        name="merge_out_ln",
    )(x, gates, o_pool, o_ssm, o_conv, o_attn, w_branch, w_out, ln_g.reshape(1, d), ln_b.reshape(1, d))


def _ffn_kernel(x_ref, wu_ref, wd_ref, lg_ref, lb_ref, o_ref, *, alpha, chunk):
    x = x_ref[...]
    xb = x.astype(BF16)
    acc = alpha * x
    for c in range(wu_ref.shape[1] // chunk):
        hid = jnp.dot(xb, wu_ref[:, c * chunk:(c + 1) * chunk], preferred_element_type=F32)
        hid = jnp.square(jnp.maximum(hid, 0.0)).astype(BF16)
        acc = acc + jnp.dot(hid, wd_ref[c * chunk:(c + 1) * chunk, :], preferred_element_type=F32)
    o_ref[...] = _layer_norm(acc, lg_ref[...], lb_ref[...])


def _ffn(x, w_up, w_down, ln_g, ln_b, alpha):
    m, d = x.shape
    row = pl.BlockSpec((ROW_TILE, d), lambda i: (i, 0))
    return pl.pallas_call(
        functools.partial(_ffn_kernel, alpha=alpha, chunk=1024),
        grid=(m // ROW_TILE,),
        in_specs=[row, _const_spec(w_up.shape), _const_spec(w_down.shape), _const_spec((1, d)), _const_spec((1, d))],
        out_specs=row,
        out_shape=jax.ShapeDtypeStruct((m, d), F32),
        compiler_params=_cparams(("parallel",)),
        name="ffn_ln",
    )(x, w_up, w_down, ln_g.reshape(1, d), ln_b.reshape(1, d))


def kernel(x_prompt, x_sample, cache_k, cache_v, state_ssm_re, state_ssm_im, state_conv, state_pool, w_in, pool_w, pool_scale, ssm_a_re, ssm_a_im, ssm_log_dt, ssm_b_re, ssm_b_im, ssm_c_re, ssm_c_im, ssm_d, ssm_w_glu, conv_w, conv_b, lambda_q1, lambda_k1, lambda_q2, lambda_k2, subln_w, w_branch, w_out, ln1_g, ln1_b, w_up, w_down, ln2_g, ln2_b):
    depth = w_in.shape[0]
    bp, lp, d = x_prompt.shape
    bs, ls, _ = x_sample.shape
    past = cache_k.shape[2]
    heads, qk_dim = cache_k.shape[3], cache_k.shape[5]
    width_qk = heads * 2 * qk_dim
    pool_width, ssm_width, conv_width = pool_scale.shape[1], ssm_d.shape[1], conv_w.shape[2]
    n_mix = pool_width + ssm_width + 3 * conv_width
    alpha = float((2 * depth) ** 0.25)
    paths = ((bp, lp, 0, 0), (bs, ls, past, bp * lp))

    x = jnp.concatenate([x_prompt.reshape(bp * lp, d), x_sample.reshape(bs * ls, d)], axis=0)
    tables = (_rope_tables(jnp.arange(lp, dtype=jnp.int32)),
              _rope_tables(jnp.tile(past + jnp.arange(ls, dtype=jnp.int32), bs)))

    cache_kt = jnp.transpose(cache_k, (0, 1, 3, 4, 5, 2)).reshape(depth, bs, width_qk, past)

    outs = [[[] for _ in range(6)] for _ in paths]
    kv_stack = None
    for l in range(depth):
        wl = w_in[l].astype(BF16)
        mix = _project(x, wl[:, :n_mix], F32, sigmoid=False)
        gates = _project(x, wl[:, n_mix + 3 * width_qk:], BF16, sigmoid=True)
        ab_re, ab_im, bb_re, bb_im = _discretize(ssm_a_re[l], ssm_a_im[l], ssm_log_dt[l], ssm_b_re[l], ssm_b_im[l])
        ssm_w = _ssm_weights(ab_re, ab_im, bb_re, bb_im, ssm_c_re[l], ssm_c_im[l], halves=2)
        lam_p = jnp.stack([lambda_q1[l], lambda_k1[l], lambda_q2[l], lambda_k2[l]]).astype(F32)
        lam_init = 0.8 - 0.6 * math.exp(-0.3 * l)

        w_qkv = wl[:, n_mix:n_mix + 3 * width_qk]
        q_scale = float(qk_dim) ** -0.5 * math.log2(math.e)
        o_pool = o_conv = o_ssm = o_attn = None
        for pi, (bt, L, pos0, row_off) in enumerate(paths):
            if pi == 0:
                q, k_stack, v_stack, kbt, vb = _qkv_project_stacked(x, w_qkv, tables[pi], bt, L, q_scale, l, depth,
                                                                    kv_stack)
                kv_stack = (k_stack, v_stack)
                hist_pool = jnp.zeros((bt, state_pool.shape[2], pool_width), F32)
                hist_conv = jnp.zeros((bt, state_conv.shape[2], conv_width), F32)
                h_re = h_im = jnp.zeros((bt,) + state_ssm_re.shape[2:], F32)
                o_attn = _attention(q, kbt, vb.reshape(bt, L, width_qk), lam_p, subln_w[l], bt, L, L, pos0, lam_init,
                                    x.shape[0])
                kv_new = ()
            else:
                q, kf, vf, kb, vb = _qkv_project(x, w_qkv, tables[pi], row_off, bt * L, q_scale)
                hist_pool, hist_conv, h_re, h_im = state_pool[l], state_conv[l], state_ssm_re[l], state_ssm_im[l]
                o_attn = _decode_attention(q, cache_kt, cache_v, l, kb, vb, lam_p, subln_w[l], bt, L, past, lam_init,
                                           row_off, o_attn)
                kv_new = (kf.reshape(bt, L, heads, 2, qk_dim), vf.reshape(bt, L, heads, 2 * qk_dim))
            o_pool, o_conv, new_pool, new_conv = _local_mixers(
                mix, hist_pool, hist_conv, pool_w[l], pool_scale[l], conv_w[l], conv_b[l], row_off, bt, L, pos0,
                None if pi == 0 else (o_pool, o_conv))
            o_ssm, new_re, new_im = _ssm_mixer(mix, h_re, h_im, ssm_w, ssm_d[l], ssm_w_glu[l], row_off, bt, L,
                                               None if pi == 0 else (o_ssm,))
            for slot, val in zip(outs[pi], (new_re, new_im, new_conv, new_pool) + kv_new):
                slot.append(val)

        x = _merge(x, gates, o_pool, o_ssm, o_conv, o_attn, w_branch[l].astype(BF16), w_out[l].astype(BF16),
                   ln1_g[l], ln1_b[l], alpha)
        x = _ffn(x, w_up[l].astype(BF16), w_down[l].astype(BF16), ln2_g[l], ln2_b[l], alpha)

    y_prompt = x[:bp * lp].reshape(bp, lp, d)
    y_sample = x[bp * lp:].reshape(bs, ls, d)
    k_stack, v_stack = kv_stack
    k_prompt = jnp.transpose(k_stack.reshape(depth, bp, heads, 2, qk_dim, lp), (0, 1, 5, 2, 3, 4))
    v_prompt = v_stack.reshape(depth, bp, lp, heads, 2 * qk_dim)
    (p_re, p_im, p_conv, p_pool), (s_re, s_im, s_conv, s_pool, s_k, s_v) = (
        [jnp.stack(slot) for slot in path_outs if slot] for path_outs in outs)
    return (y_prompt, y_sample, k_prompt, v_prompt, p_re, p_im, p_conv, p_pool,
            s_k, s_v, s_re, s_im, s_conv, s_pool)
```

```python
import functools
import math

import numpy as np
import jax
import jax.numpy as jnp
from jax import lax
from jax.experimental import pallas as pl
from jax.experimental.pallas import tpu as pltpu

F32 = jnp.float32
BF16 = jnp.bfloat16

LANES = 128
SUBLANES = 8
SCAN_SHIFTS = (1, 2, 4)
CHUNK = 64
POOL_WINDOWS = (2, 4, 8, 16)
POOL_PAD = 16
CONV_PAD = 8
ROT_DIM = 16
ROPE_THETA = 500000.0
LN_EPS = 1e-5
VMEM_LIMIT = 56 * 1024 * 1024

ROW_TILE = 512
ATTN_TILE = 512
ATTN_Q_TILE = 1024
SSM_TILE = 256


def _cparams(sem):
    return pltpu.CompilerParams(dimension_semantics=sem, vmem_limit_bytes=VMEM_LIMIT)


def _const_spec(shape):
    zeros = (0,) * len(shape)
    return pl.BlockSpec(shape, lambda *_: zeros, pipeline_mode=pl.Buffered(1))


def _layer_norm(y, g, b):
    mu = jnp.mean(y, axis=-1, keepdims=True)
    d = y - mu
    var = jnp.mean(d * d, axis=-1, keepdims=True)
    return d * lax.rsqrt(var + LN_EPS) * g + b


def _proj_kernel(x_ref, w_ref, o_ref, *, sigmoid):
    y = jnp.dot(x_ref[...].astype(BF16), w_ref[...], preferred_element_type=F32)
    if sigmoid:
        y = jax.nn.sigmoid(y)
    o_ref[...] = y.astype(o_ref.dtype)


def _project(x, w, out_dtype, sigmoid):
    m, k = x.shape
    n = w.shape[1]
    return pl.pallas_call(
        functools.partial(_proj_kernel, sigmoid=sigmoid),
        grid=(m // ROW_TILE,),
        in_specs=[pl.BlockSpec((ROW_TILE, k), lambda i: (i, 0)), _const_spec(w.shape)],
        out_specs=pl.BlockSpec((ROW_TILE, n), lambda i: (i, 0)),
        out_shape=jax.ShapeDtypeStruct((m, n), out_dtype),
        compiler_params=_cparams(("parallel",)),
        name="proj_sigmoid" if sigmoid else "proj_plain",
    )(x, w)


def _rope(y, cos, sin_up, sin_dn):
    outs = []
    for c in range(y.shape[1] // LANES):
        yc = y[:, c * LANES:(c + 1) * LANES]
        outs.append(yc * cos + pltpu.roll(yc, LANES - ROT_DIM // 2, 1) * sin_up
                    + pltpu.roll(yc, ROT_DIM // 2, 1) * sin_dn)
    return jnp.concatenate(outs, axis=1)


def _qkv_kernel(x_ref, w_ref, cos_ref, su_ref, sd_ref, *refs, width, q_scale, stacked):
    q_ref, kf_ref, vf_ref, kb_ref, vb_ref = refs[-5:]
    xb = x_ref[...].astype(BF16)
    cos, su, sd = cos_ref[...], su_ref[...], sd_ref[...]
    q = jnp.dot(xb, w_ref[:, 0:width], preferred_element_type=F32)
    q_ref[...] = (_rope(q, cos, su, sd) * q_scale).astype(BF16)
    k = _rope(jnp.dot(xb, w_ref[:, width:2 * width], preferred_element_type=F32), cos, su, sd)
    v = jnp.dot(xb, w_ref[:, 2 * width:3 * width], preferred_element_type=F32)
    if stacked:
        kt = k.T
        kf_ref[0, 0] = kt
        kb_ref[0] = kt.astype(BF16)
        vf_ref[0] = v
    else:
        kf_ref[...] = k
        kb_ref[...] = k.astype(BF16)
        vf_ref[...] = v
    vb_ref[...] = v.astype(BF16)


def _qkv_project(x_all, w_qkv, tables, row_off, rows, q_scale):
    k = x_all.shape[1]
    width = w_qkv.shape[1] // 3
    tm = min(ROW_TILE, rows)
    blk0 = row_off // tm
    row_spec = lambda cols: pl.BlockSpec((tm, cols), lambda i: (i, 0))
    tab_spec = pl.BlockSpec((tm, LANES), lambda i: (i, 0))
    return pl.pallas_call(
        functools.partial(_qkv_kernel, width=width, q_scale=q_scale, stacked=False),
        grid=(rows // tm,),
        in_specs=[pl.BlockSpec((tm, k), lambda i: (blk0 + i, 0)), _const_spec(w_qkv.shape),
                  tab_spec, tab_spec, tab_spec],
        out_specs=[row_spec(width)] * 5,
        out_shape=[jax.ShapeDtypeStruct((rows, width), BF16),
                   jax.ShapeDtypeStruct((rows, width), F32), jax.ShapeDtypeStruct((rows, width), F32),
                   jax.ShapeDtypeStruct((rows, width), BF16), jax.ShapeDtypeStruct((rows, width), BF16)],
        compiler_params=_cparams(("parallel",)),
        name="proj_qkv",
    )(x_all, w_qkv, *tables)


def _qkv_project_stacked(x_all, w_qkv, tables, bt, L, q_scale, layer, depth, prev):
    k = x_all.shape[1]
    width = w_qkv.shape[1] // 3
    rows = bt * L
    tm = min(ROW_TILE, L)
    nl = L // tm
    tab_spec = pl.BlockSpec((tm, LANES), lambda i: (i % nl, 0))
    row_spec = pl.BlockSpec((tm, width), lambda i: (i, 0))
    any_spec = pl.BlockSpec(memory_space=pl.ANY)
    n_prev = 0 if prev is None else 2
    return pl.pallas_call(
        functools.partial(_qkv_kernel, width=width, q_scale=q_scale, stacked=True),
        grid=(rows // tm,),
        in_specs=[pl.BlockSpec((tm, k), lambda i: (i, 0)), _const_spec(w_qkv.shape),
                  tab_spec, tab_spec, tab_spec] + [any_spec] * n_prev,
        out_specs=[row_spec,
                   pl.BlockSpec((1, 1, width, tm), lambda i: (layer, i // nl, 0, i % nl)),
                   pl.BlockSpec((1, tm, width), lambda i: (layer, i, 0)),
                   pl.BlockSpec((1, width, tm), lambda i: (i // nl, 0, i % nl)),
                   row_spec],
        out_shape=[jax.ShapeDtypeStruct((rows, width), BF16),
                   jax.ShapeDtypeStruct((depth, bt, width, L), F32), jax.ShapeDtypeStruct((depth, rows, width), F32),
                   jax.ShapeDtypeStruct((bt, width, L), BF16), jax.ShapeDtypeStruct((rows, width), BF16)],
        input_output_aliases={} if prev is None else {5: 1, 6: 2},
        compiler_params=_cparams(("parallel",)),
        name="proj_qkv_stacked",
    )(x_all, w_qkv, *tables, *(() if prev is None else prev))


def _rope_tables(pos):
    half = ROT_DIM // 2
    inv = ROPE_THETA ** (-jnp.arange(0, ROT_DIM, 2, dtype=F32) / ROT_DIM)
    ang = pos.astype(F32)[:, None] * inv[None, :]
    cos, sin = jnp.cos(ang), jnp.sin(ang)
    n = pos.shape[0]
    ones = jnp.ones((n, 64 - ROT_DIM), F32)
    zeros = jnp.zeros((n, 64 - half), F32)
    cos64 = jnp.concatenate([cos, cos, ones], axis=1)
    up64 = jnp.concatenate([-sin, zeros], axis=1)
    dn64 = jnp.concatenate([jnp.zeros((n, half), F32), sin, jnp.zeros((n, 64 - ROT_DIM), F32)], axis=1)
    return tuple(jnp.concatenate([t, t], axis=1) for t in (cos64, up64, dn64))


def _local_kernel(z_ref, h_ref, b_ref, c_ref, hp_ref, hc_ref, pw_ref, ps_ref, cw_ref, cb_ref, *refs, tl, pos0):
    op_ref, oc_ref, np_ref, nc_ref, zbuf, cbuf = refs[-6:]
    i = pl.program_id(1)
    last = pl.num_programs(1) - 1

    @pl.when(i == 0)
    def _():
        zbuf[0:POOL_PAD, :] = hp_ref[0]
        cbuf[0:CONV_PAD, :] = hc_ref[0]

    @pl.when(i > 0)
    def _():
        zbuf[0:POOL_PAD, :] = zbuf[tl:tl + POOL_PAD, :]
        cbuf[0:CONV_PAD, :] = cbuf[tl:tl + CONV_PAD, :]

    z = z_ref[...]
    zbuf[POOL_PAD:POOL_PAD + tl, :] = z
    cbuf[CONV_PAD:CONV_PAD + tl, :] = c_ref[...] * h_ref[...]

    pos = pos0 + i * tl + lax.broadcasted_iota(jnp.int32, (tl, 1), 0)
    gw = z.shape[1] // len(POOL_WINDOWS)
    outs = []
    for g, w in enumerate(POOL_WINDOWS):
        cols = slice(g * gw, (g + 1) * gw)
        s = z[:, cols]
        for j in range(1, w):
            s = s + zbuf[POOL_PAD - j:POOL_PAD - j + tl, cols]
        cnt = jnp.minimum(w, pos + 1).astype(F32)
        u = s / cnt - z[:, cols]
        outs.append(jnp.dot(u.astype(BF16), pw_ref[g], preferred_element_type=F32))
    op_ref[...] = (jnp.concatenate(outs, axis=1) * ps_ref[...]).astype(op_ref.dtype)

    y = cb_ref[...] + cbuf[CONV_PAD - 2:CONV_PAD - 2 + tl, :] * cw_ref[0:1, :]
    y = y + cbuf[CONV_PAD - 1:CONV_PAD - 1 + tl, :] * cw_ref[1:2, :]
    y = y + cbuf[CONV_PAD:CONV_PAD + tl, :] * cw_ref[2:3, :]
    oc_ref[...] = (b_ref[...] * y).astype(oc_ref.dtype)

    @pl.when(i == last)
    def _():
        np_ref[0] = zbuf[tl + 1:tl + POOL_PAD, :]
        nc_ref[0] = cbuf[tl + CONV_PAD - 2:tl + CONV_PAD, :]


def _alias_args(prev, n_inputs):
    if prev is None:
        return [], {}, ()
    return ([pl.BlockSpec(memory_space=pl.ANY)] * len(prev),
            {n_inputs + k: k for k in range(len(prev))}, tuple(prev))


def _local_mixers(mix, hist_pool, hist_conv, pool_w, pool_scale, conv_w, conv_b, row_off, bt, L, pos0, prev):
    width = pool_scale.shape[-1]
    tl = min(ROW_TILE, L)
    nl = L // tl
    blk0 = row_off // tl
    col_spec = lambda cb: pl.BlockSpec((tl, width), lambda b, i: (blk0 + b * nl + i, cb))
    out_spec = pl.BlockSpec((tl, width), lambda b, i: (blk0 + b * nl + i, 0))
    alias_specs, alias_map, alias_in = _alias_args(prev, 10)
    hp = jnp.pad(hist_pool, ((0, 0), (POOL_PAD - hist_pool.shape[1], 0), (0, 0)))
    hc = jnp.pad(hist_conv, ((0, 0), (CONV_PAD - hist_conv.shape[1], 0), (0, 0)))
    n_hp, n_hc = hist_pool.shape[1], hist_conv.shape[1]
    return pl.pallas_call(
        functools.partial(_local_kernel, tl=tl, pos0=pos0),
        grid=(bt, nl),
        in_specs=[col_spec(0), col_spec(2), col_spec(3), col_spec(4),
                  pl.BlockSpec((1, POOL_PAD, width), lambda b, i: (b, 0, 0)),
                  pl.BlockSpec((1, CONV_PAD, width), lambda b, i: (b, 0, 0)),
                  _const_spec(pool_w.shape), _const_spec((1, width)),
                  _const_spec(conv_w.shape), _const_spec((1, width))] + alias_specs,
        out_specs=[out_spec, out_spec,
                   pl.BlockSpec((1, n_hp, width), lambda b, i: (b, 0, 0)),
                   pl.BlockSpec((1, n_hc, width), lambda b, i: (b, 0, 0))],
        out_shape=[jax.ShapeDtypeStruct((mix.shape[0], width), BF16), jax.ShapeDtypeStruct((mix.shape[0], width), BF16),
                   jax.ShapeDtypeStruct((bt, n_hp, width), F32), jax.ShapeDtypeStruct((bt, n_hc, width), F32)],
        scratch_shapes=[pltpu.VMEM((POOL_PAD + tl, width), F32), pltpu.VMEM((CONV_PAD + tl, width), F32)],
        input_output_aliases=alias_map,
        compiler_params=_cparams(("parallel", "arbitrary")),
        name="local_mixers",
    )(mix, mix, mix, mix, hp, hc, pool_w.astype(BF16), pool_scale.reshape(1, width),
      conv_w, conv_b.reshape(1, width), *alias_in)


def _discretize_kernel(are_ref, aim_ref, ldt_ref, bre_ref, bim_ref, abr_ref, abi_ref, bbr_ref, bbi_ref):
    a_re, a_im = are_ref[...], aim_ref[...]
    dt = jnp.exp(ldt_ref[...])
    mag = jnp.exp(a_re * dt)
    ab_re = mag * jnp.cos(a_im * dt)
    ab_im = mag * jnp.sin(a_im * dt)
    den = a_re * a_re + a_im * a_im
    cr = ((ab_re - 1.0) * a_re + ab_im * a_im) / den
    ci = (ab_im * a_re - (ab_re - 1.0) * a_im) / den
    b_re, b_im = bre_ref[...], bim_ref[...]
    abr_ref[...] = ab_re
    abi_ref[...] = ab_im
    bbr_ref[...] = cr * b_re - ci * b_im
    bbi_ref[...] = cr * b_im + ci * b_re


def _discretize(a_re, a_im, log_dt, b_re, b_im):
    g, p = a_re.shape
    n = b_re.shape[-1]
    col = lambda t: t.reshape(g * p, 1)
    ldt = jnp.broadcast_to(log_dt[:, None], (g, p))
    shapes = [jax.ShapeDtypeStruct((g * p, 1), F32)] * 2 + [jax.ShapeDtypeStruct((g * p, n), F32)] * 2
    return pl.pallas_call(_discretize_kernel, out_shape=shapes, name="ssm_discretize")(
        col(a_re), col(a_im), col(ldt), b_re.reshape(g * p, n), b_im.reshape(g * p, n))


def _scan_tables(are_ref, aim_ref, tab_ref):
    sub = lax.broadcasted_iota(jnp.int32, (SUBLANES, LANES), 0)
    for j in range(tab_ref.shape[0]):
        ar, ai = are_ref[j], aim_ref[j]
        powers = [(ar, ai)]
        for _ in range(SUBLANES - 1):
            pr, pi = powers[-1]
            powers.append((pr * ar - pi * ai, pr * ai + pi * ar))
        for t, d in enumerate(SCAN_SHIFTS):
            dr, di = powers[d - 1]
            tab_ref[j, 2 * t] = jnp.where(sub >= d, dr, 0.0)
            tab_ref[j, 2 * t + 1] = jnp.where(sub >= d, di, 0.0)
        tab_ref[j, 2 * len(SCAN_SHIFTS)] = jnp.concatenate([p[0] for p in powers], axis=0)
        tab_ref[j, 2 * len(SCAN_SHIFTS) + 1] = jnp.concatenate([p[1] for p in powers], axis=0)


def _ssm_kernel(u_ref, hre_ref, him_ref, are_ref, aim_ref, bcat_ref, ccat_ref, d_ref, wg_ref, *refs,
                tl, halves, chunks):
    o_ref, nre_ref, nim_ref, s_ref, cre_ref, cim_ref, tab_ref = refs[-7:]
    i = pl.program_id(1)

    @pl.when(i == 0)
    def _():
        cre_ref[...] = hre_ref[0]
        cim_ref[...] = him_ref[0]

    u = u_ref[...]
    ub = u.astype(BF16)
    kw = ub.shape[1] // halves
    per_half = 2 * chunks
    for h in range(halves):
        bu = jnp.dot(ub[:, h * kw:(h + 1) * kw], bcat_ref[h], preferred_element_type=F32)
        for q in range(per_half):
            s_ref[h * per_half + q] = bu[:, q * LANES:(q + 1) * LANES]

    @pl.when(i == 0)
    def _():
        _scan_tables(are_ref, aim_ref, tab_ref)

    def scan_chunk(j, carry):
        ire = (j // chunks) * per_half + (j % chunks)
        iim = ire + chunks
        cr = jnp.broadcast_to(cre_ref[j], (SUBLANES, LANES))
        ci = jnp.broadcast_to(cim_ref[j], (SUBLANES, LANES))
        steps = [(d, tab_ref[j, 2 * t], tab_ref[j, 2 * t + 1]) for t, d in enumerate(SCAN_SHIFTS)]
        pw_re, pw_im = tab_ref[j, 2 * len(SCAN_SHIFTS)], tab_ref[j, 2 * len(SCAN_SHIFTS) + 1]
        for r in range(tl // SUBLANES):
            rows = pl.ds(SUBLANES * r, SUBLANES)
            xr, xi = s_ref[ire, rows, :], s_ref[iim, rows, :]
            for d, mr, mi in steps:
                sr, si = pltpu.roll(xr, d, 0), pltpu.roll(xi, d, 0)
                xr, xi = xr + mr * sr - mi * si, xi + mr * si + mi * sr
            xr, xi = xr + pw_re * cr - pw_im * ci, xi + pw_re * ci + pw_im * cr
            s_ref[ire, rows, :] = xr
            s_ref[iim, rows, :] = xi
            cr = jnp.broadcast_to(xr[SUBLANES - 1:SUBLANES], (SUBLANES, LANES))
            ci = jnp.broadcast_to(xi[SUBLANES - 1:SUBLANES], (SUBLANES, LANES))
        cre_ref[j] = cr[0:1]
        cim_ref[j] = ci[0:1]
        return carry

    for j in range(halves * chunks):
        scan_chunk(j, 0)

    ys = []
    for h in range(halves):
        st = jnp.concatenate([s_ref[h * per_half + q] for q in range(per_half)], axis=1)
        ys.append(jnp.dot(st.astype(BF16), ccat_ref[h], preferred_element_type=F32))
    y = jnp.concatenate(ys, axis=1) + d_ref[...] * u
    v = 0.5 * y * (1.0 + jnp.tanh(math.sqrt(2.0 / math.pi) * (y + 0.044715 * (y * y * y))))
    gate = jax.nn.sigmoid(jnp.dot(v.astype(BF16), wg_ref[...], preferred_element_type=F32))
    o_ref[...] = (v * gate).astype(o_ref.dtype)

    @pl.when(i == pl.num_programs(1) - 1)
    def _():
        nre_ref[0] = cre_ref[...]
        nim_ref[0] = cim_ref[...]


def _ssm_weights(ab_re, ab_im, bb_re, bb_im, c_re, c_im, halves):
    g, n, p = c_re.shape
    gh = g // halves
    eye = jnp.eye(gh, dtype=F32)
    bcat, ccat = [], []
    for h in range(halves):
        sl = slice(h * gh, (h + 1) * gh)
        dense_b = lambda t: jnp.einsum('gpn,gk->gnkp', t.reshape(g, p, n)[sl], eye).reshape(gh * n, gh * p)
        dense_c = lambda t: jnp.einsum('gnp,gk->gpkn', t[sl], eye).reshape(gh * p, gh * n)
        bcat.append(jnp.concatenate([dense_b(bb_re), dense_b(bb_im)], axis=1))
        ccat.append(jnp.concatenate([dense_c(c_re), -dense_c(c_im)], axis=0))
    nch = g * p // LANES
    return (ab_re.reshape(nch, 1, LANES), ab_im.reshape(nch, 1, LANES),
            jnp.stack(bcat).astype(BF16), jnp.stack(ccat).astype(BF16))


def _ssm_mixer(mix, h_re, h_im, ssm_w, d_skip, w_glu, row_off, bt, L, prev):
    a_re, a_im, bcat, ccat = ssm_w
    alias_specs, alias_map, alias_in = _alias_args(prev, 9)
    halves = bcat.shape[0]
    width = d_skip.shape[-1]
    nch = a_re.shape[0]
    chunks = nch // halves
    g, p = h_re.shape[1], h_re.shape[2]
    tl = min(SSM_TILE, L)
    nl = L // tl
    blk0 = row_off // tl
    state_spec = pl.BlockSpec((1, nch, 1, LANES), lambda b, i: (b, 0, 0, 0))
    o, n_re, n_im = pl.pallas_call(
        functools.partial(_ssm_kernel, tl=tl, halves=halves, chunks=chunks),
        grid=(bt, nl),
        in_specs=[pl.BlockSpec((tl, width), lambda b, i: (blk0 + b * nl + i, 1)),
                  state_spec, state_spec,
                  _const_spec(a_re.shape), _const_spec(a_im.shape),
                  _const_spec(bcat.shape), _const_spec(ccat.shape),
                  _const_spec((1, width)), _const_spec(w_glu.shape)] + alias_specs,
        out_specs=[pl.BlockSpec((tl, width), lambda b, i: (blk0 + b * nl + i, 0)), state_spec, state_spec],
        out_shape=[jax.ShapeDtypeStruct((mix.shape[0], width), BF16),
                   jax.ShapeDtypeStruct((bt, nch, 1, LANES), F32), jax.ShapeDtypeStruct((bt, nch, 1, LANES), F32)],
        scratch_shapes=[pltpu.VMEM((2 * nch, tl, LANES), F32),
                        pltpu.VMEM((nch, 1, LANES), F32), pltpu.VMEM((nch, 1, LANES), F32),
                        pltpu.VMEM((nch, 2 * len(SCAN_SHIFTS) + 2, SUBLANES, LANES), F32)],
        input_output_aliases=alias_map,
        compiler_params=_cparams(("parallel", "arbitrary")),
        name="ssm_mixer",
    )(mix, h_re.astype(F32).reshape(bt, nch, 1, LANES), h_im.astype(F32).reshape(bt, nch, 1, LANES),
      a_re, a_im, bcat, ccat, d_skip.reshape(1, width), w_glu.astype(BF16), *alias_in)
    return o, n_re.reshape(bt, g, p), n_im.reshape(bt, g, p)


def _attn_init(m_ref, l_ref, acc_ref):
    m_ref[...] = jnp.full(m_ref.shape, -jnp.inf, F32)
    l_ref[...] = jnp.zeros(l_ref.shape, F32)
    acc_ref[...] = jnp.zeros(acc_ref.shape, F32)


def _attn_update(q_of, k_of, v_of, visible, heads, m_ref, l_ref, acc_ref):
    low_lanes = lax.broadcasted_iota(jnp.int32, (1, LANES), 1) < (LANES // 2)
    if visible is not None:
        visible = jnp.concatenate([visible, visible], axis=0)
    for h in range(heads):
        qh, kh, vh = q_of(h), k_of(h), v_of(h)
        zero = jnp.zeros_like(qh)
        qm = jnp.concatenate([jnp.where(low_lanes, qh, zero), jnp.where(low_lanes, zero, qh)], axis=0)
        s = jnp.dot(qm, kh, preferred_element_type=F32)
        if visible is not None:
            s = jnp.where(visible, s, -jnp.inf)
        m_old = m_ref[h]
        m_new = jnp.maximum(m_old, jnp.max(s, axis=1, keepdims=True))
        alpha = jnp.exp2(m_old - m_new)
        ps = [jnp.exp2(s[:, j * LANES:(j + 1) * LANES] - m_new) for j in range(s.shape[1] // LANES)]
        lsum = ps[0]
        for pj in ps[1:]:
            lsum = lsum + pj
        l_ref[h] = alpha * l_ref[h] + lsum
        p = jnp.concatenate([pj.astype(BF16) for pj in ps], axis=1)
        acc_ref[h] = alpha * acc_ref[h] + jnp.dot(p, vh, preferred_element_type=F32)
        m_ref[h] = m_new


def _attn_finalize(lam_ref, sw_ref, o_ref, l_ref, acc_ref, heads, lam_init):
    lp = lam_ref[...]
    lam = (jnp.exp(jnp.sum(lp[0:1] * lp[1:2], axis=1, keepdims=True))
           - jnp.exp(jnp.sum(lp[2:3] * lp[3:4], axis=1, keepdims=True)) + lam_init)
    rows = o_ref.shape[0]
    for h in range(heads):
        oh = acc_ref[h] / jnp.sum(l_ref[h], axis=1, keepdims=True)
        o = oh[:rows] - lam * oh[rows:]
        o = o * lax.rsqrt(jnp.mean(o * o, axis=1, keepdims=True) + LN_EPS) * sw_ref[...] * (1.0 - lam_init)
        o_ref[:, h * LANES:(h + 1) * LANES] = o.astype(o_ref.dtype)


def _head_cols(ref):
    lead = (0,) * (len(ref.shape) - 2)
    return lambda h: ref[(*lead, slice(None), slice(h * LANES, (h + 1) * LANES))].astype(BF16)


def _head_rows(ref):
    lead = (0,) * (len(ref.shape) - 2)
    return lambda h: ref[(*lead, slice(h * LANES, (h + 1) * LANES), slice(None))].astype(BF16)


def _attn_kernel(qi_ref, ki_ref, fl_ref, q_ref, k_ref, v_ref, lam_ref, sw_ref, o_ref, m_ref, l_ref, acc_ref,
                 *, tq, tk, heads, q_pos0, lk, lam_init):
    step = pl.program_id(1)
    qi, ki, fl = qi_ref[step], ki_ref[step], fl_ref[step]

    @pl.when(ki == 0)
    def _():
        _attn_init(m_ref, l_ref, acc_ref)

    def accumulate(masked):
        visible = None
        if masked:
            q_pos = q_pos0 + qi * tq + lax.broadcasted_iota(jnp.int32, (tq, 1), 0)
            k_pos = ki * tk + lax.broadcasted_iota(jnp.int32, (1, tk), 1)
            visible = (k_pos < (q_pos // CHUNK + 1) * CHUNK) & (k_pos < lk)
        _attn_update(_head_cols(q_ref), _head_rows(k_ref), _head_cols(v_ref), visible, heads, m_ref, l_ref, acc_ref)

    @pl.when((fl & 1) == 0)
    def _():
        accumulate(False)

    @pl.when((fl & 1) != 0)
    def _():
        accumulate(True)

    @pl.when((fl & 2) != 0)
    def _():
        _attn_finalize(lam_ref, sw_ref, o_ref, l_ref, acc_ref, heads, lam_init)


def _decode_attn_kernel(q_ref, ck_ref, cv_ref, nk_ref, nv_ref, lam_ref, sw_ref, joint_ref, o_ref, m_ref, l_ref,
                        acc_ref, *, tq, heads, past, n_cache, lam_init):
    j = pl.program_id(1)

    @pl.when(j == 0)
    def _():
        _attn_init(m_ref, l_ref, acc_ref)

    @pl.when(j < n_cache)
    def _():
        n_keys = cv_ref.shape[2] // heads
        cached_v = lambda h: cv_ref[0, 0, pl.ds(h, n_keys, stride=heads), :].astype(BF16)
        _attn_update(_head_cols(q_ref), _head_rows(ck_ref), cached_v, None, heads, m_ref, l_ref, acc_ref)

    @pl.when(j == n_cache)
    def _():
        nk = nk_ref.shape[2]
        t = lax.broadcasted_iota(jnp.int32, (tq, 1), 0)
        i = lax.broadcasted_iota(jnp.int32, (1, nk), 1)
        visible = (past + i < ((past + t) // CHUNK + 1) * CHUNK) & (i < tq)
        _attn_update(_head_cols(q_ref), _head_rows(nk_ref), _head_cols(nv_ref), visible, heads, m_ref, l_ref, acc_ref)
        _attn_finalize(lam_ref, sw_ref, o_ref, l_ref, acc_ref, heads, lam_init)


def _attn_schedule(L, lk, tq, tk, q_pos0):
    qi, ki, fl = [], [], []
    for a in range(L // tq):
        first_end = ((q_pos0 + a * tq) // CHUNK + 1) * CHUNK
        last_end = min(((q_pos0 + a * tq + tq - 1) // CHUNK + 1) * CHUNK, lk)
        nk = -(-last_end // tk)
        for b in range(nk):
            full = (b + 1) * tk <= min(first_end, lk)
            qi.append(a); ki.append(b); fl.append((0 if full else 1) | (2 if b == nk - 1 else 0))
    return tuple(jnp.asarray(np.asarray(t, np.int32)) for t in (qi, ki, fl))


def _attention(q, kt_all, v_all, lam_p, subln_w, bt, L, lk, q_pos0, lam_init, total_rows):
    width = q.shape[-1]
    heads = width // LANES
    tq = min(ATTN_Q_TILE, L)
    tk = ATTN_TILE
    lk_pad = kt_all.shape[2]
    nq = L // tq
    qi, ki, fl = _attn_schedule(L, lk, tq, tk, q_pos0)
    grid_spec = pltpu.PrefetchScalarGridSpec(
        num_scalar_prefetch=3,
        grid=(bt, int(qi.shape[0])),
        in_specs=[pl.BlockSpec((1, tq, width), lambda b, s, qi, ki, fl: (b, qi[s], 0)),
                  pl.BlockSpec((1, width, tk), lambda b, s, qi, ki, fl: (b, 0, ki[s])),
                  pl.BlockSpec((1, tk, width), lambda b, s, qi, ki, fl: (b, ki[s], 0)),
                  pl.BlockSpec(lam_p.shape, lambda b, s, qi, ki, fl: (0, 0)),
                  pl.BlockSpec((1, LANES), lambda b, s, qi, ki, fl: (0, 0))],
        out_specs=pl.BlockSpec((tq, width), lambda b, s, qi, ki, fl: (b * nq + qi[s], 0)),
        scratch_shapes=[pltpu.VMEM((heads, 2 * tq, LANES), F32)] * 3)
    assert lk_pad % tk == 0 and L % tq == 0
    return pl.pallas_call(
        functools.partial(_attn_kernel, tq=tq, tk=tk, heads=heads, q_pos0=q_pos0, lk=lk, lam_init=lam_init),
        grid_spec=grid_spec,
        out_shape=jax.ShapeDtypeStruct((total_rows, width), BF16),
        compiler_params=_cparams(("parallel", "arbitrary")),
        name="diff_attention",
    )(qi, ki, fl, q.reshape(bt, L, width), kt_all, v_all, lam_p, subln_w.reshape(1, LANES))


def _decode_attention(q, cache_kt, cache_v, layer, new_k, new_v, lam_p, subln_w, bt, L, past, lam_init, row_off, joint):
    width = q.shape[-1]
    heads = width // LANES
    tk = ATTN_TILE
    assert past % tk == 0 and past >= tk and L <= LANES and past % CHUNK == 0 and row_off % L == 0
    n_cache = past // tk
    new_kt = jnp.pad(jnp.swapaxes(new_k.reshape(bt, L, width), 1, 2), ((0, 0), (0, 0), (0, LANES - L)))
    new_v = jnp.pad(new_v.reshape(bt, L, width), ((0, 0), (0, LANES - L), (0, 0)))
    row_spec = pl.BlockSpec((1, L, width), lambda b, j: (b, 0, 0))
    scratch = pltpu.VMEM((heads, 2 * L, LANES), F32)
    return pl.pallas_call(
        functools.partial(_decode_attn_kernel, tq=L, heads=heads, past=past, n_cache=n_cache, lam_init=lam_init),
        grid=(bt, n_cache + 1),
        in_specs=[row_spec,
                  pl.BlockSpec((1, 1, width, tk), lambda b, j: (layer, b, 0, jnp.minimum(j, n_cache - 1))),
                  pl.BlockSpec((1, 1, tk * heads, LANES), lambda b, j: (layer, b, jnp.minimum(j, n_cache - 1), 0)),
                  pl.BlockSpec((1, width, LANES), lambda b, j: (b, 0, 0)),
                  pl.BlockSpec((1, LANES, width), lambda b, j: (b, 0, 0)),
                  pl.BlockSpec(lam_p.shape, lambda b, j: (0, 0)), pl.BlockSpec((1, LANES), lambda b, j: (0, 0)),
                  pl.BlockSpec(memory_space=pl.ANY)],
        out_specs=pl.BlockSpec((L, width), lambda b, j: (row_off // L + b, 0)),
        out_shape=jax.ShapeDtypeStruct(joint.shape, BF16),
        scratch_shapes=[scratch, scratch, scratch],
        input_output_aliases={7: 0},
        compiler_params=_cparams(("parallel", "arbitrary")),
        name="decode_attention",
    )(q.reshape(bt, L, width), cache_kt, cache_v, new_kt, new_v, lam_p, subln_w.reshape(1, LANES), joint)


def _merge_kernel(x_ref, g_ref, op_ref, os_ref, oc_ref, oa_ref, wb_ref, wo_ref, lg_ref, lb_ref, o_ref,
                  *, d_model, alpha, offs):
    merged = None
    for b, (o_b, (lo, hi)) in enumerate(zip((op_ref, os_ref, oc_ref, oa_ref), offs)):
        t = jnp.dot(o_b[...], wb_ref[lo:hi, :], preferred_element_type=F32)
        t = t * g_ref[:, b * d_model:(b + 1) * d_model].astype(F32)
        merged = t if merged is None else merged + t
    y = alpha * x_ref[...] + jnp.dot(merged.astype(BF16), wo_ref[...], preferred_element_type=F32)
    o_ref[...] = _layer_norm(y, lg_ref[...], lb_ref[...])


def _merge(x, gates, o_pool, o_ssm, o_conv, o_attn, w_branch, w_out, ln_g, ln_b, alpha):
    m, d = x.shape
    widths = [o_pool.shape[1], o_ssm.shape[1], o_conv.shape[1], o_attn.shape[1]]
    ends = np.cumsum(widths)
    offs = tuple((int(e - w), int(e)) for e, w in zip(ends, widths))
    row = lambda cols: pl.BlockSpec((ROW_TILE, cols), lambda i: (i, 0))
    return pl.pallas_call(
        functools.partial(_merge_kernel, d_model=d, alpha=alpha, offs=offs),
        grid=(m // ROW_TILE,),
        in_specs=[row(d), row(gates.shape[1])] + [row(w) for w in widths]
                 + [_const_spec(w_branch.shape), _const_spec(w_out.shape), _const_spec((1, d)), _const_spec((1, d))],
        out_specs=row(d),
        out_shape=jax.ShapeDtypeStruct((m, d), F32),
        compiler_params=_cparams(("parallel",)),
        name="merge_out_ln",
    )(x, gates, o_pool, o_ssm, o_conv, o_attn, w_branch, w_out, ln_g.reshape(1, d), ln_b.reshape(1, d))


def _ffn_kernel(x_ref, wu_ref, wd_ref, lg_ref, lb_ref, o_ref, *, alpha, chunk):
    x = x_ref[...]
    xb = x.astype(BF16)
    acc = alpha * x
    for c in range(wu_ref.shape[1] // chunk):
        hid = jnp.dot(xb, wu_ref[:, c * chunk:(c + 1) * chunk], preferred_element_type=F32)
        hid = jnp.square(jnp.maximum(hid, 0.0)).astype(BF16)
        acc = acc + jnp.dot(hid, wd_ref[c * chunk:(c + 1) * chunk, :], preferred_element_type=F32)
    o_ref[...] = _layer_norm(acc, lg_ref[...], lb_ref[...])


def _ffn(x, w_up, w_down, ln_g, ln_b, alpha):
    m, d = x.shape
    row = pl.BlockSpec((ROW_TILE, d), lambda i: (i, 0))
    return pl.pallas_call(
        functools.partial(_ffn_kernel, alpha=alpha, chunk=1024),
        grid=(m // ROW_TILE,),
        in_specs=[row, _const_spec(w_up.shape), _const_spec(w_down.shape), _const_spec((1, d)), _const_spec((1, d))],
        out_specs=row,
        out_shape=jax.ShapeDtypeStruct((m, d), F32),
        compiler_params=_cparams(("parallel",)),
        name="ffn_ln",
    )(x, w_up, w_down, ln_g.reshape(1, d), ln_b.reshape(1, d))


def kernel(x_prompt, x_sample, cache_k, cache_v, state_ssm_re, state_ssm_im, state_conv, state_pool, w_in, pool_w, pool_scale, ssm_a_re, ssm_a_im, ssm_log_dt, ssm_b_re, ssm_b_im, ssm_c_re, ssm_c_im, ssm_d, ssm_w_glu, conv_w, conv_b, lambda_q1, lambda_k1, lambda_q2, lambda_k2, subln_w, w_branch, w_out, ln1_g, ln1_b, w_up, w_down, ln2_g, ln2_b):
    depth = w_in.shape[0]
    bp, lp, d = x_prompt.shape
    bs, ls, _ = x_sample.shape
    past = cache_k.shape[2]
    heads, qk_dim = cache_k.shape[3], cache_k.shape[5]
    width_qk = heads * 2 * qk_dim
    pool_width, ssm_width, conv_width = pool_scale.shape[1], ssm_d.shape[1], conv_w.shape[2]
    n_mix = pool_width + ssm_width + 3 * conv_width
    alpha = float((2 * depth) ** 0.25)
    paths = ((bp, lp, 0, 0), (bs, ls, past, bp * lp))

    x = jnp.concatenate([x_prompt.reshape(bp * lp, d), x_sample.reshape(bs * ls, d)], axis=0)
    tables = (_rope_tables(jnp.arange(lp, dtype=jnp.int32)),
              _rope_tables(jnp.tile(past + jnp.arange(ls, dtype=jnp.int32), bs)))

    cache_kt = jnp.transpose(cache_k, (0, 1, 3, 4, 5, 2)).reshape(depth, bs, width_qk, past)
    cache_vr = cache_v.reshape(depth, bs, past * heads, 2 * qk_dim)

    outs = [[[] for _ in range(6)] for _ in paths]
    kv_stack = None
    for l in range(depth):
        wl = w_in[l].astype(BF16)
        mix = _project(x, wl[:, :n_mix], F32, sigmoid=False)
        gates = _project(x, wl[:, n_mix + 3 * width_qk:], BF16, sigmoid=True)
        ab_re, ab_im, bb_re, bb_im = _discretize(ssm_a_re[l], ssm_a_im[l], ssm_log_dt[l], ssm_b_re[l], ssm_b_im[l])
        ssm_w = _ssm_weights(ab_re, ab_im, bb_re, bb_im, ssm_c_re[l], ssm_c_im[l], halves=2)
        lam_p = jnp.stack([lambda_q1[l], lambda_k1[l], lambda_q2[l], lambda_k2[l]]).astype(F32)
        lam_init = 0.8 - 0.6 * math.exp(-0.3 * l)

        w_qkv = wl[:, n_mix:n_mix + 3 * width_qk]
        q_scale = float(qk_dim) ** -0.5 * math.log2(math.e)
        o_pool = o_conv = o_ssm = o_attn = None
        for pi, (bt, L, pos0, row_off) in enumerate(paths):
            if pi == 0:
                q, k_stack, v_stack, kbt, vb = _qkv_project_stacked(x, w_qkv, tables[pi], bt, L, q_scale, l, depth,
                                                                    kv_stack)
                kv_stack = (k_stack, v_stack)
                hist_pool = jnp.zeros((bt, state_pool.shape[2], pool_width), F32)
                hist_conv = jnp.zeros((bt, state_conv.shape[2], conv_width), F32)
                h_re = h_im = jnp.zeros((bt,) + state_ssm_re.shape[2:], F32)
                o_attn = _attention(q, kbt, vb.reshape(bt, L, width_qk), lam_p, subln_w[l], bt, L, L, pos0, lam_init,
                                    x.shape[0])
                kv_new = ()
            else:
                q, kf, vf, kb, vb = _qkv_project(x, w_qkv, tables[pi], row_off, bt * L, q_scale)
                hist_pool, hist_conv, h_re, h_im = state_pool[l], state_conv[l], state_ssm_re[l], state_ssm_im[l]
                o_attn = _decode_attention(q, cache_kt, cache_vr, l, kb, vb, lam_p, subln_w[l], bt, L, past, lam_init,
                                           row_off, o_attn)
                kv_new = (kf.reshape(bt, L, heads, 2, qk_dim), vf.reshape(bt, L, heads, 2 * qk_dim))
            o_pool, o_conv, new_pool, new_conv = _local_mixers(
                mix, hist_pool, hist_conv, pool_w[l], pool_scale[l], conv_w[l], conv_b[l], row_off, bt, L, pos0,
                None if pi == 0 else (o_pool, o_conv))
            o_ssm, new_re, new_im = _ssm_mixer(mix, h_re, h_im, ssm_w, ssm_d[l], ssm_w_glu[l], row_off, bt, L,
                                               None if pi == 0 else (o_ssm,))
            for slot, val in zip(outs[pi], (new_re, new_im, new_conv, new_pool) + kv_new):
                slot.append(val)

        x = _merge(x, gates, o_pool, o_ssm, o_conv, o_attn, w_branch[l].astype(BF16), w_out[l].astype(BF16),
                   ln1_g[l], ln1_b[l], alpha)
        x = _ffn(x, w_up[l].astype(BF16), w_down[l].astype(BF16), ln2_g[l], ln2_b[l], alpha)

    y_prompt = x[:bp * lp].reshape(bp, lp, d)
    y_sample = x[bp * lp:].reshape(bs, ls, d)
    k_stack, v_stack = kv_stack
    k_prompt = jnp.transpose(k_stack.reshape(depth, bp, heads, 2, qk_dim, lp), (0, 1, 5, 2, 3, 4))
    v_prompt = v_stack.reshape(depth, bp, lp, heads, 2 * qk_dim)
    (p_re, p_im, p_conv, p_pool), (s_re, s_im, s_conv, s_pool, s_k, s_v) = (
        [jnp.stack(slot) for slot in path_outs if slot] for path_outs in outs)
    return (y_prompt, y_sample, k_prompt, v_prompt, p_re, p_im, p_conv, p_pool,
            s_k, s_v, s_re, s_im, s_conv, s_pool)
```

```python
import functools
import math

import numpy as np
import jax
import jax.numpy as jnp
from jax import lax
from jax.experimental import pallas as pl
from jax.experimental.pallas import tpu as pltpu

F32 = jnp.float32
BF16 = jnp.bfloat16

LANES = 128
SUBLANES = 8
SCAN_SHIFTS = (1, 2, 4)
CHUNK = 64
POOL_WINDOWS = (2, 4, 8, 16)
POOL_PAD = 16
CONV_PAD = 8
ROT_DIM = 16
ROPE_THETA = 500000.0
LN_EPS = 1e-5
VMEM_LIMIT = 56 * 1024 * 1024

ROW_TILE = 512
ATTN_TILE = 1024
ATTN_Q_TILE = 512
DECODE_TILE = 512
SSM_TILE = 256


def _cparams(sem):
    return pltpu.CompilerParams(dimension_semantics=sem, vmem_limit_bytes=VMEM_LIMIT)


def _const_spec(shape):
    zeros = (0,) * len(shape)
    return pl.BlockSpec(shape, lambda *_: zeros, pipeline_mode=pl.Buffered(1))


def _layer_norm(y, g, b):
    mu = jnp.mean(y, axis=-1, keepdims=True)
    d = y - mu
    var = jnp.mean(d * d, axis=-1, keepdims=True)
    return d * lax.rsqrt(var + LN_EPS) * g + b


def _proj_kernel(x_ref, w_ref, o_ref, *, sigmoid):
    y = jnp.dot(x_ref[...].astype(BF16), w_ref[...], preferred_element_type=F32)
    if sigmoid:
        y = jax.nn.sigmoid(y)
    o_ref[...] = y.astype(o_ref.dtype)


def _project(x, w, out_dtype, sigmoid):
    m, k = x.shape
    n = w.shape[1]
    return pl.pallas_call(
        functools.partial(_proj_kernel, sigmoid=sigmoid),
        grid=(m // ROW_TILE,),
        in_specs=[pl.BlockSpec((ROW_TILE, k), lambda i: (i, 0)), _const_spec(w.shape)],
        out_specs=pl.BlockSpec((ROW_TILE, n), lambda i: (i, 0)),
        out_shape=jax.ShapeDtypeStruct((m, n), out_dtype),
        compiler_params=_cparams(("parallel",)),
        name="proj_sigmoid" if sigmoid else "proj_plain",
    )(x, w)


def _rope(y, cos, sin_up, sin_dn):
    outs = []
    for c in range(y.shape[1] // LANES):
        yc = y[:, c * LANES:(c + 1) * LANES]
        outs.append(yc * cos + pltpu.roll(yc, LANES - ROT_DIM // 2, 1) * sin_up
                    + pltpu.roll(yc, ROT_DIM // 2, 1) * sin_dn)
    return jnp.concatenate(outs, axis=1)


def _qkv_kernel(x_ref, w_ref, cos_ref, su_ref, sd_ref, *refs, width, q_scale, stacked):
    q_ref, kf_ref, vf_ref, kb_ref, vb_ref = refs[-5:]
    xb = x_ref[...].astype(BF16)
    cos, su, sd = cos_ref[...], su_ref[...], sd_ref[...]
    q = jnp.dot(xb, w_ref[:, 0:width], preferred_element_type=F32)
    q_ref[...] = (_rope(q, cos, su, sd) * q_scale).astype(BF16)
    k = _rope(jnp.dot(xb, w_ref[:, width:2 * width], preferred_element_type=F32), cos, su, sd)
    v = jnp.dot(xb, w_ref[:, 2 * width:3 * width], preferred_element_type=F32)
    if stacked:
        kt = k.T
        kf_ref[0, 0] = kt
        kb_ref[0] = kt.astype(BF16)
        heads = width // LANES
        for h in range(heads):
            vf_ref[0, pl.ds(h, v.shape[0], stride=heads), :] = v[:, h * LANES:(h + 1) * LANES]
    else:
        kf_ref[...] = k
        kb_ref[...] = k.astype(BF16)
        vf_ref[...] = v
    vb_ref[...] = v.astype(BF16)


def _qkv_project(x_all, w_qkv, tables, row_off, rows, q_scale):
    k = x_all.shape[1]
    width = w_qkv.shape[1] // 3
    tm = min(ROW_TILE, rows)
    blk0 = row_off // tm
    row_spec = lambda cols: pl.BlockSpec((tm, cols), lambda i: (i, 0))
    tab_spec = pl.BlockSpec((tm, LANES), lambda i: (i, 0))
    return pl.pallas_call(
        functools.partial(_qkv_kernel, width=width, q_scale=q_scale, stacked=False),
        grid=(rows // tm,),
        in_specs=[pl.BlockSpec((tm, k), lambda i: (blk0 + i, 0)), _const_spec(w_qkv.shape),
                  tab_spec, tab_spec, tab_spec],
        out_specs=[row_spec(width)] * 5,
        out_shape=[jax.ShapeDtypeStruct((rows, width), BF16),
                   jax.ShapeDtypeStruct((rows, width), F32), jax.ShapeDtypeStruct((rows, width), F32),
                   jax.ShapeDtypeStruct((rows, width), BF16), jax.ShapeDtypeStruct((rows, width), BF16)],
        compiler_params=_cparams(("parallel",)),
        name="proj_qkv",
    )(x_all, w_qkv, *tables)


def _qkv_project_stacked(x_all, w_qkv, tables, bt, L, q_scale, layer, depth, prev):
    k = x_all.shape[1]
    width = w_qkv.shape[1] // 3
    rows = bt * L
    tm = min(ROW_TILE, L)
    nl = L // tm
    tab_spec = pl.BlockSpec((tm, LANES), lambda i: (i % nl, 0))
    row_spec = pl.BlockSpec((tm, width), lambda i: (i, 0))
    any_spec = pl.BlockSpec(memory_space=pl.ANY)
    n_prev = 0 if prev is None else 2
    return pl.pallas_call(
        functools.partial(_qkv_kernel, width=width, q_scale=q_scale, stacked=True),
        grid=(rows // tm,),
        in_specs=[pl.BlockSpec((tm, k), lambda i: (i, 0)), _const_spec(w_qkv.shape),
                  tab_spec, tab_spec, tab_spec] + [any_spec] * n_prev,
        out_specs=[row_spec,
                   pl.BlockSpec((1, 1, width, tm), lambda i: (layer, i // nl, 0, i % nl)),
                   pl.BlockSpec((1, tm * (width // LANES), LANES), lambda i: (layer, i, 0)),
                   pl.BlockSpec((1, width, tm), lambda i: (i // nl, 0, i % nl)),
                   row_spec],
        out_shape=[jax.ShapeDtypeStruct((rows, width), BF16),
                   jax.ShapeDtypeStruct((depth, bt, width, L), F32),
                   jax.ShapeDtypeStruct((depth, rows * (width // LANES), LANES), F32),
                   jax.ShapeDtypeStruct((bt, width, L), BF16), jax.ShapeDtypeStruct((rows, width), BF16)],
        input_output_aliases={} if prev is None else {5: 1, 6: 2},
        compiler_params=_cparams(("parallel",)),
        name="proj_qkv_stacked",
    )(x_all, w_qkv, *tables, *(() if prev is None else prev))


def _rope_tables(pos):
    half = ROT_DIM // 2
    inv = ROPE_THETA ** (-jnp.arange(0, ROT_DIM, 2, dtype=F32) / ROT_DIM)
    ang = pos.astype(F32)[:, None] * inv[None, :]
    cos, sin = jnp.cos(ang), jnp.sin(ang)
    n = pos.shape[0]
    ones = jnp.ones((n, 64 - ROT_DIM), F32)
    zeros = jnp.zeros((n, 64 - half), F32)
    cos64 = jnp.concatenate([cos, cos, ones], axis=1)
    up64 = jnp.concatenate([-sin, zeros], axis=1)
    dn64 = jnp.concatenate([jnp.zeros((n, half), F32), sin, jnp.zeros((n, 64 - ROT_DIM), F32)], axis=1)
    return tuple(jnp.concatenate([t, t], axis=1) for t in (cos64, up64, dn64))


def _local_kernel(z_ref, h_ref, b_ref, c_ref, hp_ref, hc_ref, pw_ref, ps_ref, cw_ref, cb_ref, *refs, tl, pos0):
    op_ref, oc_ref, np_ref, nc_ref, zbuf, cbuf = refs[-6:]
    i = pl.program_id(1)
    last = pl.num_programs(1) - 1

    @pl.when(i == 0)
    def _():
        zbuf[0:POOL_PAD, :] = hp_ref[0]
        cbuf[0:CONV_PAD, :] = hc_ref[0]

    @pl.when(i > 0)
    def _():
        zbuf[0:POOL_PAD, :] = zbuf[tl:tl + POOL_PAD, :]
        cbuf[0:CONV_PAD, :] = cbuf[tl:tl + CONV_PAD, :]

    z = z_ref[...]
    zbuf[POOL_PAD:POOL_PAD + tl, :] = z
    cbuf[CONV_PAD:CONV_PAD + tl, :] = c_ref[...] * h_ref[...]

    pos = pos0 + i * tl + lax.broadcasted_iota(jnp.int32, (tl, 1), 0)
    gw = z.shape[1] // len(POOL_WINDOWS)
    outs = []
    for g, w in enumerate(POOL_WINDOWS):
        cols = slice(g * gw, (g + 1) * gw)
        s = z[:, cols]
        for j in range(1, w):
            s = s + zbuf[POOL_PAD - j:POOL_PAD - j + tl, cols]
        cnt = jnp.minimum(w, pos + 1).astype(F32)
        u = s / cnt - z[:, cols]
        outs.append(jnp.dot(u.astype(BF16), pw_ref[g], preferred_element_type=F32))
    op_ref[...] = (jnp.concatenate(outs, axis=1) * ps_ref[...]).astype(op_ref.dtype)

    y = cb_ref[...] + cbuf[CONV_PAD - 2:CONV_PAD - 2 + tl, :] * cw_ref[0:1, :]
    y = y + cbuf[CONV_PAD - 1:CONV_PAD - 1 + tl, :] * cw_ref[1:2, :]
    y = y + cbuf[CONV_PAD:CONV_PAD + tl, :] * cw_ref[2:3, :]
    oc_ref[...] = (b_ref[...] * y).astype(oc_ref.dtype)

    @pl.when(i == last)
    def _():
        np_ref[0] = zbuf[tl + 1:tl + POOL_PAD, :]
        nc_ref[0] = cbuf[tl + CONV_PAD - 2:tl + CONV_PAD, :]


def _alias_args(prev, n_inputs):
    if prev is None:
        return [], {}, ()
    return ([pl.BlockSpec(memory_space=pl.ANY)] * len(prev),
            {n_inputs + k: k for k in range(len(prev))}, tuple(prev))


def _local_mixers(mix, hist_pool, hist_conv, pool_w, pool_scale, conv_w, conv_b, row_off, bt, L, pos0, prev):
    width = pool_scale.shape[-1]
    tl = min(ROW_TILE, L)
    nl = L // tl
    blk0 = row_off // tl
    col_spec = lambda cb: pl.BlockSpec((tl, width), lambda b, i: (blk0 + b * nl + i, cb))
    out_spec = pl.BlockSpec((tl, width), lambda b, i: (blk0 + b * nl + i, 0))
    alias_specs, alias_map, alias_in = _alias_args(prev, 10)
    hp = jnp.pad(hist_pool, ((0, 0), (POOL_PAD - hist_pool.shape[1], 0), (0, 0)))
    hc = jnp.pad(hist_conv, ((0, 0), (CONV_PAD - hist_conv.shape[1], 0), (0, 0)))
    n_hp, n_hc = hist_pool.shape[1], hist_conv.shape[1]
    return pl.pallas_call(
        functools.partial(_local_kernel, tl=tl, pos0=pos0),
        grid=(bt, nl),
        in_specs=[col_spec(0), col_spec(2), col_spec(3), col_spec(4),
                  pl.BlockSpec((1, POOL_PAD, width), lambda b, i: (b, 0, 0)),
                  pl.BlockSpec((1, CONV_PAD, width), lambda b, i: (b, 0, 0)),
                  _const_spec(pool_w.shape), _const_spec((1, width)),
                  _const_spec(conv_w.shape), _const_spec((1, width))] + alias_specs,
        out_specs=[out_spec, out_spec,
                   pl.BlockSpec((1, n_hp, width), lambda b, i: (b, 0, 0)),
                   pl.BlockSpec((1, n_hc, width), lambda b, i: (b, 0, 0))],
        out_shape=[jax.ShapeDtypeStruct((mix.shape[0], width), BF16), jax.ShapeDtypeStruct((mix.shape[0], width), BF16),
                   jax.ShapeDtypeStruct((bt, n_hp, width), F32), jax.ShapeDtypeStruct((bt, n_hc, width), F32)],
        scratch_shapes=[pltpu.VMEM((POOL_PAD + tl, width), F32), pltpu.VMEM((CONV_PAD + tl, width), F32)],
        input_output_aliases=alias_map,
        compiler_params=_cparams(("parallel", "arbitrary")),
        name="local_mixers",
    )(mix, mix, mix, mix, hp, hc, pool_w.astype(BF16), pool_scale.reshape(1, width),
      conv_w, conv_b.reshape(1, width), *alias_in)


def _discretize_kernel(are_ref, aim_ref, ldt_ref, bre_ref, bim_ref, abr_ref, abi_ref, bbr_ref, bbi_ref):
    a_re, a_im = are_ref[...], aim_ref[...]
    dt = jnp.exp(ldt_ref[...])
    mag = jnp.exp(a_re * dt)
    ab_re = mag * jnp.cos(a_im * dt)
    ab_im = mag * jnp.sin(a_im * dt)
    den = a_re * a_re + a_im * a_im
    cr = ((ab_re - 1.0) * a_re + ab_im * a_im) / den
    ci = (ab_im * a_re - (ab_re - 1.0) * a_im) / den
    b_re, b_im = bre_ref[...], bim_ref[...]
    abr_ref[...] = ab_re
    abi_ref[...] = ab_im
    bbr_ref[...] = cr * b_re - ci * b_im
    bbi_ref[...] = cr * b_im + ci * b_re


def _discretize(a_re, a_im, log_dt, b_re, b_im):
    g, p = a_re.shape
    n = b_re.shape[-1]
    col = lambda t: t.reshape(g * p, 1)
    ldt = jnp.broadcast_to(log_dt[:, None], (g, p))
    shapes = [jax.ShapeDtypeStruct((g * p, 1), F32)] * 2 + [jax.ShapeDtypeStruct((g * p, n), F32)] * 2
    return pl.pallas_call(_discretize_kernel, out_shape=shapes, name="ssm_discretize")(
        col(a_re), col(a_im), col(ldt), b_re.reshape(g * p, n), b_im.reshape(g * p, n))


def _scan_tables(are_ref, aim_ref, tab_ref):
    sub = lax.broadcasted_iota(jnp.int32, (SUBLANES, LANES), 0)
    for j in range(tab_ref.shape[0]):
        ar, ai = are_ref[j], aim_ref[j]
        powers = [(ar, ai)]
        for _ in range(SUBLANES - 1):
            pr, pi = powers[-1]
            powers.append((pr * ar - pi * ai, pr * ai + pi * ar))
        for t, d in enumerate(SCAN_SHIFTS):
            dr, di = powers[d - 1]
            tab_ref[j, 2 * t] = jnp.where(sub >= d, dr, 0.0)
            tab_ref[j, 2 * t + 1] = jnp.where(sub >= d, di, 0.0)
        tab_ref[j, 2 * len(SCAN_SHIFTS)] = jnp.concatenate([p[0] for p in powers], axis=0)
        tab_ref[j, 2 * len(SCAN_SHIFTS) + 1] = jnp.concatenate([p[1] for p in powers], axis=0)


def _ssm_kernel(u_ref, hre_ref, him_ref, are_ref, aim_ref, bcat_ref, ccat_ref, d_ref, wg_ref, *refs,
                tl, halves, chunks):
    o_ref, nre_ref, nim_ref, s_ref, cre_ref, cim_ref, tab_ref = refs[-7:]
    i = pl.program_id(1)

    @pl.when(i == 0)
    def _():
        cre_ref[...] = hre_ref[0]
        cim_ref[...] = him_ref[0]

    u = u_ref[...]
    ub = u.astype(BF16)
    kw = ub.shape[1] // halves
    per_half = 2 * chunks
    for h in range(halves):
        bu = jnp.dot(ub[:, h * kw:(h + 1) * kw], bcat_ref[h], preferred_element_type=F32)
        for q in range(per_half):
            s_ref[h * per_half + q] = bu[:, q * LANES:(q + 1) * LANES]

    @pl.when(i == 0)
    def _():
        _scan_tables(are_ref, aim_ref, tab_ref)

    def scan_chunk(j, carry):
        ire = (j // chunks) * per_half + (j % chunks)
        iim = ire + chunks
        cr = jnp.broadcast_to(cre_ref[j], (SUBLANES, LANES))
        ci = jnp.broadcast_to(cim_ref[j], (SUBLANES, LANES))
        steps = [(d, tab_ref[j, 2 * t], tab_ref[j, 2 * t + 1]) for t, d in enumerate(SCAN_SHIFTS)]
        pw_re, pw_im = tab_ref[j, 2 * len(SCAN_SHIFTS)], tab_ref[j, 2 * len(SCAN_SHIFTS) + 1]
        for r in range(tl // SUBLANES):
            rows = pl.ds(SUBLANES * r, SUBLANES)
            xr, xi = s_ref[ire, rows, :], s_ref[iim, rows, :]
            for d, mr, mi in steps:
                sr, si = pltpu.roll(xr, d, 0), pltpu.roll(xi, d, 0)
                xr, xi = xr + mr * sr - mi * si, xi + mr * si + mi * sr
            xr, xi = xr + pw_re * cr - pw_im * ci, xi + pw_re * ci + pw_im * cr
            s_ref[ire, rows, :] = xr
            s_ref[iim, rows, :] = xi
            cr = jnp.broadcast_to(xr[SUBLANES - 1:SUBLANES], (SUBLANES, LANES))
            ci = jnp.broadcast_to(xi[SUBLANES - 1:SUBLANES], (SUBLANES, LANES))
        cre_ref[j] = cr[0:1]
        cim_ref[j] = ci[0:1]
        return carry

    for j in range(halves * chunks):
        scan_chunk(j, 0)

    ys = []
    for h in range(halves):
        st = jnp.concatenate([s_ref[h * per_half + q] for q in range(per_half)], axis=1)
        ys.append(jnp.dot(st.astype(BF16), ccat_ref[h], preferred_element_type=F32))
    y = jnp.concatenate(ys, axis=1) + d_ref[...] * u
    v = 0.5 * y * (1.0 + jnp.tanh(math.sqrt(2.0 / math.pi) * (y + 0.044715 * (y * y * y))))
    gate = jax.nn.sigmoid(jnp.dot(v.astype(BF16), wg_ref[...], preferred_element_type=F32))
    o_ref[...] = (v * gate).astype(o_ref.dtype)

    @pl.when(i == pl.num_programs(1) - 1)
    def _():
        nre_ref[0] = cre_ref[...]
        nim_ref[0] = cim_ref[...]


def _ssm_weights(ab_re, ab_im, bb_re, bb_im, c_re, c_im, halves):
    g, n, p = c_re.shape
    gh = g // halves
    eye = jnp.eye(gh, dtype=F32)
    bcat, ccat = [], []
    for h in range(halves):
        sl = slice(h * gh, (h + 1) * gh)
        dense_b = lambda t: jnp.einsum('gpn,gk->gnkp', t.reshape(g, p, n)[sl], eye).reshape(gh * n, gh * p)
        dense_c = lambda t: jnp.einsum('gnp,gk->gpkn', t[sl], eye).reshape(gh * p, gh * n)
        bcat.append(jnp.concatenate([dense_b(bb_re), dense_b(bb_im)], axis=1))
        ccat.append(jnp.concatenate([dense_c(c_re), -dense_c(c_im)], axis=0))
    nch = g * p // LANES
    return (ab_re.reshape(nch, 1, LANES), ab_im.reshape(nch, 1, LANES),
            jnp.stack(bcat).astype(BF16), jnp.stack(ccat).astype(BF16))


def _ssm_mixer(mix, h_re, h_im, ssm_w, d_skip, w_glu, row_off, bt, L, prev):
    a_re, a_im, bcat, ccat = ssm_w
    alias_specs, alias_map, alias_in = _alias_args(prev, 9)
    halves = bcat.shape[0]
    width = d_skip.shape[-1]
    nch = a_re.shape[0]
    chunks = nch // halves
    g, p = h_re.shape[1], h_re.shape[2]
    tl = min(SSM_TILE, L)
    nl = L // tl
    blk0 = row_off // tl
    state_spec = pl.BlockSpec((1, nch, 1, LANES), lambda b, i: (b, 0, 0, 0))
    o, n_re, n_im = pl.pallas_call(
        functools.partial(_ssm_kernel, tl=tl, halves=halves, chunks=chunks),
        grid=(bt, nl),
        in_specs=[pl.BlockSpec((tl, width), lambda b, i: (blk0 + b * nl + i, 1)),
                  state_spec, state_spec,
                  _const_spec(a_re.shape), _const_spec(a_im.shape),
                  _const_spec(bcat.shape), _const_spec(ccat.shape),
                  _const_spec((1, width)), _const_spec(w_glu.shape)] + alias_specs,
        out_specs=[pl.BlockSpec((tl, width), lambda b, i: (blk0 + b * nl + i, 0)), state_spec, state_spec],
        out_shape=[jax.ShapeDtypeStruct((mix.shape[0], width), BF16),
                   jax.ShapeDtypeStruct((bt, nch, 1, LANES), F32), jax.ShapeDtypeStruct((bt, nch, 1, LANES), F32)],
        scratch_shapes=[pltpu.VMEM((2 * nch, tl, LANES), F32),
                        pltpu.VMEM((nch, 1, LANES), F32), pltpu.VMEM((nch, 1, LANES), F32),
                        pltpu.VMEM((nch, 2 * len(SCAN_SHIFTS) + 2, SUBLANES, LANES), F32)],
        input_output_aliases=alias_map,
        compiler_params=_cparams(("parallel", "arbitrary")),
        name="ssm_mixer",
    )(mix, h_re.astype(F32).reshape(bt, nch, 1, LANES), h_im.astype(F32).reshape(bt, nch, 1, LANES),
      a_re, a_im, bcat, ccat, d_skip.reshape(1, width), w_glu.astype(BF16), *alias_in)
    return o, n_re.reshape(bt, g, p), n_im.reshape(bt, g, p)


def _attn_init(m_ref, l_ref, acc_ref):
    m_ref[...] = jnp.full(m_ref.shape, -jnp.inf, F32)
    l_ref[...] = jnp.zeros(l_ref.shape, F32)
    acc_ref[...] = jnp.zeros(acc_ref.shape, F32)


def _attn_update(q_of, k_of, v_of, visible, heads, m_ref, l_ref, acc_ref):
    low_lanes = lax.broadcasted_iota(jnp.int32, (1, LANES), 1) < (LANES // 2)
    if visible is not None:
        visible = jnp.concatenate([visible, visible], axis=0)
    for h in range(heads):
        qh, kh, vh = q_of(h), k_of(h), v_of(h)
        zero = jnp.zeros_like(qh)
        qm = jnp.concatenate([jnp.where(low_lanes, qh, zero), jnp.where(low_lanes, zero, qh)], axis=0)
        s = jnp.dot(qm, kh, preferred_element_type=F32)
        if visible is not None:
            s = jnp.where(visible, s, -jnp.inf)
        m_old = m_ref[h]
        m_new = jnp.maximum(m_old, jnp.max(s, axis=1, keepdims=True))
        alpha = jnp.exp2(m_old - m_new)
        ps = [jnp.exp2(s[:, j * LANES:(j + 1) * LANES] - m_new) for j in range(s.shape[1] // LANES)]
        lsum = ps[0]
        for pj in ps[1:]:
            lsum = lsum + pj
        l_ref[h] = alpha * l_ref[h] + lsum
        p = jnp.concatenate([pj.astype(BF16) for pj in ps], axis=1)
        acc_ref[h] = alpha * acc_ref[h] + jnp.dot(p, vh, preferred_element_type=F32)
        m_ref[h] = m_new


def _attn_finalize(lam_ref, sw_ref, o_ref, l_ref, acc_ref, heads, lam_init):
    lp = lam_ref[...]
    lam = (jnp.exp(jnp.sum(lp[0:1] * lp[1:2], axis=1, keepdims=True))
           - jnp.exp(jnp.sum(lp[2:3] * lp[3:4], axis=1, keepdims=True)) + lam_init)
    rows = o_ref.shape[0]
    for h in range(heads):
        oh = acc_ref[h] / jnp.sum(l_ref[h], axis=1, keepdims=True)
        o = oh[:rows] - lam * oh[rows:]
        o = o * lax.rsqrt(jnp.mean(o * o, axis=1, keepdims=True) + LN_EPS) * sw_ref[...] * (1.0 - lam_init)
        o_ref[:, h * LANES:(h + 1) * LANES] = o.astype(o_ref.dtype)


def _head_cols(ref):
    lead = (0,) * (len(ref.shape) - 2)
    return lambda h: ref[(*lead, slice(None), slice(h * LANES, (h + 1) * LANES))].astype(BF16)


def _head_rows(ref):
    lead = (0,) * (len(ref.shape) - 2)
    return lambda h: ref[(*lead, slice(h * LANES, (h + 1) * LANES), slice(None))].astype(BF16)


def _attn_kernel(qi_ref, ki_ref, fl_ref, q_ref, k_ref, v_ref, lam_ref, sw_ref, o_ref, m_ref, l_ref, acc_ref,
                 *, tq, tk, heads, q_pos0, lk, lam_init):
    step = pl.program_id(1)
    qi, ki, fl = qi_ref[step], ki_ref[step], fl_ref[step]

    @pl.when(ki == 0)
    def _():
        _attn_init(m_ref, l_ref, acc_ref)

    def accumulate(masked, keys):
        visible = None
        if masked:
            q_pos = q_pos0 + qi * tq + lax.broadcasted_iota(jnp.int32, (tq, 1), 0)
            k_pos = ki * tk + lax.broadcasted_iota(jnp.int32, (1, keys), 1)
            visible = (k_pos < (q_pos // CHUNK + 1) * CHUNK) & (k_pos < lk)
        k_of = lambda h: k_ref[0, h * LANES:(h + 1) * LANES, 0:keys]
        v_of = lambda h: v_ref[0, 0:keys, h * LANES:(h + 1) * LANES]
        _attn_update(_head_cols(q_ref), k_of, v_of, visible, heads, m_ref, l_ref, acc_ref)

    @pl.when((fl & 1) == 0)
    def _():
        accumulate(False, tk)

    @pl.when(((fl & 1) != 0) & ((fl & 4) == 0))
    def _():
        accumulate(True, tk)

    @pl.when((fl & 4) != 0)
    def _():
        accumulate(True, tk // 2)

    @pl.when((fl & 2) != 0)
    def _():
        _attn_finalize(lam_ref, sw_ref, o_ref, l_ref, acc_ref, heads, lam_init)


def _decode_attn_kernel(q_ref, ck_ref, cv_ref, nk_ref, nv_ref, lam_ref, sw_ref, joint_ref, o_ref, m_ref, l_ref,
                        acc_ref, *, tq, heads, past, n_cache, lam_init):
    j = pl.program_id(1)

    @pl.when(j == 0)
    def _():
        _attn_init(m_ref, l_ref, acc_ref)

    @pl.when(j < n_cache)
    def _():
        n_keys = cv_ref.shape[2] // heads
        cached_v = lambda h: cv_ref[0, 0, pl.ds(h, n_keys, stride=heads), :].astype(BF16)
        _attn_update(_head_cols(q_ref), _head_rows(ck_ref), cached_v, None, heads, m_ref, l_ref, acc_ref)

    @pl.when(j == n_cache)
    def _():
        nk = nk_ref.shape[2]
        t = lax.broadcasted_iota(jnp.int32, (tq, 1), 0)
        i = lax.broadcasted_iota(jnp.int32, (1, nk), 1)
        visible = (past + i < ((past + t) // CHUNK + 1) * CHUNK) & (i < tq)
        _attn_update(_head_cols(q_ref), _head_rows(nk_ref), _head_cols(nv_ref), visible, heads, m_ref, l_ref, acc_ref)
        _attn_finalize(lam_ref, sw_ref, o_ref, l_ref, acc_ref, heads, lam_init)


def _attn_schedule(L, lk, tq, tk, q_pos0):
    qi, ki, fl = [], [], []
    for a in range(L // tq):
        first_end = ((q_pos0 + a * tq) // CHUNK + 1) * CHUNK
        last_end = min(((q_pos0 + a * tq + tq - 1) // CHUNK + 1) * CHUNK, lk)
        nk = -(-last_end // tk)
        for b in range(nk):
            full = (b + 1) * tk <= min(first_end, lk)
            half = not full and last_end <= b * tk + tk // 2
            qi.append(a); ki.append(b); fl.append((0 if full else 1) | (2 if b == nk - 1 else 0) | (4 if half else 0))
    return tuple(jnp.asarray(np.asarray(t, np.int32)) for t in (qi, ki, fl))


def _attention(q, kt_all, v_all, lam_p, subln_w, bt, L, lk, q_pos0, lam_init, total_rows):
    width = q.shape[-1]
    heads = width // LANES
    tq = min(ATTN_Q_TILE, L)
    tk = ATTN_TILE
    lk_pad = kt_all.shape[2]
    nq = L // tq
    qi, ki, fl = _attn_schedule(L, lk, tq, tk, q_pos0)
    grid_spec = pltpu.PrefetchScalarGridSpec(
        num_scalar_prefetch=3,
        grid=(bt, int(qi.shape[0])),
        in_specs=[pl.BlockSpec((1, tq, width), lambda b, s, qi, ki, fl: (b, qi[s], 0)),
                  pl.BlockSpec((1, width, tk), lambda b, s, qi, ki, fl: (b, 0, ki[s])),
                  pl.BlockSpec((1, tk, width), lambda b, s, qi, ki, fl: (b, ki[s], 0)),
                  pl.BlockSpec(lam_p.shape, lambda b, s, qi, ki, fl: (0, 0)),
                  pl.BlockSpec((1, LANES), lambda b, s, qi, ki, fl: (0, 0))],
        out_specs=pl.BlockSpec((tq, width), lambda b, s, qi, ki, fl: (b * nq + qi[s], 0)),
        scratch_shapes=[pltpu.VMEM((heads, 2 * tq, LANES), F32)] * 3)
    assert lk_pad % tk == 0 and L % tq == 0
    return pl.pallas_call(
        functools.partial(_attn_kernel, tq=tq, tk=tk, heads=heads, q_pos0=q_pos0, lk=lk, lam_init=lam_init),
        grid_spec=grid_spec,
        out_shape=jax.ShapeDtypeStruct((total_rows, width), BF16),
        compiler_params=_cparams(("parallel", "arbitrary")),
        name="diff_attention",
    )(qi, ki, fl, q.reshape(bt, L, width), kt_all, v_all, lam_p, subln_w.reshape(1, LANES))


def _decode_attention(q, cache_kt, cache_v, layer, new_k, new_v, lam_p, subln_w, bt, L, past, lam_init, row_off, joint):
    width = q.shape[-1]
    heads = width // LANES
    tk = DECODE_TILE
    assert past % tk == 0 and past >= tk and L <= LANES and past % CHUNK == 0 and row_off % L == 0
    n_cache = past // tk
    new_kt = jnp.pad(jnp.swapaxes(new_k.reshape(bt, L, width), 1, 2), ((0, 0), (0, 0), (0, LANES - L)))
    new_v = jnp.pad(new_v.reshape(bt, L, width), ((0, 0), (0, LANES - L), (0, 0)))
    row_spec = pl.BlockSpec((1, L, width), lambda b, j: (b, 0, 0))
    scratch = pltpu.VMEM((heads, 2 * L, LANES), F32)
    return pl.pallas_call(
        functools.partial(_decode_attn_kernel, tq=L, heads=heads, past=past, n_cache=n_cache, lam_init=lam_init),
        grid=(bt, n_cache + 1),
        in_specs=[row_spec,
                  pl.BlockSpec((1, 1, width, tk), lambda b, j: (layer, b, 0, jnp.minimum(j, n_cache - 1))),
                  pl.BlockSpec((1, 1, tk * heads, LANES), lambda b, j: (layer, b, jnp.minimum(j, n_cache - 1), 0)),
                  pl.BlockSpec((1, width, LANES), lambda b, j: (b, 0, 0)),
                  pl.BlockSpec((1, LANES, width), lambda b, j: (b, 0, 0)),
                  pl.BlockSpec(lam_p.shape, lambda b, j: (0, 0)), pl.BlockSpec((1, LANES), lambda b, j: (0, 0)),
                  pl.BlockSpec(memory_space=pl.ANY)],
        out_specs=pl.BlockSpec((L, width), lambda b, j: (row_off // L + b, 0)),
        out_shape=jax.ShapeDtypeStruct(joint.shape, BF16),
        scratch_shapes=[scratch, scratch, scratch],
        input_output_aliases={7: 0},
        compiler_params=_cparams(("parallel", "arbitrary")),
        name="decode_attention",
    )(q.reshape(bt, L, width), cache_kt, cache_v, new_kt, new_v, lam_p, subln_w.reshape(1, LANES), joint)


def _merge_kernel(x_ref, g_ref, op_ref, os_ref, oc_ref, oa_ref, wb_ref, wo_ref, lg_ref, lb_ref, o_ref,
                  *, d_model, alpha, offs):
    merged = None
    for b, (o_b, (lo, hi)) in enumerate(zip((op_ref, os_ref, oc_ref, oa_ref), offs)):
        t = jnp.dot(o_b[...], wb_ref[lo:hi, :], preferred_element_type=F32)
        t = t * g_ref[:, b * d_model:(b + 1) * d_model].astype(F32)
        merged = t if merged is None else merged + t
    y = alpha * x_ref[...] + jnp.dot(merged.astype(BF16), wo_ref[...], preferred_element_type=F32)
    o_ref[...] = _layer_norm(y, lg_ref[...], lb_ref[...])


def _merge(x, gates, o_pool, o_ssm, o_conv, o_attn, w_branch, w_out, ln_g, ln_b, alpha):
    m, d = x.shape
    widths = [o_pool.shape[1], o_ssm.shape[1], o_conv.shape[1], o_attn.shape[1]]
    ends = np.cumsum(widths)
    offs = tuple((int(e - w), int(e)) for e, w in zip(ends, widths))
    row = lambda cols: pl.BlockSpec((ROW_TILE, cols), lambda i: (i, 0))
    return pl.pallas_call(
        functools.partial(_merge_kernel, d_model=d, alpha=alpha, offs=offs),
        grid=(m // ROW_TILE,),
        in_specs=[row(d), row(gates.shape[1])] + [row(w) for w in widths]
                 + [_const_spec(w_branch.shape), _const_spec(w_out.shape), _const_spec((1, d)), _const_spec((1, d))],
        out_specs=row(d),
        out_shape=jax.ShapeDtypeStruct((m, d), F32),
        compiler_params=_cparams(("parallel",)),
        name="merge_out_ln",
    )(x, gates, o_pool, o_ssm, o_conv, o_attn, w_branch, w_out, ln_g.reshape(1, d), ln_b.reshape(1, d))


def _ffn_kernel(x_ref, wu_ref, wd_ref, lg_ref, lb_ref, o_ref, *, alpha, chunk):
    x = x_ref[...]
    xb = x.astype(BF16)
    acc = alpha * x
    for c in range(wu_ref.shape[1] // chunk):
        hid = jnp.dot(xb, wu_ref[:, c * chunk:(c + 1) * chunk], preferred_element_type=F32)
        hid = jnp.square(jnp.maximum(hid, 0.0)).astype(BF16)
        acc = acc + jnp.dot(hid, wd_ref[c * chunk:(c + 1) * chunk, :], preferred_element_type=F32)
    o_ref[...] = _layer_norm(acc, lg_ref[...], lb_ref[...])


def _ffn(x, w_up, w_down, ln_g, ln_b, alpha):
    m, d = x.shape
    row = pl.BlockSpec((ROW_TILE, d), lambda i: (i, 0))
    return pl.pallas_call(
        functools.partial(_ffn_kernel, alpha=alpha, chunk=1024),
        grid=(m // ROW_TILE,),
        in_specs=[row, _const_spec(w_up.shape), _const_spec(w_down.shape), _const_spec((1, d)), _const_spec((1, d))],
        out_specs=row,
        out_shape=jax.ShapeDtypeStruct((m, d), F32),
        compiler_params=_cparams(("parallel",)),
        name="ffn_ln",
    )(x, w_up, w_down, ln_g.reshape(1, d), ln_b.reshape(1, d))


def kernel(x_prompt, x_sample, cache_k, cache_v, state_ssm_re, state_ssm_im, state_conv, state_pool, w_in, pool_w, pool_scale, ssm_a_re, ssm_a_im, ssm_log_dt, ssm_b_re, ssm_b_im, ssm_c_re, ssm_c_im, ssm_d, ssm_w_glu, conv_w, conv_b, lambda_q1, lambda_k1, lambda_q2, lambda_k2, subln_w, w_branch, w_out, ln1_g, ln1_b, w_up, w_down, ln2_g, ln2_b):
    depth = w_in.shape[0]
    bp, lp, d = x_prompt.shape
    bs, ls, _ = x_sample.shape
    past = cache_k.shape[2]
    heads, qk_dim = cache_k.shape[3], cache_k.shape[5]
    width_qk = heads * 2 * qk_dim
    pool_width, ssm_width, conv_width = pool_scale.shape[1], ssm_d.shape[1], conv_w.shape[2]
    n_mix = pool_width + ssm_width + 3 * conv_width
    alpha = float((2 * depth) ** 0.25)
    paths = ((bp, lp, 0, 0), (bs, ls, past, bp * lp))

    x = jnp.concatenate([x_prompt.reshape(bp * lp, d), x_sample.reshape(bs * ls, d)], axis=0)
    tables = (_rope_tables(jnp.arange(lp, dtype=jnp.int32)),
              _rope_tables(jnp.tile(past + jnp.arange(ls, dtype=jnp.int32), bs)))

    cache_kt = jnp.transpose(cache_k, (0, 1, 3, 4, 5, 2)).reshape(depth, bs, width_qk, past)
    cache_vr = cache_v.reshape(depth, bs, past * heads, 2 * qk_dim)

    outs = [[[] for _ in range(6)] for _ in paths]
    kv_stack = None
    for l in range(depth):
        wl = w_in[l].astype(BF16)
        mix = _project(x, wl[:, :n_mix], F32, sigmoid=False)
        gates = _project(x, wl[:, n_mix + 3 * width_qk:], BF16, sigmoid=True)
        ab_re, ab_im, bb_re, bb_im = _discretize(ssm_a_re[l], ssm_a_im[l], ssm_log_dt[l], ssm_b_re[l], ssm_b_im[l])
        ssm_w = _ssm_weights(ab_re, ab_im, bb_re, bb_im, ssm_c_re[l], ssm_c_im[l], halves=2)
        lam_p = jnp.stack([lambda_q1[l], lambda_k1[l], lambda_q2[l], lambda_k2[l]]).astype(F32)
        lam_init = 0.8 - 0.6 * math.exp(-0.3 * l)

        w_qkv = wl[:, n_mix:n_mix + 3 * width_qk]
        q_scale = float(qk_dim) ** -0.5 * math.log2(math.e)
        o_pool = o_conv = o_ssm = o_attn = None
        for pi, (bt, L, pos0, row_off) in enumerate(paths):
            if pi == 0:
                q, k_stack, v_stack, kbt, vb = _qkv_project_stacked(x, w_qkv, tables[pi], bt, L, q_scale, l, depth,
                                                                    kv_stack)
                kv_stack = (k_stack, v_stack)
                hist_pool = jnp.zeros((bt, state_pool.shape[2], pool_width), F32)
                hist_conv = jnp.zeros((bt, state_conv.shape[2], conv_width), F32)
                h_re = h_im = jnp.zeros((bt,) + state_ssm_re.shape[2:], F32)
                o_attn = _attention(q, kbt, vb.reshape(bt, L, width_qk), lam_p, subln_w[l], bt, L, L, pos0, lam_init,
                                    x.shape[0])
                kv_new = ()
            else:
                q, kf, vf, kb, vb = _qkv_project(x, w_qkv, tables[pi], row_off, bt * L, q_scale)
                hist_pool, hist_conv, h_re, h_im = state_pool[l], state_conv[l], state_ssm_re[l], state_ssm_im[l]
                o_attn = _decode_attention(q, cache_kt, cache_vr, l, kb, vb, lam_p, subln_w[l], bt, L, past, lam_init,
                                           row_off, o_attn)
                kv_new = (kf.reshape(bt, L, heads, 2, qk_dim), vf.reshape(bt, L, heads, 2 * qk_dim))
            o_pool, o_conv, new_pool, new_conv = _local_mixers(
                mix, hist_pool, hist_conv, pool_w[l], pool_scale[l], conv_w[l], conv_b[l], row_off, bt, L, pos0,
                None if pi == 0 else (o_pool, o_conv))
            o_ssm, new_re, new_im = _ssm_mixer(mix, h_re, h_im, ssm_w, ssm_d[l], ssm_w_glu[l], row_off, bt, L,
                                               None if pi == 0 else (o_ssm,))
            for slot, val in zip(outs[pi], (new_re, new_im, new_conv, new_pool) + kv_new):
                slot.append(val)

        x = _merge(x, gates, o_pool, o_ssm, o_conv, o_attn, w_branch[l].astype(BF16), w_out[l].astype(BF16),
                   ln1_g[l], ln1_b[l], alpha)
        x = _ffn(x, w_up[l].astype(BF16), w_down[l].astype(BF16), ln2_g[l], ln2_b[l], alpha)

    y_prompt = x[:bp * lp].reshape(bp, lp, d)
    y_sample = x[bp * lp:].reshape(bs, ls, d)
    k_stack, v_stack = kv_stack
    k_prompt = jnp.transpose(k_stack.reshape(depth, bp, heads, 2, qk_dim, lp), (0, 1, 5, 2, 3, 4))
    v_prompt = v_stack.reshape(depth, bp, lp, heads, 2 * qk_dim)
    (p_re, p_im, p_conv, p_pool), (s_re, s_im, s_conv, s_pool, s_k, s_v) = (
        [jnp.stack(slot) for slot in path_outs if slot] for path_outs in outs)
    return (y_prompt, y_sample, k_prompt, v_prompt, p_re, p_im, p_conv, p_pool,
            s_k, s_v, s_re, s_im, s_conv, s_pool)
```

```python
import functools
import math

import numpy as np
import jax
import jax.numpy as jnp
from jax import lax
from jax.experimental import pallas as pl
from jax.experimental.pallas import tpu as pltpu

F32 = jnp.float32
BF16 = jnp.bfloat16

LANES = 128
SUBLANES = 8
SCAN_SHIFTS = (1, 2, 4)
CHUNK = 64
POOL_WINDOWS = (2, 4, 8, 16)
POOL_PAD = 16
CONV_PAD = 8
ROT_DIM = 16
ROPE_THETA = 500000.0
LN_EPS = 1e-5
VMEM_LIMIT = 56 * 1024 * 1024

ROW_TILE = 512
ATTN_TILE = 1024
ATTN_Q_TILE = 512
DECODE_TILE = 1024
SSM_TILE = 256


def _cparams(sem):
    return pltpu.CompilerParams(dimension_semantics=sem, vmem_limit_bytes=VMEM_LIMIT)


def _const_spec(shape):
    zeros = (0,) * len(shape)
    return pl.BlockSpec(shape, lambda *_: zeros, pipeline_mode=pl.Buffered(1))


def _layer_norm(y, g, b):
    mu = jnp.mean(y, axis=-1, keepdims=True)
    d = y - mu
    var = jnp.mean(d * d, axis=-1, keepdims=True)
    return d * lax.rsqrt(var + LN_EPS) * g + b


def _proj_kernel(x_ref, w_ref, o_ref, *, sigmoid):
    y = jnp.dot(x_ref[...].astype(BF16), w_ref[...], preferred_element_type=F32)
    if sigmoid:
        y = jax.nn.sigmoid(y)
    o_ref[...] = y.astype(o_ref.dtype)


def _project(x, w, out_dtype, sigmoid):
    m, k = x.shape
    n = w.shape[1]
    return pl.pallas_call(
        functools.partial(_proj_kernel, sigmoid=sigmoid),
        grid=(m // ROW_TILE,),
        in_specs=[pl.BlockSpec((ROW_TILE, k), lambda i: (i, 0)), _const_spec(w.shape)],
        out_specs=pl.BlockSpec((ROW_TILE, n), lambda i: (i, 0)),
        out_shape=jax.ShapeDtypeStruct((m, n), out_dtype),
        compiler_params=_cparams(("parallel",)),
        name="proj_sigmoid" if sigmoid else "proj_plain",
    )(x, w)


def _rope(y, cos, sin_up, sin_dn):
    outs = []
    for c in range(y.shape[1] // LANES):
        yc = y[:, c * LANES:(c + 1) * LANES]
        outs.append(yc * cos + pltpu.roll(yc, LANES - ROT_DIM // 2, 1) * sin_up
                    + pltpu.roll(yc, ROT_DIM // 2, 1) * sin_dn)
    return jnp.concatenate(outs, axis=1)


def _qkv_kernel(x_ref, w_ref, cos_ref, su_ref, sd_ref, *refs, width, q_scale, stacked):
    q_ref, kf_ref, vf_ref, kb_ref, vb_ref = refs[-5:]
    xb = x_ref[...].astype(BF16)
    cos, su, sd = cos_ref[...], su_ref[...], sd_ref[...]
    q = jnp.dot(xb, w_ref[:, 0:width], preferred_element_type=F32)
    q_ref[...] = (_rope(q, cos, su, sd) * q_scale).astype(BF16)
    k = _rope(jnp.dot(xb, w_ref[:, width:2 * width], preferred_element_type=F32), cos, su, sd)
    v = jnp.dot(xb, w_ref[:, 2 * width:3 * width], preferred_element_type=F32)
    if stacked:
        kt = k.T
        kf_ref[0, 0] = kt
        kb_ref[0] = kt.astype(BF16)
        heads = width // LANES
        for h in range(heads):
            vf_ref[0, pl.ds(h, v.shape[0], stride=heads), :] = v[:, h * LANES:(h + 1) * LANES]
    else:
        kf_ref[...] = k
        kb_ref[...] = k.astype(BF16)
        vf_ref[...] = v
    vb_ref[...] = v.astype(BF16)


def _qkv_project(x_all, w_qkv, tables, row_off, rows, q_scale):
    k = x_all.shape[1]
    width = w_qkv.shape[1] // 3
    tm = min(ROW_TILE, rows)
    blk0 = row_off // tm
    row_spec = lambda cols: pl.BlockSpec((tm, cols), lambda i: (i, 0))
    tab_spec = pl.BlockSpec((tm, LANES), lambda i: (i, 0))
    return pl.pallas_call(
        functools.partial(_qkv_kernel, width=width, q_scale=q_scale, stacked=False),
        grid=(rows // tm,),
        in_specs=[pl.BlockSpec((tm, k), lambda i: (blk0 + i, 0)), _const_spec(w_qkv.shape),
                  tab_spec, tab_spec, tab_spec],
        out_specs=[row_spec(width)] * 5,
        out_shape=[jax.ShapeDtypeStruct((rows, width), BF16),
                   jax.ShapeDtypeStruct((rows, width), F32), jax.ShapeDtypeStruct((rows, width), F32),
                   jax.ShapeDtypeStruct((rows, width), BF16), jax.ShapeDtypeStruct((rows, width), BF16)],
        compiler_params=_cparams(("parallel",)),
        name="proj_qkv",
    )(x_all, w_qkv, *tables)


def _qkv_project_stacked(x_all, w_qkv, tables, bt, L, q_scale, layer, depth, prev):
    k = x_all.shape[1]
    width = w_qkv.shape[1] // 3
    rows = bt * L
    tm = min(ROW_TILE, L)
    nl = L // tm
    tab_spec = pl.BlockSpec((tm, LANES), lambda i: (i % nl, 0))
    row_spec = pl.BlockSpec((tm, width), lambda i: (i, 0))
    any_spec = pl.BlockSpec(memory_space=pl.ANY)
    n_prev = 0 if prev is None else 2
    return pl.pallas_call(
        functools.partial(_qkv_kernel, width=width, q_scale=q_scale, stacked=True),
        grid=(rows // tm,),
        in_specs=[pl.BlockSpec((tm, k), lambda i: (i, 0)), _const_spec(w_qkv.shape),
                  tab_spec, tab_spec, tab_spec] + [any_spec] * n_prev,
        out_specs=[row_spec,
                   pl.BlockSpec((1, 1, width, tm), lambda i: (layer, i // nl, 0, i % nl)),
                   pl.BlockSpec((1, tm * (width // LANES), LANES), lambda i: (layer, i, 0)),
                   pl.BlockSpec((1, width, tm), lambda i: (i // nl, 0, i % nl)),
                   row_spec],
        out_shape=[jax.ShapeDtypeStruct((rows, width), BF16),
                   jax.ShapeDtypeStruct((depth, bt, width, L), F32),
                   jax.ShapeDtypeStruct((depth, rows * (width // LANES), LANES), F32),
                   jax.ShapeDtypeStruct((bt, width, L), BF16), jax.ShapeDtypeStruct((rows, width), BF16)],
        input_output_aliases={} if prev is None else {5: 1, 6: 2},
        compiler_params=_cparams(("parallel",)),
        name="proj_qkv_stacked",
    )(x_all, w_qkv, *tables, *(() if prev is None else prev))


def _rope_tables(pos):
    half = ROT_DIM // 2
    inv = ROPE_THETA ** (-jnp.arange(0, ROT_DIM, 2, dtype=F32) / ROT_DIM)
    ang = pos.astype(F32)[:, None] * inv[None, :]
    cos, sin = jnp.cos(ang), jnp.sin(ang)
    n = pos.shape[0]
    ones = jnp.ones((n, 64 - ROT_DIM), F32)
    zeros = jnp.zeros((n, 64 - half), F32)
    cos64 = jnp.concatenate([cos, cos, ones], axis=1)
    up64 = jnp.concatenate([-sin, zeros], axis=1)
    dn64 = jnp.concatenate([jnp.zeros((n, half), F32), sin, jnp.zeros((n, 64 - ROT_DIM), F32)], axis=1)
    return tuple(jnp.concatenate([t, t], axis=1) for t in (cos64, up64, dn64))


def _local_kernel(z_ref, h_ref, b_ref, c_ref, hp_ref, hc_ref, pw_ref, ps_ref, cw_ref, cb_ref, *refs, tl, pos0):
    op_ref, oc_ref, np_ref, nc_ref, zbuf, cbuf = refs[-6:]
    i = pl.program_id(1)
    last = pl.num_programs(1) - 1

    @pl.when(i == 0)
    def _():
        zbuf[0:POOL_PAD, :] = hp_ref[0]
        cbuf[0:CONV_PAD, :] = hc_ref[0]

    @pl.when(i > 0)
    def _():
        zbuf[0:POOL_PAD, :] = zbuf[tl:tl + POOL_PAD, :]
        cbuf[0:CONV_PAD, :] = cbuf[tl:tl + CONV_PAD, :]

    z = z_ref[...]
    zbuf[POOL_PAD:POOL_PAD + tl, :] = z
    cbuf[CONV_PAD:CONV_PAD + tl, :] = c_ref[...] * h_ref[...]

    pos = pos0 + i * tl + lax.broadcasted_iota(jnp.int32, (tl, 1), 0)
    gw = z.shape[1] // len(POOL_WINDOWS)
    outs = []
    for g, w in enumerate(POOL_WINDOWS):
        cols = slice(g * gw, (g + 1) * gw)
        s = z[:, cols]
        for j in range(1, w):
            s = s + zbuf[POOL_PAD - j:POOL_PAD - j + tl, cols]
        cnt = jnp.minimum(w, pos + 1).astype(F32)
        u = s / cnt - z[:, cols]
        outs.append(jnp.dot(u.astype(BF16), pw_ref[g], preferred_element_type=F32))
    op_ref[...] = (jnp.concatenate(outs, axis=1) * ps_ref[...]).astype(op_ref.dtype)

    y = cb_ref[...] + cbuf[CONV_PAD - 2:CONV_PAD - 2 + tl, :] * cw_ref[0:1, :]
    y = y + cbuf[CONV_PAD - 1:CONV_PAD - 1 + tl, :] * cw_ref[1:2, :]
    y = y + cbuf[CONV_PAD:CONV_PAD + tl, :] * cw_ref[2:3, :]
    oc_ref[...] = (b_ref[...] * y).astype(oc_ref.dtype)

    @pl.when(i == last)
    def _():
        np_ref[0] = zbuf[tl + 1:tl + POOL_PAD, :]
        nc_ref[0] = cbuf[tl + CONV_PAD - 2:tl + CONV_PAD, :]


def _alias_args(prev, n_inputs):
    if prev is None:
        return [], {}, ()
    return ([pl.BlockSpec(memory_space=pl.ANY)] * len(prev),
            {n_inputs + k: k for k in range(len(prev))}, tuple(prev))


def _local_mixers(mix, hist_pool, hist_conv, pool_w, pool_scale, conv_w, conv_b, row_off, bt, L, pos0, prev):
    width = pool_scale.shape[-1]
    tl = min(ROW_TILE, L)
    nl = L // tl
    blk0 = row_off // tl
    col_spec = lambda cb: pl.BlockSpec((tl, width), lambda b, i: (blk0 + b * nl + i, cb))
    out_spec = pl.BlockSpec((tl, width), lambda b, i: (blk0 + b * nl + i, 0))
    alias_specs, alias_map, alias_in = _alias_args(prev, 10)
    hp = jnp.pad(hist_pool, ((0, 0), (POOL_PAD - hist_pool.shape[1], 0), (0, 0)))
    hc = jnp.pad(hist_conv, ((0, 0), (CONV_PAD - hist_conv.shape[1], 0), (0, 0)))
    n_hp, n_hc = hist_pool.shape[1], hist_conv.shape[1]
    return pl.pallas_call(
        functools.partial(_local_kernel, tl=tl, pos0=pos0),
        grid=(bt, nl),
        in_specs=[col_spec(0), col_spec(2), col_spec(3), col_spec(4),
                  pl.BlockSpec((1, POOL_PAD, width), lambda b, i: (b, 0, 0)),
                  pl.BlockSpec((1, CONV_PAD, width), lambda b, i: (b, 0, 0)),
                  _const_spec(pool_w.shape), _const_spec((1, width)),
                  _const_spec(conv_w.shape), _const_spec((1, width))] + alias_specs,
        out_specs=[out_spec, out_spec,
                   pl.BlockSpec((1, n_hp, width), lambda b, i: (b, 0, 0)),
                   pl.BlockSpec((1, n_hc, width), lambda b, i: (b, 0, 0))],
        out_shape=[jax.ShapeDtypeStruct((mix.shape[0], width), BF16), jax.ShapeDtypeStruct((mix.shape[0], width), BF16),
                   jax.ShapeDtypeStruct((bt, n_hp, width), F32), jax.ShapeDtypeStruct((bt, n_hc, width), F32)],
        scratch_shapes=[pltpu.VMEM((POOL_PAD + tl, width), F32), pltpu.VMEM((CONV_PAD + tl, width), F32)],
        input_output_aliases=alias_map,
        compiler_params=_cparams(("parallel", "arbitrary")),
        name="local_mixers",
    )(mix, mix, mix, mix, hp, hc, pool_w.astype(BF16), pool_scale.reshape(1, width),
      conv_w, conv_b.reshape(1, width), *alias_in)


def _discretize_kernel(are_ref, aim_ref, ldt_ref, bre_ref, bim_ref, abr_ref, abi_ref, bbr_ref, bbi_ref):
    a_re, a_im = are_ref[...], aim_ref[...]
    dt = jnp.exp(ldt_ref[...])
    mag = jnp.exp(a_re * dt)
    ab_re = mag * jnp.cos(a_im * dt)
    ab_im = mag * jnp.sin(a_im * dt)
    den = a_re * a_re + a_im * a_im
    cr = ((ab_re - 1.0) * a_re + ab_im * a_im) / den
    ci = (ab_im * a_re - (ab_re - 1.0) * a_im) / den
    b_re, b_im = bre_ref[...], bim_ref[...]
    abr_ref[...] = ab_re
    abi_ref[...] = ab_im
    bbr_ref[...] = cr * b_re - ci * b_im
    bbi_ref[...] = cr * b_im + ci * b_re


def _discretize(a_re, a_im, log_dt, b_re, b_im):
    g, p = a_re.shape
    n = b_re.shape[-1]
    col = lambda t: t.reshape(g * p, 1)
    ldt = jnp.broadcast_to(log_dt[:, None], (g, p))
    shapes = [jax.ShapeDtypeStruct((g * p, 1), F32)] * 2 + [jax.ShapeDtypeStruct((g * p, n), F32)] * 2
    return pl.pallas_call(_discretize_kernel, out_shape=shapes, name="ssm_discretize")(
        col(a_re), col(a_im), col(ldt), b_re.reshape(g * p, n), b_im.reshape(g * p, n))


def _scan_tables(are_ref, aim_ref, tab_ref):
    sub = lax.broadcasted_iota(jnp.int32, (SUBLANES, LANES), 0)
    for j in range(tab_ref.shape[0]):
        ar, ai = are_ref[j], aim_ref[j]
        powers = [(ar, ai)]
        for _ in range(SUBLANES - 1):
            pr, pi = powers[-1]
            powers.append((pr * ar - pi * ai, pr * ai + pi * ar))
        for t, d in enumerate(SCAN_SHIFTS):
            dr, di = powers[d - 1]
            tab_ref[j, 2 * t] = jnp.where(sub >= d, dr, 0.0)
            tab_ref[j, 2 * t + 1] = jnp.where(sub >= d, di, 0.0)
        tab_ref[j, 2 * len(SCAN_SHIFTS)] = jnp.concatenate([p[0] for p in powers], axis=0)
        tab_ref[j, 2 * len(SCAN_SHIFTS) + 1] = jnp.concatenate([p[1] for p in powers], axis=0)


def _ssm_kernel(u_ref, hre_ref, him_ref, are_ref, aim_ref, bcat_ref, ccat_ref, d_ref, wg_ref, *refs,
                tl, halves, chunks):
    o_ref, nre_ref, nim_ref, s_ref, cre_ref, cim_ref, tab_ref = refs[-7:]
    i = pl.program_id(1)

    @pl.when(i == 0)
    def _():
        cre_ref[...] = hre_ref[0]
        cim_ref[...] = him_ref[0]

    u = u_ref[...]
    ub = u.astype(BF16)
    kw = ub.shape[1] // halves
    per_half = 2 * chunks
    for h in range(halves):
        bu = jnp.dot(ub[:, h * kw:(h + 1) * kw], bcat_ref[h], preferred_element_type=F32)
        for q in range(per_half):
            s_ref[h * per_half + q] = bu[:, q * LANES:(q + 1) * LANES]

    @pl.when(i == 0)
    def _():
        _scan_tables(are_ref, aim_ref, tab_ref)

    def scan_chunk(j, carry):
        ire = (j // chunks) * per_half + (j % chunks)
        iim = ire + chunks
        cr = jnp.broadcast_to(cre_ref[j], (SUBLANES, LANES))
        ci = jnp.broadcast_to(cim_ref[j], (SUBLANES, LANES))
        steps = [(d, tab_ref[j, 2 * t], tab_ref[j, 2 * t + 1]) for t, d in enumerate(SCAN_SHIFTS)]
        pw_re, pw_im = tab_ref[j, 2 * len(SCAN_SHIFTS)], tab_ref[j, 2 * len(SCAN_SHIFTS) + 1]
        for r in range(tl // SUBLANES):
            rows = pl.ds(SUBLANES * r, SUBLANES)
            xr, xi = s_ref[ire, rows, :], s_ref[iim, rows, :]
            for d, mr, mi in steps:
                sr, si = pltpu.roll(xr, d, 0), pltpu.roll(xi, d, 0)
                xr, xi = xr + mr * sr - mi * si, xi + mr * si + mi * sr
            xr, xi = xr + pw_re * cr - pw_im * ci, xi + pw_re * ci + pw_im * cr
            s_ref[ire, rows, :] = xr
            s_ref[iim, rows, :] = xi
            cr = jnp.broadcast_to(xr[SUBLANES - 1:SUBLANES], (SUBLANES, LANES))
            ci = jnp.broadcast_to(xi[SUBLANES - 1:SUBLANES], (SUBLANES, LANES))
        cre_ref[j] = cr[0:1]
        cim_ref[j] = ci[0:1]
        return carry

    for j in range(halves * chunks):
        scan_chunk(j, 0)

    ys = []
    for h in range(halves):
        st = jnp.concatenate([s_ref[h * per_half + q] for q in range(per_half)], axis=1)
        ys.append(jnp.dot(st.astype(BF16), ccat_ref[h], preferred_element_type=F32))
    y = jnp.concatenate(ys, axis=1) + d_ref[...] * u
    v = 0.5 * y * (1.0 + jnp.tanh(math.sqrt(2.0 / math.pi) * (y + 0.044715 * (y * y * y))))
    gate = jax.nn.sigmoid(jnp.dot(v.astype(BF16), wg_ref[...], preferred_element_type=F32))
    o_ref[...] = (v * gate).astype(o_ref.dtype)

    @pl.when(i == pl.num_programs(1) - 1)
    def _():
        nre_ref[0] = cre_ref[...]
        nim_ref[0] = cim_ref[...]


def _ssm_weights(ab_re, ab_im, bb_re, bb_im, c_re, c_im, halves):
    g, n, p = c_re.shape
    gh = g // halves
    eye = jnp.eye(gh, dtype=F32)
    bcat, ccat = [], []
    for h in range(halves):
        sl = slice(h * gh, (h + 1) * gh)
        dense_b = lambda t: jnp.einsum('gpn,gk->gnkp', t.reshape(g, p, n)[sl], eye).reshape(gh * n, gh * p)
        dense_c = lambda t: jnp.einsum('gnp,gk->gpkn', t[sl], eye).reshape(gh * p, gh * n)
        bcat.append(jnp.concatenate([dense_b(bb_re), dense_b(bb_im)], axis=1))
        ccat.append(jnp.concatenate([dense_c(c_re), -dense_c(c_im)], axis=0))
    nch = g * p // LANES
    return (ab_re.reshape(nch, 1, LANES), ab_im.reshape(nch, 1, LANES),
            jnp.stack(bcat).astype(BF16), jnp.stack(ccat).astype(BF16))


def _ssm_mixer(mix, h_re, h_im, ssm_w, d_skip, w_glu, row_off, bt, L, prev):
    a_re, a_im, bcat, ccat = ssm_w
    alias_specs, alias_map, alias_in = _alias_args(prev, 9)
    halves = bcat.shape[0]
    width = d_skip.shape[-1]
    nch = a_re.shape[0]
    chunks = nch // halves
    g, p = h_re.shape[1], h_re.shape[2]
    tl = min(SSM_TILE, L)
    nl = L // tl
    blk0 = row_off // tl
    state_spec = pl.BlockSpec((1, nch, 1, LANES), lambda b, i: (b, 0, 0, 0))
    o, n_re, n_im = pl.pallas_call(
        functools.partial(_ssm_kernel, tl=tl, halves=halves, chunks=chunks),
        grid=(bt, nl),
        in_specs=[pl.BlockSpec((tl, width), lambda b, i: (blk0 + b * nl + i, 1)),
                  state_spec, state_spec,
                  _const_spec(a_re.shape), _const_spec(a_im.shape),
                  _const_spec(bcat.shape), _const_spec(ccat.shape),
                  _const_spec((1, width)), _const_spec(w_glu.shape)] + alias_specs,
        out_specs=[pl.BlockSpec((tl, width), lambda b, i: (blk0 + b * nl + i, 0)), state_spec, state_spec],
        out_shape=[jax.ShapeDtypeStruct((mix.shape[0], width), BF16),
                   jax.ShapeDtypeStruct((bt, nch, 1, LANES), F32), jax.ShapeDtypeStruct((bt, nch, 1, LANES), F32)],
        scratch_shapes=[pltpu.VMEM((2 * nch, tl, LANES), F32),
                        pltpu.VMEM((nch, 1, LANES), F32), pltpu.VMEM((nch, 1, LANES), F32),
                        pltpu.VMEM((nch, 2 * len(SCAN_SHIFTS) + 2, SUBLANES, LANES), F32)],
        input_output_aliases=alias_map,
        compiler_params=_cparams(("parallel", "arbitrary")),
        name="ssm_mixer",
    )(mix, h_re.astype(F32).reshape(bt, nch, 1, LANES), h_im.astype(F32).reshape(bt, nch, 1, LANES),
      a_re, a_im, bcat, ccat, d_skip.reshape(1, width), w_glu.astype(BF16), *alias_in)
    return o, n_re.reshape(bt, g, p), n_im.reshape(bt, g, p)


def _attn_init(m_ref, l_ref, acc_ref):
    m_ref[...] = jnp.full(m_ref.shape, -jnp.inf, F32)
    l_ref[...] = jnp.zeros(l_ref.shape, F32)
    acc_ref[...] = jnp.zeros(acc_ref.shape, F32)


def _attn_update(q_of, k_of, v_of, visible, heads, m_ref, l_ref, acc_ref):
    low_lanes = lax.broadcasted_iota(jnp.int32, (1, LANES), 1) < (LANES // 2)
    if visible is not None:
        visible = jnp.concatenate([visible, visible], axis=0)
    for h in range(heads):
        qh, kh, vh = q_of(h), k_of(h), v_of(h)
        zero = jnp.zeros_like(qh)
        qm = jnp.concatenate([jnp.where(low_lanes, qh, zero), jnp.where(low_lanes, zero, qh)], axis=0)
        s = jnp.dot(qm, kh, preferred_element_type=F32)
        if visible is not None:
            s = jnp.where(visible, s, -jnp.inf)
        m_old = m_ref[h]
        m_new = jnp.maximum(m_old, jnp.max(s, axis=1, keepdims=True))
        alpha = jnp.exp2(m_old - m_new)
        ps = [jnp.exp2(s[:, j * LANES:(j + 1) * LANES] - m_new) for j in range(s.shape[1] // LANES)]
        lsum = ps[0]
        for pj in ps[1:]:
            lsum = lsum + pj
        l_ref[h] = alpha * l_ref[h] + lsum
        p = jnp.concatenate([pj.astype(BF16) for pj in ps], axis=1)
        acc_ref[h] = alpha * acc_ref[h] + jnp.dot(p, vh, preferred_element_type=F32)
        m_ref[h] = m_new


def _attn_finalize(lam_ref, sw_ref, o_ref, l_ref, acc_ref, heads, lam_init):
    lp = lam_ref[...]
    lam = (jnp.exp(jnp.sum(lp[0:1] * lp[1:2], axis=1, keepdims=True))
           - jnp.exp(jnp.sum(lp[2:3] * lp[3:4], axis=1, keepdims=True)) + lam_init)
    rows = o_ref.shape[0]
    for h in range(heads):
        oh = acc_ref[h] / jnp.sum(l_ref[h], axis=1, keepdims=True)
        o = oh[:rows] - lam * oh[rows:]
        o = o * lax.rsqrt(jnp.mean(o * o, axis=1, keepdims=True) + LN_EPS) * sw_ref[...] * (1.0 - lam_init)
        o_ref[:, h * LANES:(h + 1) * LANES] = o.astype(o_ref.dtype)


def _head_cols(ref):
    lead = (0,) * (len(ref.shape) - 2)
    return lambda h: ref[(*lead, slice(None), slice(h * LANES, (h + 1) * LANES))].astype(BF16)


def _head_rows(ref):
    lead = (0,) * (len(ref.shape) - 2)
    return lambda h: ref[(*lead, slice(h * LANES, (h + 1) * LANES), slice(None))].astype(BF16)


def _attn_kernel(qi_ref, ki_ref, fl_ref, q_ref, k_ref, v_ref, lam_ref, sw_ref, o_ref, m_ref, l_ref, acc_ref,
                 *, tq, tk, heads, q_pos0, lk, lam_init):
    step = pl.program_id(1)
    qi, ki, fl = qi_ref[step], ki_ref[step], fl_ref[step]

    @pl.when(ki == 0)
    def _():
        _attn_init(m_ref, l_ref, acc_ref)

    def accumulate(masked, keys):
        visible = None
        if masked:
            q_pos = q_pos0 + qi * tq + lax.broadcasted_iota(jnp.int32, (tq, 1), 0)
            k_pos = ki * tk + lax.broadcasted_iota(jnp.int32, (1, keys), 1)
            visible = (k_pos < (q_pos // CHUNK + 1) * CHUNK) & (k_pos < lk)
        k_of = lambda h: k_ref[0, h * LANES:(h + 1) * LANES, 0:keys]
        v_of = lambda h: v_ref[0, 0:keys, h * LANES:(h + 1) * LANES]
        _attn_update(_head_cols(q_ref), k_of, v_of, visible, heads, m_ref, l_ref, acc_ref)

    @pl.when((fl & 1) == 0)
    def _():
        accumulate(False, tk)

    @pl.when(((fl & 1) != 0) & ((fl & 4) == 0))
    def _():
        accumulate(True, tk)

    @pl.when((fl & 4) != 0)
    def _():
        accumulate(True, tk // 2)

    @pl.when((fl & 2) != 0)
    def _():
        _attn_finalize(lam_ref, sw_ref, o_ref, l_ref, acc_ref, heads, lam_init)


def _decode_attn_kernel(q_ref, ck_ref, cv_ref, nk_ref, nv_ref, lam_ref, sw_ref, joint_ref, o_ref, m_ref, l_ref,
                        acc_ref, *, tq, heads, past, n_cache, lam_init):
    j = pl.program_id(1)

    @pl.when(j == 0)
    def _():
        _attn_init(m_ref, l_ref, acc_ref)

    @pl.when(j < n_cache)
    def _():
        n_keys = cv_ref.shape[2] // heads
        cached_v = lambda h: cv_ref[0, 0, pl.ds(h, n_keys, stride=heads), :].astype(BF16)
        _attn_update(_head_cols(q_ref), _head_rows(ck_ref), cached_v, None, heads, m_ref, l_ref, acc_ref)

    @pl.when(j == n_cache)
    def _():
        nk = nk_ref.shape[2]
        t = lax.broadcasted_iota(jnp.int32, (tq, 1), 0)
        i = lax.broadcasted_iota(jnp.int32, (1, nk), 1)
        visible = (past + i < ((past + t) // CHUNK + 1) * CHUNK) & (i < tq)
        _attn_update(_head_cols(q_ref), _head_rows(nk_ref), _head_cols(nv_ref), visible, heads, m_ref, l_ref, acc_ref)
        _attn_finalize(lam_ref, sw_ref, o_ref, l_ref, acc_ref, heads, lam_init)


def _attn_schedule(L, lk, tq, tk, q_pos0):
    qi, ki, fl = [], [], []
    for a in range(L // tq):
        first_end = ((q_pos0 + a * tq) // CHUNK + 1) * CHUNK
        last_end = min(((q_pos0 + a * tq + tq - 1) // CHUNK + 1) * CHUNK, lk)
        nk = -(-last_end // tk)
        for b in range(nk):
            full = (b + 1) * tk <= min(first_end, lk)
            half = not full and last_end <= b * tk + tk // 2
            qi.append(a); ki.append(b); fl.append((0 if full else 1) | (2 if b == nk - 1 else 0) | (4 if half else 0))
    return tuple(jnp.asarray(np.asarray(t, np.int32)) for t in (qi, ki, fl))


def _attention(q, kt_all, v_all, lam_p, subln_w, bt, L, lk, q_pos0, lam_init, total_rows):
    width = q.shape[-1]
    heads = width // LANES
    tq = min(ATTN_Q_TILE, L)
    tk = ATTN_TILE
    lk_pad = kt_all.shape[2]
    nq = L // tq
    qi, ki, fl = _attn_schedule(L, lk, tq, tk, q_pos0)
    grid_spec = pltpu.PrefetchScalarGridSpec(
        num_scalar_prefetch=3,
        grid=(bt, int(qi.shape[0])),
        in_specs=[pl.BlockSpec((1, tq, width), lambda b, s, qi, ki, fl: (b, qi[s], 0)),
                  pl.BlockSpec((1, width, tk), lambda b, s, qi, ki, fl: (b, 0, ki[s])),
                  pl.BlockSpec((1, tk, width), lambda b, s, qi, ki, fl: (b, ki[s], 0)),
                  pl.BlockSpec(lam_p.shape, lambda b, s, qi, ki, fl: (0, 0)),
                  pl.BlockSpec((1, LANES), lambda b, s, qi, ki, fl: (0, 0))],
        out_specs=pl.BlockSpec((tq, width), lambda b, s, qi, ki, fl: (b * nq + qi[s], 0)),
        scratch_shapes=[pltpu.VMEM((heads, 2 * tq, LANES), F32)] * 3)
    assert lk_pad % tk == 0 and L % tq == 0
    return pl.pallas_call(
        functools.partial(_attn_kernel, tq=tq, tk=tk, heads=heads, q_pos0=q_pos0, lk=lk, lam_init=lam_init),
        grid_spec=grid_spec,
        out_shape=jax.ShapeDtypeStruct((total_rows, width), BF16),
        compiler_params=_cparams(("parallel", "arbitrary")),
        name="diff_attention",
    )(qi, ki, fl, q.reshape(bt, L, width), kt_all, v_all, lam_p, subln_w.reshape(1, LANES))


def _decode_attention(q, cache_kt, cache_v, layer, new_k, new_v, lam_p, subln_w, bt, L, past, lam_init, row_off, joint):
    width = q.shape[-1]
    heads = width // LANES
    tk = min(DECODE_TILE, past)
    assert past % tk == 0 and past >= tk and L <= LANES and past % CHUNK == 0 and row_off % L == 0
    n_cache = past // tk
    new_kt = jnp.pad(jnp.swapaxes(new_k.reshape(bt, L, width), 1, 2), ((0, 0), (0, 0), (0, LANES - L)))
    new_v = jnp.pad(new_v.reshape(bt, L, width), ((0, 0), (0, LANES - L), (0, 0)))
    row_spec = pl.BlockSpec((1, L, width), lambda b, j: (b, 0, 0))
    scratch = pltpu.VMEM((heads, 2 * L, LANES), F32)
    return pl.pallas_call(
        functools.partial(_decode_attn_kernel, tq=L, heads=heads, past=past, n_cache=n_cache, lam_init=lam_init),
        grid=(bt, n_cache + 1),
        in_specs=[row_spec,
                  pl.BlockSpec((1, 1, width, tk), lambda b, j: (layer, b, 0, jnp.minimum(j, n_cache - 1))),
                  pl.BlockSpec((1, 1, tk * heads, LANES), lambda b, j: (layer, b, jnp.minimum(j, n_cache - 1), 0)),
                  pl.BlockSpec((1, width, LANES), lambda b, j: (b, 0, 0)),
                  pl.BlockSpec((1, LANES, width), lambda b, j: (b, 0, 0)),
                  pl.BlockSpec(lam_p.shape, lambda b, j: (0, 0)), pl.BlockSpec((1, LANES), lambda b, j: (0, 0)),
                  pl.BlockSpec(memory_space=pl.ANY)],
        out_specs=pl.BlockSpec((L, width), lambda b, j: (row_off // L + b, 0)),
        out_shape=jax.ShapeDtypeStruct(joint.shape, BF16),
        scratch_shapes=[scratch, scratch, scratch],
        input_output_aliases={7: 0},
        compiler_params=_cparams(("parallel", "arbitrary")),
        name="decode_attention",
    )(q.reshape(bt, L, width), cache_kt, cache_v, new_kt, new_v, lam_p, subln_w.reshape(1, LANES), joint)


def _merge_kernel(x_ref, g_ref, op_ref, os_ref, oc_ref, oa_ref, wb_ref, wo_ref, lg_ref, lb_ref, o_ref,
                  *, d_model, alpha, offs):
    merged = None
    for b, (o_b, (lo, hi)) in enumerate(zip((op_ref, os_ref, oc_ref, oa_ref), offs)):
        t = jnp.dot(o_b[...], wb_ref[lo:hi, :], preferred_element_type=F32)
        t = t * g_ref[:, b * d_model:(b + 1) * d_model].astype(F32)
        merged = t if merged is None else merged + t
    y = alpha * x_ref[...] + jnp.dot(merged.astype(BF16), wo_ref[...], preferred_element_type=F32)
    o_ref[...] = _layer_norm(y, lg_ref[...], lb_ref[...])


def _merge(x, gates, o_pool, o_ssm, o_conv, o_attn, w_branch, w_out, ln_g, ln_b, alpha):
    m, d = x.shape
    widths = [o_pool.shape[1], o_ssm.shape[1], o_conv.shape[1], o_attn.shape[1]]
    ends = np.cumsum(widths)
    offs = tuple((int(e - w), int(e)) for e, w in zip(ends, widths))
    row = lambda cols: pl.BlockSpec((ROW_TILE, cols), lambda i: (i, 0))
    return pl.pallas_call(
        functools.partial(_merge_kernel, d_model=d, alpha=alpha, offs=offs),
        grid=(m // ROW_TILE,),
        in_specs=[row(d), row(gates.shape[1])] + [row(w) for w in widths]
                 + [_const_spec(w_branch.shape), _const_spec(w_out.shape), _const_spec((1, d)), _const_spec((1, d))],
        out_specs=row(d),
        out_shape=jax.ShapeDtypeStruct((m, d), F32),
        compiler_params=_cparams(("parallel",)),
        name="merge_out_ln",
    )(x, gates, o_pool, o_ssm, o_conv, o_attn, w_branch, w_out, ln_g.reshape(1, d), ln_b.reshape(1, d))


def _ffn_kernel(x_ref, wu_ref, wd_ref, lg_ref, lb_ref, o_ref, *, alpha, chunk):
    x = x_ref[...]
    xb = x.astype(BF16)
    acc = alpha * x
    for c in range(wu_ref.shape[1] // chunk):
        hid = jnp.dot(xb, wu_ref[:, c * chunk:(c + 1) * chunk], preferred_element_type=F32)
        hid = jnp.square(jnp.maximum(hid, 0.0)).astype(BF16)
        acc = acc + jnp.dot(hid, wd_ref[c * chunk:(c + 1) * chunk, :], preferred_element_type=F32)
    o_ref[...] = _layer_norm(acc, lg_ref[...], lb_ref[...])


def _ffn(x, w_up, w_down, ln_g, ln_b, alpha, row_off=0, rows=None):
    d = x.shape[1]
    rows = x.shape[0] if rows is None else rows
    blk0 = row_off // ROW_TILE
    return pl.pallas_call(
        functools.partial(_ffn_kernel, alpha=alpha, chunk=1024),
        grid=(rows // ROW_TILE,),
        in_specs=[pl.BlockSpec((ROW_TILE, d), lambda i: (blk0 + i, 0)),
                  _const_spec(w_up.shape), _const_spec(w_down.shape), _const_spec((1, d)), _const_spec((1, d))],
        out_specs=pl.BlockSpec((ROW_TILE, d), lambda i: (i, 0)),
        out_shape=jax.ShapeDtypeStruct((rows, d), F32),
        compiler_params=_cparams(("parallel",)),
        name="ffn_ln",
    )(x, w_up, w_down, ln_g.reshape(1, d), ln_b.reshape(1, d))


def kernel(x_prompt, x_sample, cache_k, cache_v, state_ssm_re, state_ssm_im, state_conv, state_pool, w_in, pool_w, pool_scale, ssm_a_re, ssm_a_im, ssm_log_dt, ssm_b_re, ssm_b_im, ssm_c_re, ssm_c_im, ssm_d, ssm_w_glu, conv_w, conv_b, lambda_q1, lambda_k1, lambda_q2, lambda_k2, subln_w, w_branch, w_out, ln1_g, ln1_b, w_up, w_down, ln2_g, ln2_b):
    depth = w_in.shape[0]
    bp, lp, d = x_prompt.shape
    bs, ls, _ = x_sample.shape
    past = cache_k.shape[2]
    heads, qk_dim = cache_k.shape[3], cache_k.shape[5]
    width_qk = heads * 2 * qk_dim
    pool_width, ssm_width, conv_width = pool_scale.shape[1], ssm_d.shape[1], conv_w.shape[2]
    n_mix = pool_width + ssm_width + 3 * conv_width
    alpha = float((2 * depth) ** 0.25)
    paths = ((bp, lp, 0, 0), (bs, ls, past, bp * lp))

    x = jnp.concatenate([x_prompt.reshape(bp * lp, d), x_sample.reshape(bs * ls, d)], axis=0)
    tables = (_rope_tables(jnp.arange(lp, dtype=jnp.int32)),
              _rope_tables(jnp.tile(past + jnp.arange(ls, dtype=jnp.int32), bs)))

    cache_kt = jnp.transpose(cache_k, (0, 1, 3, 4, 5, 2)).reshape(depth, bs, width_qk, past)
    cache_vr = cache_v.reshape(depth, bs, past * heads, 2 * qk_dim)

    outs = [[[] for _ in range(6)] for _ in paths]
    kv_stack = None
    for l in range(depth):
        wl = w_in[l].astype(BF16)
        mix = _project(x, wl[:, :n_mix], F32, sigmoid=False)
        gates = _project(x, wl[:, n_mix + 3 * width_qk:], BF16, sigmoid=True)
        ab_re, ab_im, bb_re, bb_im = _discretize(ssm_a_re[l], ssm_a_im[l], ssm_log_dt[l], ssm_b_re[l], ssm_b_im[l])
        ssm_w = _ssm_weights(ab_re, ab_im, bb_re, bb_im, ssm_c_re[l], ssm_c_im[l], halves=2)
        lam_p = jnp.stack([lambda_q1[l], lambda_k1[l], lambda_q2[l], lambda_k2[l]]).astype(F32)
        lam_init = 0.8 - 0.6 * math.exp(-0.3 * l)

        w_qkv = wl[:, n_mix:n_mix + 3 * width_qk]
        q_scale = float(qk_dim) ** -0.5 * math.log2(math.e)
        o_pool = o_conv = o_ssm = o_attn = None
        for pi, (bt, L, pos0, row_off) in enumerate(paths):
            if pi == 0:
                q, k_stack, v_stack, kbt, vb = _qkv_project_stacked(x, w_qkv, tables[pi], bt, L, q_scale, l, depth,
                                                                    kv_stack)
                kv_stack = (k_stack, v_stack)
                hist_pool = jnp.zeros((bt, state_pool.shape[2], pool_width), F32)
                hist_conv = jnp.zeros((bt, state_conv.shape[2], conv_width), F32)
                h_re = h_im = jnp.zeros((bt,) + state_ssm_re.shape[2:], F32)
                o_attn = _attention(q, kbt, vb.reshape(bt, L, width_qk), lam_p, subln_w[l], bt, L, L, pos0, lam_init,
                                    x.shape[0])
                kv_new = ()
            else:
                q, kf, vf, kb, vb = _qkv_project(x, w_qkv, tables[pi], row_off, bt * L, q_scale)
                hist_pool, hist_conv, h_re, h_im = state_pool[l], state_conv[l], state_ssm_re[l], state_ssm_im[l]
                o_attn = _decode_attention(q, cache_kt, cache_vr, l, kb, vb, lam_p, subln_w[l], bt, L, past, lam_init,
                                           row_off, o_attn)
                kv_new = (kf.reshape(bt, L, heads, 2, qk_dim), vf.reshape(bt, L, heads, 2 * qk_dim))
            o_pool, o_conv, new_pool, new_conv = _local_mixers(
                mix, hist_pool, hist_conv, pool_w[l], pool_scale[l], conv_w[l], conv_b[l], row_off, bt, L, pos0,
                None if pi == 0 else (o_pool, o_conv))
            o_ssm, new_re, new_im = _ssm_mixer(mix, h_re, h_im, ssm_w, ssm_d[l], ssm_w_glu[l], row_off, bt, L,
                                               None if pi == 0 else (o_ssm,))
            for slot, val in zip(outs[pi], (new_re, new_im, new_conv, new_pool) + kv_new):
                slot.append(val)

        x = _merge(x, gates, o_pool, o_ssm, o_conv, o_attn, w_branch[l].astype(BF16), w_out[l].astype(BF16),
                   ln1_g[l], ln1_b[l], alpha)
        ffn_w = (w_up[l].astype(BF16), w_down[l].astype(BF16), ln2_g[l], ln2_b[l], alpha)
        if l + 1 < depth:
            x = _ffn(x, *ffn_w)

    y_prompt, y_sample = (_ffn(x, *ffn_w, row_off, bt * L).reshape(bt, L, d) for bt, L, _, row_off in paths)
    k_stack, v_stack = kv_stack
    k_prompt = jnp.transpose(k_stack.reshape(depth, bp, heads, 2, qk_dim, lp), (0, 1, 5, 2, 3, 4))
    v_prompt = v_stack.reshape(depth, bp, lp, heads, 2 * qk_dim)
    (p_re, p_im, p_conv, p_pool), (s_re, s_im, s_conv, s_pool, s_k, s_v) = (
        [jnp.stack(slot) for slot in path_outs if slot] for path_outs in outs)
    return (y_prompt, y_sample, k_prompt, v_prompt, p_re, p_im, p_conv, p_pool,
            s_k, s_v, s_re, s_im, s_conv, s_pool)
```

```python
import functools
import math

import numpy as np
import jax
import jax.numpy as jnp
from jax import lax
from jax.experimental import pallas as pl
from jax.experimental.pallas import tpu as pltpu

F32 = jnp.float32
BF16 = jnp.bfloat16

LANES = 128
SUBLANES = 8
SCAN_SHIFTS = (1, 2, 4)
CHUNK = 64
POOL_WINDOWS = (2, 4, 8, 16)
POOL_PAD = 16
CONV_PAD = 8
ROT_DIM = 16
ROPE_THETA = 500000.0
LN_EPS = 1e-5
VMEM_LIMIT = 56 * 1024 * 1024

ROW_TILE = 512
ATTN_TILE = 1024
ATTN_Q_TILE = 512
DECODE_TILE = 1024
SSM_TILE = 256


def _cparams(sem):
    return pltpu.CompilerParams(dimension_semantics=sem, vmem_limit_bytes=VMEM_LIMIT)


def _const_spec(shape):
    zeros = (0,) * len(shape)
    return pl.BlockSpec(shape, lambda *_: zeros, pipeline_mode=pl.Buffered(1))


def _layer_norm(y, g, b):
    mu = jnp.mean(y, axis=-1, keepdims=True)
    d = y - mu
    var = jnp.mean(d * d, axis=-1, keepdims=True)
    return d * lax.rsqrt(var + LN_EPS) * g + b


def _proj_kernel(x_ref, w_ref, o_ref, *, sigmoid):
    y = jnp.dot(x_ref[...].astype(BF16), w_ref[...], preferred_element_type=F32)
    if sigmoid:
        y = 0.5 * jnp.tanh(0.5 * y) + 0.5
    o_ref[...] = y.astype(o_ref.dtype)


def _project(x, w, out_dtype, sigmoid):
    m, k = x.shape
    n = w.shape[1]
    return pl.pallas_call(
        functools.partial(_proj_kernel, sigmoid=sigmoid),
        grid=(m // ROW_TILE,),
        in_specs=[pl.BlockSpec((ROW_TILE, k), lambda i: (i, 0)), _const_spec(w.shape)],
        out_specs=pl.BlockSpec((ROW_TILE, n), lambda i: (i, 0)),
        out_shape=jax.ShapeDtypeStruct((m, n), out_dtype),
        compiler_params=_cparams(("parallel",)),
        name="proj_sigmoid" if sigmoid else "proj_plain",
    )(x, w)


def _rope(y, cos, sin_up, sin_dn):
    outs = []
    for c in range(y.shape[1] // LANES):
        yc = y[:, c * LANES:(c + 1) * LANES]
        outs.append(yc * cos + pltpu.roll(yc, LANES - ROT_DIM // 2, 1) * sin_up
                    + pltpu.roll(yc, ROT_DIM // 2, 1) * sin_dn)
    return jnp.concatenate(outs, axis=1)


def _qkv_kernel(x_ref, w_ref, cos_ref, su_ref, sd_ref, *refs, width, q_scale, stacked):
    q_ref, kf_ref, vf_ref, kb_ref, vb_ref = refs[-5:]
    xb = x_ref[...].astype(BF16)
    cos, su, sd = cos_ref[...], su_ref[...], sd_ref[...]
    q = jnp.dot(xb, w_ref[:, 0:width], preferred_element_type=F32)
    q_ref[...] = (_rope(q, cos, su, sd) * q_scale).astype(BF16)
    k = _rope(jnp.dot(xb, w_ref[:, width:2 * width], preferred_element_type=F32), cos, su, sd)
    v = jnp.dot(xb, w_ref[:, 2 * width:3 * width], preferred_element_type=F32)
    if stacked:
        kt = k.T
        kf_ref[0, 0] = kt
        kb_ref[0] = kt.astype(BF16)
        heads = width // LANES
        for h in range(heads):
            vf_ref[0, pl.ds(h, v.shape[0], stride=heads), :] = v[:, h * LANES:(h + 1) * LANES]
    else:
        kf_ref[...] = k
        kb_ref[...] = k.astype(BF16)
        vf_ref[...] = v
    vb_ref[...] = v.astype(BF16)


def _qkv_project(x_all, w_qkv, tables, row_off, rows, q_scale):
    k = x_all.shape[1]
    width = w_qkv.shape[1] // 3
    tm = min(ROW_TILE, rows)
    blk0 = row_off // tm
    row_spec = lambda cols: pl.BlockSpec((tm, cols), lambda i: (i, 0))
    tab_spec = pl.BlockSpec((tm, LANES), lambda i: (i, 0))
    return pl.pallas_call(
        functools.partial(_qkv_kernel, width=width, q_scale=q_scale, stacked=False),
        grid=(rows // tm,),
        in_specs=[pl.BlockSpec((tm, k), lambda i: (blk0 + i, 0)), _const_spec(w_qkv.shape),
                  tab_spec, tab_spec, tab_spec],
        out_specs=[row_spec(width)] * 5,
        out_shape=[jax.ShapeDtypeStruct((rows, width), BF16),
                   jax.ShapeDtypeStruct((rows, width), F32), jax.ShapeDtypeStruct((rows, width), F32),
                   jax.ShapeDtypeStruct((rows, width), BF16), jax.ShapeDtypeStruct((rows, width), BF16)],
        compiler_params=_cparams(("parallel",)),
        name="proj_qkv",
    )(x_all, w_qkv, *tables)


def _qkv_project_stacked(x_all, w_qkv, tables, bt, L, q_scale, layer, depth, prev):
    k = x_all.shape[1]
    width = w_qkv.shape[1] // 3
    rows = bt * L
    tm = min(ROW_TILE, L)
    nl = L // tm
    tab_spec = pl.BlockSpec((tm, LANES), lambda i: (i % nl, 0))
    row_spec = pl.BlockSpec((tm, width), lambda i: (i, 0))
    any_spec = pl.BlockSpec(memory_space=pl.ANY)
    n_prev = 0 if prev is None else 2
    return pl.pallas_call(
        functools.partial(_qkv_kernel, width=width, q_scale=q_scale, stacked=True),
        grid=(rows // tm,),
        in_specs=[pl.BlockSpec((tm, k), lambda i: (i, 0)), _const_spec(w_qkv.shape),
                  tab_spec, tab_spec, tab_spec] + [any_spec] * n_prev,
        out_specs=[row_spec,
                   pl.BlockSpec((1, 1, width, tm), lambda i: (layer, i // nl, 0, i % nl)),
                   pl.BlockSpec((1, tm * (width // LANES), LANES), lambda i: (layer, i, 0)),
                   pl.BlockSpec((1, width, tm), lambda i: (i // nl, 0, i % nl)),
                   row_spec],
        out_shape=[jax.ShapeDtypeStruct((rows, width), BF16),
                   jax.ShapeDtypeStruct((depth, bt, width, L), F32),
                   jax.ShapeDtypeStruct((depth, rows * (width // LANES), LANES), F32),
                   jax.ShapeDtypeStruct((bt, width, L), BF16), jax.ShapeDtypeStruct((rows, width), BF16)],
        input_output_aliases={} if prev is None else {5: 1, 6: 2},
        compiler_params=_cparams(("parallel",)),
        name="proj_qkv_stacked",
    )(x_all, w_qkv, *tables, *(() if prev is None else prev))


def _rope_tables(pos):
    half = ROT_DIM // 2
    inv = ROPE_THETA ** (-jnp.arange(0, ROT_DIM, 2, dtype=F32) / ROT_DIM)
    ang = pos.astype(F32)[:, None] * inv[None, :]
    cos, sin = jnp.cos(ang), jnp.sin(ang)
    n = pos.shape[0]
    ones = jnp.ones((n, 64 - ROT_DIM), F32)
    zeros = jnp.zeros((n, 64 - half), F32)
    cos64 = jnp.concatenate([cos, cos, ones], axis=1)
    up64 = jnp.concatenate([-sin, zeros], axis=1)
    dn64 = jnp.concatenate([jnp.zeros((n, half), F32), sin, jnp.zeros((n, 64 - ROT_DIM), F32)], axis=1)
    return tuple(jnp.concatenate([t, t], axis=1) for t in (cos64, up64, dn64))


def _local_kernel(z_ref, h_ref, b_ref, c_ref, hp_ref, hc_ref, pw_ref, ps_ref, cw_ref, cb_ref, *refs, tl, pos0):
    op_ref, oc_ref, np_ref, nc_ref, zbuf, cbuf = refs[-6:]
    i = pl.program_id(1)
    last = pl.num_programs(1) - 1

    @pl.when(i == 0)
    def _():
        zbuf[0:POOL_PAD, :] = hp_ref[0]
        cbuf[0:CONV_PAD, :] = hc_ref[0]

    @pl.when(i > 0)
    def _():
        zbuf[0:POOL_PAD, :] = zbuf[tl:tl + POOL_PAD, :]
        cbuf[0:CONV_PAD, :] = cbuf[tl:tl + CONV_PAD, :]

    z = z_ref[...]
    zbuf[POOL_PAD:POOL_PAD + tl, :] = z
    cbuf[CONV_PAD:CONV_PAD + tl, :] = c_ref[...] * h_ref[...]

    pos = pos0 + i * tl + lax.broadcasted_iota(jnp.int32, (tl, 1), 0)
    gw = z.shape[1] // len(POOL_WINDOWS)
    outs = []
    for g, w in enumerate(POOL_WINDOWS):
        cols = slice(g * gw, (g + 1) * gw)
        s = z[:, cols]
        for j in range(1, w):
            s = s + zbuf[POOL_PAD - j:POOL_PAD - j + tl, cols]
        cnt = jnp.minimum(w, pos + 1).astype(F32)
        u = s / cnt - z[:, cols]
        outs.append(jnp.dot(u.astype(BF16), pw_ref[g], preferred_element_type=F32))
    op_ref[...] = (jnp.concatenate(outs, axis=1) * ps_ref[...]).astype(op_ref.dtype)

    y = cb_ref[...] + cbuf[CONV_PAD - 2:CONV_PAD - 2 + tl, :] * cw_ref[0:1, :]
    y = y + cbuf[CONV_PAD - 1:CONV_PAD - 1 + tl, :] * cw_ref[1:2, :]
    y = y + cbuf[CONV_PAD:CONV_PAD + tl, :] * cw_ref[2:3, :]
    oc_ref[...] = (b_ref[...] * y).astype(oc_ref.dtype)

    @pl.when(i == last)
    def _():
        np_ref[0] = zbuf[tl + 1:tl + POOL_PAD, :]
        nc_ref[0] = cbuf[tl + CONV_PAD - 2:tl + CONV_PAD, :]


def _alias_args(prev, n_inputs):
    if prev is None:
        return [], {}, ()
    return ([pl.BlockSpec(memory_space=pl.ANY)] * len(prev),
            {n_inputs + k: k for k in range(len(prev))}, tuple(prev))


def _local_mixers(mix, hist_pool, hist_conv, pool_w, pool_scale, conv_w, conv_b, row_off, bt, L, pos0, prev):
    width = pool_scale.shape[-1]
    tl = min(ROW_TILE, L)
    nl = L // tl
    blk0 = row_off // tl
    col_spec = lambda cb: pl.BlockSpec((tl, width), lambda b, i: (blk0 + b * nl + i, cb))
    out_spec = pl.BlockSpec((tl, width), lambda b, i: (blk0 + b * nl + i, 0))
    alias_specs, alias_map, alias_in = _alias_args(prev, 10)
    hp = jnp.pad(hist_pool, ((0, 0), (POOL_PAD - hist_pool.shape[1], 0), (0, 0)))
    hc = jnp.pad(hist_conv, ((0, 0), (CONV_PAD - hist_conv.shape[1], 0), (0, 0)))
    n_hp, n_hc = hist_pool.shape[1], hist_conv.shape[1]
    return pl.pallas_call(
        functools.partial(_local_kernel, tl=tl, pos0=pos0),
        grid=(bt, nl),
        in_specs=[col_spec(0), col_spec(2), col_spec(3), col_spec(4),
                  pl.BlockSpec((1, POOL_PAD, width), lambda b, i: (b, 0, 0)),
                  pl.BlockSpec((1, CONV_PAD, width), lambda b, i: (b, 0, 0)),
                  _const_spec(pool_w.shape), _const_spec((1, width)),
                  _const_spec(conv_w.shape), _const_spec((1, width))] + alias_specs,
        out_specs=[out_spec, out_spec,
                   pl.BlockSpec((1, n_hp, width), lambda b, i: (b, 0, 0)),
                   pl.BlockSpec((1, n_hc, width), lambda b, i: (b, 0, 0))],
        out_shape=[jax.ShapeDtypeStruct((mix.shape[0], width), BF16), jax.ShapeDtypeStruct((mix.shape[0], width), BF16),
                   jax.ShapeDtypeStruct((bt, n_hp, width), F32), jax.ShapeDtypeStruct((bt, n_hc, width), F32)],
        scratch_shapes=[pltpu.VMEM((POOL_PAD + tl, width), F32), pltpu.VMEM((CONV_PAD + tl, width), F32)],
        input_output_aliases=alias_map,
        compiler_params=_cparams(("parallel", "arbitrary")),
        name="local_mixers",
    )(mix, mix, mix, mix, hp, hc, pool_w.astype(BF16), pool_scale.reshape(1, width),
      conv_w, conv_b.reshape(1, width), *alias_in)


def _discretize_kernel(are_ref, aim_ref, ldt_ref, bre_ref, bim_ref, abr_ref, abi_ref, bbr_ref, bbi_ref):
    a_re, a_im = are_ref[...], aim_ref[...]
    dt = jnp.exp(ldt_ref[...])
    mag = jnp.exp(a_re * dt)
    ab_re = mag * jnp.cos(a_im * dt)
    ab_im = mag * jnp.sin(a_im * dt)
    den = a_re * a_re + a_im * a_im
    cr = ((ab_re - 1.0) * a_re + ab_im * a_im) / den
    ci = (ab_im * a_re - (ab_re - 1.0) * a_im) / den
    b_re, b_im = bre_ref[...], bim_ref[...]
    abr_ref[...] = ab_re
    abi_ref[...] = ab_im
    bbr_ref[...] = cr * b_re - ci * b_im
    bbi_ref[...] = cr * b_im + ci * b_re


def _discretize(a_re, a_im, log_dt, b_re, b_im):
    g, p = a_re.shape
    n = b_re.shape[-1]
    col = lambda t: t.reshape(g * p, 1)
    ldt = jnp.broadcast_to(log_dt[:, None], (g, p))
    shapes = [jax.ShapeDtypeStruct((g * p, 1), F32)] * 2 + [jax.ShapeDtypeStruct((g * p, n), F32)] * 2
    return pl.pallas_call(_discretize_kernel, out_shape=shapes, name="ssm_discretize")(
        col(a_re), col(a_im), col(ldt), b_re.reshape(g * p, n), b_im.reshape(g * p, n))


def _scan_tables(are_ref, aim_ref, tab_ref):
    sub = lax.broadcasted_iota(jnp.int32, (SUBLANES, LANES), 0)
    for j in range(tab_ref.shape[0]):
        ar, ai = are_ref[j], aim_ref[j]
        powers = [(ar, ai)]
        for _ in range(SUBLANES - 1):
            pr, pi = powers[-1]
            powers.append((pr * ar - pi * ai, pr * ai + pi * ar))
        for t, d in enumerate(SCAN_SHIFTS):
            dr, di = powers[d - 1]
            tab_ref[j, 2 * t] = jnp.where(sub >= d, dr, 0.0)
            tab_ref[j, 2 * t + 1] = jnp.where(sub >= d, di, 0.0)
        tab_ref[j, 2 * len(SCAN_SHIFTS)] = jnp.concatenate([p[0] for p in powers], axis=0)
        tab_ref[j, 2 * len(SCAN_SHIFTS) + 1] = jnp.concatenate([p[1] for p in powers], axis=0)


def _ssm_kernel(u_ref, hre_ref, him_ref, are_ref, aim_ref, bcat_ref, ccat_ref, d_ref, wg_ref, *refs,
                tl, halves, chunks):
    o_ref, nre_ref, nim_ref, s_ref, cre_ref, cim_ref, tab_ref = refs[-7:]
    i = pl.program_id(1)

    @pl.when(i == 0)
    def _():
        cre_ref[...] = hre_ref[0]
        cim_ref[...] = him_ref[0]

    u = u_ref[...]
    ub = u.astype(BF16)
    kw = ub.shape[1] // halves
    per_half = 2 * chunks
    for h in range(halves):
        bu = jnp.dot(ub[:, h * kw:(h + 1) * kw], bcat_ref[h], preferred_element_type=F32)
        for q in range(per_half):
            s_ref[h * per_half + q] = bu[:, q * LANES:(q + 1) * LANES]

    @pl.when(i == 0)
    def _():
        _scan_tables(are_ref, aim_ref, tab_ref)

    def scan_chunk(j, carry):
        ire = (j // chunks) * per_half + (j % chunks)
        iim = ire + chunks
        cr = jnp.broadcast_to(cre_ref[j], (SUBLANES, LANES))
        ci = jnp.broadcast_to(cim_ref[j], (SUBLANES, LANES))
        steps = [(d, tab_ref[j, 2 * t], tab_ref[j, 2 * t + 1]) for t, d in enumerate(SCAN_SHIFTS)]
        pw_re, pw_im = tab_ref[j, 2 * len(SCAN_SHIFTS)], tab_ref[j, 2 * len(SCAN_SHIFTS) + 1]
        for r in range(tl // SUBLANES):
            rows = pl.ds(SUBLANES * r, SUBLANES)
            xr, xi = s_ref[ire, rows, :], s_ref[iim, rows, :]
            for d, mr, mi in steps:
                sr, si = pltpu.roll(xr, d, 0), pltpu.roll(xi, d, 0)
                xr, xi = xr + mr * sr - mi * si, xi + mr * si + mi * sr
            xr, xi = xr + pw_re * cr - pw_im * ci, xi + pw_re * ci + pw_im * cr
            s_ref[ire, rows, :] = xr
            s_ref[iim, rows, :] = xi
            cr = jnp.broadcast_to(xr[SUBLANES - 1:SUBLANES], (SUBLANES, LANES))
            ci = jnp.broadcast_to(xi[SUBLANES - 1:SUBLANES], (SUBLANES, LANES))
        cre_ref[j] = cr[0:1]
        cim_ref[j] = ci[0:1]
        return carry

    for j in range(halves * chunks):
        scan_chunk(j, 0)

    ys = []
    for h in range(halves):
        st = jnp.concatenate([s_ref[h * per_half + q] for q in range(per_half)], axis=1)
        ys.append(jnp.dot(st.astype(BF16), ccat_ref[h], preferred_element_type=F32))
    y = jnp.concatenate(ys, axis=1) + d_ref[...] * u
    v = 0.5 * y * (1.0 + jnp.tanh(math.sqrt(2.0 / math.pi) * (y + 0.044715 * (y * y * y))))
    gate = jax.nn.sigmoid(jnp.dot(v.astype(BF16), wg_ref[...], preferred_element_type=F32))
    o_ref[...] = (v * gate).astype(o_ref.dtype)

    @pl.when(i == pl.num_programs(1) - 1)
    def _():
        nre_ref[0] = cre_ref[...]
        nim_ref[0] = cim_ref[...]


def _ssm_weights(ab_re, ab_im, bb_re, bb_im, c_re, c_im, halves):
    g, n, p = c_re.shape
    gh = g // halves
    eye = jnp.eye(gh, dtype=F32)
    bcat, ccat = [], []
    for h in range(halves):
        sl = slice(h * gh, (h + 1) * gh)
        dense_b = lambda t: jnp.einsum('gpn,gk->gnkp', t.reshape(g, p, n)[sl], eye).reshape(gh * n, gh * p)
        dense_c = lambda t: jnp.einsum('gnp,gk->gpkn', t[sl], eye).reshape(gh * p, gh * n)
        bcat.append(jnp.concatenate([dense_b(bb_re), dense_b(bb_im)], axis=1))
        ccat.append(jnp.concatenate([dense_c(c_re), -dense_c(c_im)], axis=0))
    nch = g * p // LANES
    return (ab_re.reshape(nch, 1, LANES), ab_im.reshape(nch, 1, LANES),
            jnp.stack(bcat).astype(BF16), jnp.stack(ccat).astype(BF16))


def _ssm_mixer(mix, h_re, h_im, ssm_w, d_skip, w_glu, row_off, bt, L, prev):
    a_re, a_im, bcat, ccat = ssm_w
    alias_specs, alias_map, alias_in = _alias_args(prev, 9)
    halves = bcat.shape[0]
    width = d_skip.shape[-1]
    nch = a_re.shape[0]
    chunks = nch // halves
    g, p = h_re.shape[1], h_re.shape[2]
    tl = min(SSM_TILE, L)
    nl = L // tl
    blk0 = row_off // tl
    state_spec = pl.BlockSpec((1, nch, 1, LANES), lambda b, i: (b, 0, 0, 0))
    o, n_re, n_im = pl.pallas_call(
        functools.partial(_ssm_kernel, tl=tl, halves=halves, chunks=chunks),
        grid=(bt, nl),
        in_specs=[pl.BlockSpec((tl, width), lambda b, i: (blk0 + b * nl + i, 1)),
                  state_spec, state_spec,
                  _const_spec(a_re.shape), _const_spec(a_im.shape),
                  _const_spec(bcat.shape), _const_spec(ccat.shape),
                  _const_spec((1, width)), _const_spec(w_glu.shape)] + alias_specs,
        out_specs=[pl.BlockSpec((tl, width), lambda b, i: (blk0 + b * nl + i, 0)), state_spec, state_spec],
        out_shape=[jax.ShapeDtypeStruct((mix.shape[0], width), BF16),
                   jax.ShapeDtypeStruct((bt, nch, 1, LANES), F32), jax.ShapeDtypeStruct((bt, nch, 1, LANES), F32)],
        scratch_shapes=[pltpu.VMEM((2 * nch, tl, LANES), F32),
                        pltpu.VMEM((nch, 1, LANES), F32), pltpu.VMEM((nch, 1, LANES), F32),
                        pltpu.VMEM((nch, 2 * len(SCAN_SHIFTS) + 2, SUBLANES, LANES), F32)],
        input_output_aliases=alias_map,
        compiler_params=_cparams(("parallel", "arbitrary")),
        name="ssm_mixer",
    )(mix, h_re.astype(F32).reshape(bt, nch, 1, LANES), h_im.astype(F32).reshape(bt, nch, 1, LANES),
      a_re, a_im, bcat, ccat, d_skip.reshape(1, width), w_glu.astype(BF16), *alias_in)
    return o, n_re.reshape(bt, g, p), n_im.reshape(bt, g, p)


def _attn_init(m_ref, l_ref, acc_ref):
    m_ref[...] = jnp.full(m_ref.shape, -jnp.inf, F32)
    l_ref[...] = jnp.zeros(l_ref.shape, F32)
    acc_ref[...] = jnp.zeros(acc_ref.shape, F32)


def _attn_update(q_of, k_of, v_of, visible, heads, m_ref, l_ref, acc_ref):
    low_lanes = lax.broadcasted_iota(jnp.int32, (1, LANES), 1) < (LANES // 2)
    if visible is not None:
        visible = jnp.concatenate([visible, visible], axis=0)
    for h in range(heads):
        qh, kh, vh = q_of(h), k_of(h), v_of(h)
        zero = jnp.zeros_like(qh)
        qm = jnp.concatenate([jnp.where(low_lanes, qh, zero), jnp.where(low_lanes, zero, qh)], axis=0)
        s = jnp.dot(qm, kh, preferred_element_type=F32)
        if visible is not None:
            s = jnp.where(visible, s, -jnp.inf)
        m_old = m_ref[h]
        m_new = jnp.maximum(m_old, jnp.max(s, axis=1, keepdims=True))
        alpha = jnp.exp2(m_old - m_new)
        ps = [jnp.exp2(s[:, j * LANES:(j + 1) * LANES] - m_new) for j in range(s.shape[1] // LANES)]
        lsum = ps[0]
        for pj in ps[1:]:
            lsum = lsum + pj
        l_ref[h] = alpha * l_ref[h] + lsum
        p = jnp.concatenate([pj.astype(BF16) for pj in ps], axis=1)
        acc_ref[h] = alpha * acc_ref[h] + jnp.dot(p, vh, preferred_element_type=F32)
        m_ref[h] = m_new


def _attn_finalize(lam_ref, sw_ref, o_ref, l_ref, acc_ref, heads, lam_init):
    lp = lam_ref[...]
    lam = (jnp.exp(jnp.sum(lp[0:1] * lp[1:2], axis=1, keepdims=True))
           - jnp.exp(jnp.sum(lp[2:3] * lp[3:4], axis=1, keepdims=True)) + lam_init)
    rows = o_ref.shape[0]
    for h in range(heads):
        oh = acc_ref[h] * (1.0 / jnp.sum(l_ref[h], axis=1, keepdims=True))
        o = oh[:rows] - lam * oh[rows:]
        o = o * lax.rsqrt(jnp.mean(o * o, axis=1, keepdims=True) + LN_EPS) * sw_ref[...] * (1.0 - lam_init)
        o_ref[:, h * LANES:(h + 1) * LANES] = o.astype(o_ref.dtype)


def _head_cols(ref):
    lead = (0,) * (len(ref.shape) - 2)
    return lambda h: ref[(*lead, slice(None), slice(h * LANES, (h + 1) * LANES))].astype(BF16)


def _head_rows(ref):
    lead = (0,) * (len(ref.shape) - 2)
    return lambda h: ref[(*lead, slice(h * LANES, (h + 1) * LANES), slice(None))].astype(BF16)


def _attn_kernel(qi_ref, ki_ref, fl_ref, q_ref, k_ref, v_ref, lam_ref, sw_ref, o_ref, m_ref, l_ref, acc_ref,
                 *, tq, tk, heads, q_pos0, lk, lam_init):
    step = pl.program_id(1)
    qi, ki, fl = qi_ref[step], ki_ref[step], fl_ref[step]

    @pl.when(ki == 0)
    def _():
        _attn_init(m_ref, l_ref, acc_ref)

    def accumulate(masked, keys):
        visible = None
        if masked:
            q_pos = q_pos0 + qi * tq + lax.broadcasted_iota(jnp.int32, (tq, 1), 0)
            k_pos = ki * tk + lax.broadcasted_iota(jnp.int32, (1, keys), 1)
            visible = (k_pos < (q_pos // CHUNK + 1) * CHUNK) & (k_pos < lk)
        k_of = lambda h: k_ref[0, h * LANES:(h + 1) * LANES, 0:keys]
        v_of = lambda h: v_ref[0, 0:keys, h * LANES:(h + 1) * LANES]
        _attn_update(_head_cols(q_ref), k_of, v_of, visible, heads, m_ref, l_ref, acc_ref)

    @pl.when((fl & 1) == 0)
    def _():
        accumulate(False, tk)

    @pl.when(((fl & 1) != 0) & ((fl & 4) == 0))
    def _():
        accumulate(True, tk)

    @pl.when((fl & 4) != 0)
    def _():
        accumulate(True, tk // 2)

    @pl.when((fl & 2) != 0)
    def _():
        _attn_finalize(lam_ref, sw_ref, o_ref, l_ref, acc_ref, heads, lam_init)


def _decode_attn_kernel(q_ref, ck_ref, cv_ref, nk_ref, nv_ref, lam_ref, sw_ref, joint_ref, o_ref, m_ref, l_ref,
                        acc_ref, *, tq, heads, past, n_cache, lam_init):
    j = pl.program_id(1)

    @pl.when(j == 0)
    def _():
        _attn_init(m_ref, l_ref, acc_ref)

    @pl.when(j < n_cache)
    def _():
        n_keys = cv_ref.shape[2] // heads
        cached_v = lambda h: cv_ref[0, 0, pl.ds(h, n_keys, stride=heads), :].astype(BF16)
        _attn_update(_head_cols(q_ref), _head_rows(ck_ref), cached_v, None, heads, m_ref, l_ref, acc_ref)

    @pl.when(j == n_cache)
    def _():
        nk = nk_ref.shape[2]
        t = lax.broadcasted_iota(jnp.int32, (tq, 1), 0)
        i = lax.broadcasted_iota(jnp.int32, (1, nk), 1)
        visible = (past + i < ((past + t) // CHUNK + 1) * CHUNK) & (i < tq)
        _attn_update(_head_cols(q_ref), _head_rows(nk_ref), _head_cols(nv_ref), visible, heads, m_ref, l_ref, acc_ref)
        _attn_finalize(lam_ref, sw_ref, o_ref, l_ref, acc_ref, heads, lam_init)


def _attn_schedule(L, lk, tq, tk, q_pos0):
    qi, ki, fl = [], [], []
    for a in range(L // tq):
        first_end = ((q_pos0 + a * tq) // CHUNK + 1) * CHUNK
        last_end = min(((q_pos0 + a * tq + tq - 1) // CHUNK + 1) * CHUNK, lk)
        nk = -(-last_end // tk)
        for b in range(nk):
            full = (b + 1) * tk <= min(first_end, lk)
            half = not full and last_end <= b * tk + tk // 2
            qi.append(a); ki.append(b); fl.append((0 if full else 1) | (2 if b == nk - 1 else 0) | (4 if half else 0))
    return tuple(jnp.asarray(np.asarray(t, np.int32)) for t in (qi, ki, fl))


def _attention(q, kt_all, v_all, lam_p, subln_w, bt, L, lk, q_pos0, lam_init, total_rows):
    width = q.shape[-1]
    heads = width // LANES
    tq = min(ATTN_Q_TILE, L)
    tk = ATTN_TILE
    lk_pad = kt_all.shape[2]
    nq = L // tq
    qi, ki, fl = _attn_schedule(L, lk, tq, tk, q_pos0)
    grid_spec = pltpu.PrefetchScalarGridSpec(
        num_scalar_prefetch=3,
        grid=(bt, int(qi.shape[0])),
        in_specs=[pl.BlockSpec((1, tq, width), lambda b, s, qi, ki, fl: (b, qi[s], 0)),
                  pl.BlockSpec((1, width, tk), lambda b, s, qi, ki, fl: (b, 0, ki[s])),
                  pl.BlockSpec((1, tk, width), lambda b, s, qi, ki, fl: (b, ki[s], 0)),
                  pl.BlockSpec(lam_p.shape, lambda b, s, qi, ki, fl: (0, 0)),
                  pl.BlockSpec((1, LANES), lambda b, s, qi, ki, fl: (0, 0))],
        out_specs=pl.BlockSpec((tq, width), lambda b, s, qi, ki, fl: (b * nq + qi[s], 0)),
        scratch_shapes=[pltpu.VMEM((heads, 2 * tq, LANES), F32)] * 3)
    assert lk_pad % tk == 0 and L % tq == 0
    return pl.pallas_call(
        functools.partial(_attn_kernel, tq=tq, tk=tk, heads=heads, q_pos0=q_pos0, lk=lk, lam_init=lam_init),
        grid_spec=grid_spec,
        out_shape=jax.ShapeDtypeStruct((total_rows, width), BF16),
        compiler_params=_cparams(("parallel", "arbitrary")),
        name="diff_attention",
    )(qi, ki, fl, q.reshape(bt, L, width), kt_all, v_all, lam_p, subln_w.reshape(1, LANES))


def _decode_attention(q, cache_kt, cache_v, layer, new_k, new_v, lam_p, subln_w, bt, L, past, lam_init, row_off, joint):
    width = q.shape[-1]
    heads = width // LANES
    tk = min(DECODE_TILE, past)
    assert past % tk == 0 and past >= tk and L <= LANES and past % CHUNK == 0 and row_off % L == 0
    n_cache = past // tk
    new_kt = jnp.pad(jnp.swapaxes(new_k.reshape(bt, L, width), 1, 2), ((0, 0), (0, 0), (0, LANES - L)))
    new_v = jnp.pad(new_v.reshape(bt, L, width), ((0, 0), (0, LANES - L), (0, 0)))
    row_spec = pl.BlockSpec((1, L, width), lambda b, j: (b, 0, 0))
    scratch = pltpu.VMEM((heads, 2 * L, LANES), F32)
    return pl.pallas_call(
        functools.partial(_decode_attn_kernel, tq=L, heads=heads, past=past, n_cache=n_cache, lam_init=lam_init),
        grid=(bt, n_cache + 1),
        in_specs=[row_spec,
                  pl.BlockSpec((1, 1, width, tk), lambda b, j: (layer, b, 0, jnp.minimum(j, n_cache - 1))),
                  pl.BlockSpec((1, 1, tk * heads, LANES), lambda b, j: (layer, b, jnp.minimum(j, n_cache - 1), 0)),
                  pl.BlockSpec((1, width, LANES), lambda b, j: (b, 0, 0)),
                  pl.BlockSpec((1, LANES, width), lambda b, j: (b, 0, 0)),
                  pl.BlockSpec(lam_p.shape, lambda b, j: (0, 0)), pl.BlockSpec((1, LANES), lambda b, j: (0, 0)),
                  pl.BlockSpec(memory_space=pl.ANY)],
        out_specs=pl.BlockSpec((L, width), lambda b, j: (row_off // L + b, 0)),
        out_shape=jax.ShapeDtypeStruct(joint.shape, BF16),
        scratch_shapes=[scratch, scratch, scratch],
        input_output_aliases={7: 0},
        compiler_params=_cparams(("parallel", "arbitrary")),
        name="decode_attention",
    )(q.reshape(bt, L, width), cache_kt, cache_v, new_kt, new_v, lam_p, subln_w.reshape(1, LANES), joint)


def _merge_kernel(x_ref, g_ref, op_ref, os_ref, oc_ref, oa_ref, wb_ref, wo_ref, lg_ref, lb_ref, o_ref,
                  *, d_model, alpha, offs):
    merged = None
    for b, (o_b, (lo, hi)) in enumerate(zip((op_ref, os_ref, oc_ref, oa_ref), offs)):
        t = jnp.dot(o_b[...], wb_ref[lo:hi, :], preferred_element_type=F32)
        t = t * g_ref[:, b * d_model:(b + 1) * d_model].astype(F32)
        merged = t if merged is None else merged + t
    y = alpha * x_ref[...] + jnp.dot(merged.astype(BF16), wo_ref[...], preferred_element_type=F32)
    o_ref[...] = _layer_norm(y, lg_ref[...], lb_ref[...])


def _merge(x, gates, o_pool, o_ssm, o_conv, o_attn, w_branch, w_out, ln_g, ln_b, alpha):
    m, d = x.shape
    widths = [o_pool.shape[1], o_ssm.shape[1], o_conv.shape[1], o_attn.shape[1]]
    ends = np.cumsum(widths)
    offs = tuple((int(e - w), int(e)) for e, w in zip(ends, widths))
    row = lambda cols: pl.BlockSpec((ROW_TILE, cols), lambda i: (i, 0))
    return pl.pallas_call(
        functools.partial(_merge_kernel, d_model=d, alpha=alpha, offs=offs),
        grid=(m // ROW_TILE,),
        in_specs=[row(d), row(gates.shape[1])] + [row(w) for w in widths]
                 + [_const_spec(w_branch.shape), _const_spec(w_out.shape), _const_spec((1, d)), _const_spec((1, d))],
        out_specs=row(d),
        out_shape=jax.ShapeDtypeStruct((m, d), F32),
        compiler_params=_cparams(("parallel",)),
        name="merge_out_ln",
    )(x, gates, o_pool, o_ssm, o_conv, o_attn, w_branch, w_out, ln_g.reshape(1, d), ln_b.reshape(1, d))


def _ffn_kernel(x_ref, wu_ref, wd_ref, lg_ref, lb_ref, o_ref, *, alpha, chunk):
    x = x_ref[...]
    xb = x.astype(BF16)
    acc = alpha * x
    for c in range(wu_ref.shape[1] // chunk):
        hid = jnp.dot(xb, wu_ref[:, c * chunk:(c + 1) * chunk], preferred_element_type=F32)
        hid = jnp.square(jnp.maximum(hid, 0.0)).astype(BF16)
        acc = acc + jnp.dot(hid, wd_ref[c * chunk:(c + 1) * chunk, :], preferred_element_type=F32)
    o_ref[...] = _layer_norm(acc, lg_ref[...], lb_ref[...])


def _ffn(x, w_up, w_down, ln_g, ln_b, alpha, row_off=0, rows=None):
    d = x.shape[1]
    rows = x.shape[0] if rows is None else rows
    blk0 = row_off // ROW_TILE
    return pl.pallas_call(
        functools.partial(_ffn_kernel, alpha=alpha, chunk=1024),
        grid=(rows // ROW_TILE,),
        in_specs=[pl.BlockSpec((ROW_TILE, d), lambda i: (blk0 + i, 0)),
                  _const_spec(w_up.shape), _const_spec(w_down.shape), _const_spec((1, d)), _const_spec((1, d))],
        out_specs=pl.BlockSpec((ROW_TILE, d), lambda i: (i, 0)),
        out_shape=jax.ShapeDtypeStruct((rows, d), F32),
        compiler_params=_cparams(("parallel",)),
        name="ffn_ln",
    )(x, w_up, w_down, ln_g.reshape(1, d), ln_b.reshape(1, d))


def kernel(x_prompt, x_sample, cache_k, cache_v, state_ssm_re, state_ssm_im, state_conv, state_pool, w_in, pool_w, pool_scale, ssm_a_re, ssm_a_im, ssm_log_dt, ssm_b_re, ssm_b_im, ssm_c_re, ssm_c_im, ssm_d, ssm_w_glu, conv_w, conv_b, lambda_q1, lambda_k1, lambda_q2, lambda_k2, subln_w, w_branch, w_out, ln1_g, ln1_b, w_up, w_down, ln2_g, ln2_b):
    depth = w_in.shape[0]
    bp, lp, d = x_prompt.shape
    bs, ls, _ = x_sample.shape
    past = cache_k.shape[2]
    heads, qk_dim = cache_k.shape[3], cache_k.shape[5]
    width_qk = heads * 2 * qk_dim
    pool_width, ssm_width, conv_width = pool_scale.shape[1], ssm_d.shape[1], conv_w.shape[2]
    n_mix = pool_width + ssm_width + 3 * conv_width
    alpha = float((2 * depth) ** 0.25)
    paths = ((bp, lp, 0, 0), (bs, ls, past, bp * lp))

    x = jnp.concatenate([x_prompt.reshape(bp * lp, d), x_sample.reshape(bs * ls, d)], axis=0)
    tables = (_rope_tables(jnp.arange(lp, dtype=jnp.int32)),
              _rope_tables(jnp.tile(past + jnp.arange(ls, dtype=jnp.int32), bs)))

    cache_kt = jnp.transpose(cache_k, (0, 1, 3, 4, 5, 2)).reshape(depth, bs, width_qk, past)
    cache_vr = cache_v.reshape(depth, bs, past * heads, 2 * qk_dim)

    outs = [[[] for _ in range(6)] for _ in paths]
    kv_stack = None
    for l in range(depth):
        wl = w_in[l].astype(BF16)
        mix = _project(x, wl[:, :n_mix], F32, sigmoid=False)
        gates = _project(x, wl[:, n_mix + 3 * width_qk:], BF16, sigmoid=True)
        ab_re, ab_im, bb_re, bb_im = _discretize(ssm_a_re[l], ssm_a_im[l], ssm_log_dt[l], ssm_b_re[l], ssm_b_im[l])
        ssm_w = _ssm_weights(ab_re, ab_im, bb_re, bb_im, ssm_c_re[l], ssm_c_im[l], halves=2)
        lam_p = jnp.stack([lambda_q1[l], lambda_k1[l], lambda_q2[l], lambda_k2[l]]).astype(F32)
        lam_init = 0.8 - 0.6 * math.exp(-0.3 * l)

        w_qkv = wl[:, n_mix:n_mix + 3 * width_qk]
        q_scale = float(qk_dim) ** -0.5 * math.log2(math.e)
        o_pool = o_conv = o_ssm = o_attn = None
        for pi, (bt, L, pos0, row_off) in enumerate(paths):
            if pi == 0:
                q, k_stack, v_stack, kbt, vb = _qkv_project_stacked(x, w_qkv, tables[pi], bt, L, q_scale, l, depth,
                                                                    kv_stack)
                kv_stack = (k_stack, v_stack)
                hist_pool = jnp.zeros((bt, state_pool.shape[2], pool_width), F32)
                hist_conv = jnp.zeros((bt, state_conv.shape[2], conv_width), F32)
                h_re = h_im = jnp.zeros((bt,) + state_ssm_re.shape[2:], F32)
                o_attn = _attention(q, kbt, vb.reshape(bt, L, width_qk), lam_p, subln_w[l], bt, L, L, pos0, lam_init,
                                    x.shape[0])
                kv_new = ()
            else:
                q, kf, vf, kb, vb = _qkv_project(x, w_qkv, tables[pi], row_off, bt * L, q_scale)
                hist_pool, hist_conv, h_re, h_im = state_pool[l], state_conv[l], state_ssm_re[l], state_ssm_im[l]
                o_attn = _decode_attention(q, cache_kt, cache_vr, l, kb, vb, lam_p, subln_w[l], bt, L, past, lam_init,
                                           row_off, o_attn)
                kv_new = (kf.reshape(bt, L, heads, 2, qk_dim), vf.reshape(bt, L, heads, 2 * qk_dim))
            o_pool, o_conv, new_pool, new_conv = _local_mixers(
                mix, hist_pool, hist_conv, pool_w[l], pool_scale[l], conv_w[l], conv_b[l], row_off, bt, L, pos0,
                None if pi == 0 else (o_pool, o_conv))
            o_ssm, new_re, new_im = _ssm_mixer(mix, h_re, h_im, ssm_w, ssm_d[l], ssm_w_glu[l], row_off, bt, L,
                                               None if pi == 0 else (o_ssm,))
            for slot, val in zip(outs[pi], (new_re, new_im, new_conv, new_pool) + kv_new):
                slot.append(val)

        x = _merge(x, gates, o_pool, o_ssm, o_conv, o_attn, w_branch[l].astype(BF16), w_out[l].astype(BF16),
                   ln1_g[l], ln1_b[l], alpha)
        ffn_w = (w_up[l].astype(BF16), w_down[l].astype(BF16), ln2_g[l], ln2_b[l], alpha)
        if l + 1 < depth:
            x = _ffn(x, *ffn_w)

    y_prompt, y_sample = (_ffn(x, *ffn_w, row_off, bt * L).reshape(bt, L, d) for bt, L, _, row_off in paths)
    k_stack, v_stack = kv_stack
    k_prompt = jnp.transpose(k_stack.reshape(depth, bp, heads, 2, qk_dim, lp), (0, 1, 5, 2, 3, 4))
    v_prompt = v_stack.reshape(depth, bp, lp, heads, 2 * qk_dim)
    (p_re, p_im, p_conv, p_pool), (s_re, s_im, s_conv, s_pool, s_k, s_v) = (
        [jnp.stack(slot) for slot in path_outs if slot] for path_outs in outs)
    return (y_prompt, y_sample, k_prompt, v_prompt, p_re, p_im, p_conv, p_pool,
            s_k, s_v, s_re, s_im, s_conv, s_pool)
```

```python
import functools
import math

import numpy as np
import jax
import jax.numpy as jnp
from jax import lax
from jax.experimental import pallas as pl
from jax.experimental.pallas import tpu as pltpu

F32 = jnp.float32
BF16 = jnp.bfloat16

LANES = 128
SUBLANES = 8
SCAN_SHIFTS = (1, 2, 4)
CHUNK = 64
POOL_WINDOWS = (2, 4, 8, 16)
POOL_PAD = 16
CONV_PAD = 8
ROT_DIM = 16
ROPE_THETA = 500000.0
LN_EPS = 1e-5
VMEM_LIMIT = 56 * 1024 * 1024

ROW_TILE = 512
ATTN_TILE = 1024
ATTN_Q_TILE = 512
DECODE_TILE = 1024
SSM_TILE = 256


def _cparams(sem):
    return pltpu.CompilerParams(dimension_semantics=sem, vmem_limit_bytes=VMEM_LIMIT)


def _const_spec(shape):
    zeros = (0,) * len(shape)
    return pl.BlockSpec(shape, lambda *_: zeros, pipeline_mode=pl.Buffered(1))


def _layer_norm(y, g, b):
    mu = jnp.mean(y, axis=-1, keepdims=True)
    d = y - mu
    var = jnp.mean(d * d, axis=-1, keepdims=True)
    return d * lax.rsqrt(var + LN_EPS) * g + b


def _token_specs(x_parts):
    k = x_parts[0].shape[1]
    if len(x_parts) == 1:
        return [pl.BlockSpec((ROW_TILE, k), lambda i: (i, 0))]
    nh = x_parts[0].shape[0] // ROW_TILE
    return [pl.BlockSpec((ROW_TILE, k), lambda i: (jnp.minimum(i, nh - 1), 0)),
            pl.BlockSpec((ROW_TILE, k), lambda i: (jnp.maximum(i - nh, 0), 0))]


def _token_tile(x_refs, head_tiles):
    if len(x_refs) == 1:
        return x_refs[0][...]
    return jnp.where(pl.program_id(0) < head_tiles, x_refs[0][...], x_refs[1][...])


def _proj_kernel(*refs, sigmoid, n_x, head_tiles):
    w_ref, o_ref = refs[n_x:]
    y = jnp.dot(_token_tile(refs[:n_x], head_tiles).astype(BF16), w_ref[...], preferred_element_type=F32)
    if sigmoid:
        y = 0.5 * jnp.tanh(0.5 * y) + 0.5
    o_ref[...] = y.astype(o_ref.dtype)


def _project(x_parts, w, out_dtype, sigmoid):
    m = sum(part.shape[0] for part in x_parts)
    n = w.shape[1]
    return pl.pallas_call(
        functools.partial(_proj_kernel, sigmoid=sigmoid, n_x=len(x_parts), head_tiles=x_parts[0].shape[0] // ROW_TILE),
        grid=(m // ROW_TILE,),
        in_specs=_token_specs(x_parts) + [_const_spec(w.shape)],
        out_specs=pl.BlockSpec((ROW_TILE, n), lambda i: (i, 0)),
        out_shape=jax.ShapeDtypeStruct((m, n), out_dtype),
        compiler_params=_cparams(("parallel",)),
        name="proj_sigmoid" if sigmoid else "proj_plain",
    )(*x_parts, w)


def _rope(y, cos, sin_up, sin_dn):
    outs = []
    for c in range(y.shape[1] // LANES):
        yc = y[:, c * LANES:(c + 1) * LANES]
        outs.append(yc * cos + pltpu.roll(yc, LANES - ROT_DIM // 2, 1) * sin_up
                    + pltpu.roll(yc, ROT_DIM // 2, 1) * sin_dn)
    return jnp.concatenate(outs, axis=1)


def _qkv_kernel(x_ref, w_ref, cos_ref, su_ref, sd_ref, *refs, width, q_scale, stacked):
    q_ref, kf_ref, vf_ref, kb_ref, vb_ref = refs[-5:]
    xb = x_ref[...].astype(BF16)
    cos, su, sd = cos_ref[...], su_ref[...], sd_ref[...]
    q = jnp.dot(xb, w_ref[:, 0:width], preferred_element_type=F32)
    q_ref[...] = (_rope(q, cos, su, sd) * q_scale).astype(BF16)
    k = _rope(jnp.dot(xb, w_ref[:, width:2 * width], preferred_element_type=F32), cos, su, sd)
    v = jnp.dot(xb, w_ref[:, 2 * width:3 * width], preferred_element_type=F32)
    if stacked:
        kt = k.T
        kf_ref[0, 0] = kt
        kb_ref[0] = kt.astype(BF16)
        heads = width // LANES
        for h in range(heads):
            vf_ref[0, pl.ds(h, v.shape[0], stride=heads), :] = v[:, h * LANES:(h + 1) * LANES]
    else:
        kf_ref[...] = k
        kb_ref[...] = k.astype(BF16)
        vf_ref[...] = v
    vb_ref[...] = v.astype(BF16)


def _qkv_project(x_all, w_qkv, tables, row_off, rows, q_scale):
    k = x_all.shape[1]
    width = w_qkv.shape[1] // 3
    tm = min(ROW_TILE, rows)
    blk0 = row_off // tm
    row_spec = lambda cols: pl.BlockSpec((tm, cols), lambda i: (i, 0))
    tab_spec = pl.BlockSpec((tm, LANES), lambda i: (i, 0))
    return pl.pallas_call(
        functools.partial(_qkv_kernel, width=width, q_scale=q_scale, stacked=False),
        grid=(rows // tm,),
        in_specs=[pl.BlockSpec((tm, k), lambda i: (blk0 + i, 0)), _const_spec(w_qkv.shape),
                  tab_spec, tab_spec, tab_spec],
        out_specs=[row_spec(width)] * 5,
        out_shape=[jax.ShapeDtypeStruct((rows, width), BF16),
                   jax.ShapeDtypeStruct((rows, width), F32), jax.ShapeDtypeStruct((rows, width), F32),
                   jax.ShapeDtypeStruct((rows, width), BF16), jax.ShapeDtypeStruct((rows, width), BF16)],
        compiler_params=_cparams(("parallel",)),
        name="proj_qkv",
    )(x_all, w_qkv, *tables)


def _qkv_project_stacked(x_all, w_qkv, tables, bt, L, q_scale, layer, depth, prev):
    k = x_all.shape[1]
    width = w_qkv.shape[1] // 3
    rows = bt * L
    tm = min(ROW_TILE, L)
    nl = L // tm
    tab_spec = pl.BlockSpec((tm, LANES), lambda i: (i % nl, 0))
    row_spec = pl.BlockSpec((tm, width), lambda i: (i, 0))
    any_spec = pl.BlockSpec(memory_space=pl.ANY)
    n_prev = 0 if prev is None else 2
    return pl.pallas_call(
        functools.partial(_qkv_kernel, width=width, q_scale=q_scale, stacked=True),
        grid=(rows // tm,),
        in_specs=[pl.BlockSpec((tm, k), lambda i: (i, 0)), _const_spec(w_qkv.shape),
                  tab_spec, tab_spec, tab_spec] + [any_spec] * n_prev,
        out_specs=[row_spec,
                   pl.BlockSpec((1, 1, width, tm), lambda i: (layer, i // nl, 0, i % nl)),
                   pl.BlockSpec((1, tm * (width // LANES), LANES), lambda i: (layer, i, 0)),
                   pl.BlockSpec((1, width, tm), lambda i: (i // nl, 0, i % nl)),
                   row_spec],
        out_shape=[jax.ShapeDtypeStruct((rows, width), BF16),
                   jax.ShapeDtypeStruct((depth, bt, width, L), F32),
                   jax.ShapeDtypeStruct((depth, rows * (width // LANES), LANES), F32),
                   jax.ShapeDtypeStruct((bt, width, L), BF16), jax.ShapeDtypeStruct((rows, width), BF16)],
        input_output_aliases={} if prev is None else {5: 1, 6: 2},
        compiler_params=_cparams(("parallel",)),
        name="proj_qkv_stacked",
    )(x_all, w_qkv, *tables, *(() if prev is None else prev))


def _rope_tables(pos):
    half = ROT_DIM // 2
    inv = ROPE_THETA ** (-jnp.arange(0, ROT_DIM, 2, dtype=F32) / ROT_DIM)
    ang = pos.astype(F32)[:, None] * inv[None, :]
    cos, sin = jnp.cos(ang), jnp.sin(ang)
    n = pos.shape[0]
    ones = jnp.ones((n, 64 - ROT_DIM), F32)
    zeros = jnp.zeros((n, 64 - half), F32)
    cos64 = jnp.concatenate([cos, cos, ones], axis=1)
    up64 = jnp.concatenate([-sin, zeros], axis=1)
    dn64 = jnp.concatenate([jnp.zeros((n, half), F32), sin, jnp.zeros((n, 64 - ROT_DIM), F32)], axis=1)
    return tuple(jnp.concatenate([t, t], axis=1) for t in (cos64, up64, dn64))


def _local_kernel(z_ref, h_ref, b_ref, c_ref, hp_ref, hc_ref, pw_ref, ps_ref, cw_ref, cb_ref, *refs, tl, pos0):
    op_ref, oc_ref, np_ref, nc_ref, zbuf, cbuf = refs[-6:]
    i = pl.program_id(1)
    last = pl.num_programs(1) - 1

    @pl.when(i == 0)
    def _():
        zbuf[0:POOL_PAD, :] = hp_ref[0]
        cbuf[0:CONV_PAD, :] = hc_ref[0]

    @pl.when(i > 0)
    def _():
        zbuf[0:POOL_PAD, :] = zbuf[tl:tl + POOL_PAD, :]
        cbuf[0:CONV_PAD, :] = cbuf[tl:tl + CONV_PAD, :]

    z = z_ref[...]
    zbuf[POOL_PAD:POOL_PAD + tl, :] = z
    cbuf[CONV_PAD:CONV_PAD + tl, :] = c_ref[...] * h_ref[...]

    pos = pos0 + i * tl + lax.broadcasted_iota(jnp.int32, (tl, 1), 0)
    gw = z.shape[1] // len(POOL_WINDOWS)
    outs = []
    for g, w in enumerate(POOL_WINDOWS):
        cols = slice(g * gw, (g + 1) * gw)
        s = z[:, cols]
        for j in range(1, w):
            s = s + zbuf[POOL_PAD - j:POOL_PAD - j + tl, cols]
        cnt = jnp.minimum(w, pos + 1).astype(F32)
        u = s / cnt - z[:, cols]
        outs.append(jnp.dot(u.astype(BF16), pw_ref[g], preferred_element_type=F32))
    op_ref[...] = (jnp.concatenate(outs, axis=1) * ps_ref[...]).astype(op_ref.dtype)

    y = cb_ref[...] + cbuf[CONV_PAD - 2:CONV_PAD - 2 + tl, :] * cw_ref[0:1, :]
    y = y + cbuf[CONV_PAD - 1:CONV_PAD - 1 + tl, :] * cw_ref[1:2, :]
    y = y + cbuf[CONV_PAD:CONV_PAD + tl, :] * cw_ref[2:3, :]
    oc_ref[...] = (b_ref[...] * y).astype(oc_ref.dtype)

    @pl.when(i == last)
    def _():
        np_ref[0] = zbuf[tl + 1:tl + POOL_PAD, :]
        nc_ref[0] = cbuf[tl + CONV_PAD - 2:tl + CONV_PAD, :]


def _alias_args(prev, n_inputs):
    if prev is None:
        return [], {}, ()
    return ([pl.BlockSpec(memory_space=pl.ANY)] * len(prev),
            {n_inputs + k: k for k in range(len(prev))}, tuple(prev))


def _local_mixers(mix, hist_pool, hist_conv, pool_w, pool_scale, conv_w, conv_b, row_off, bt, L, pos0, prev):
    width = pool_scale.shape[-1]
    tl = min(ROW_TILE, L)
    nl = L // tl
    blk0 = row_off // tl
    col_spec = lambda cb: pl.BlockSpec((tl, width), lambda b, i: (blk0 + b * nl + i, cb))
    out_spec = pl.BlockSpec((tl, width), lambda b, i: (blk0 + b * nl + i, 0))
    alias_specs, alias_map, alias_in = _alias_args(prev, 10)
    hp = jnp.pad(hist_pool, ((0, 0), (POOL_PAD - hist_pool.shape[1], 0), (0, 0)))
    hc = jnp.pad(hist_conv, ((0, 0), (CONV_PAD - hist_conv.shape[1], 0), (0, 0)))
    n_hp, n_hc = hist_pool.shape[1], hist_conv.shape[1]
    return pl.pallas_call(
        functools.partial(_local_kernel, tl=tl, pos0=pos0),
        grid=(bt, nl),
        in_specs=[col_spec(0), col_spec(2), col_spec(3), col_spec(4),
                  pl.BlockSpec((1, POOL_PAD, width), lambda b, i: (b, 0, 0)),
                  pl.BlockSpec((1, CONV_PAD, width), lambda b, i: (b, 0, 0)),
                  _const_spec(pool_w.shape), _const_spec((1, width)),
                  _const_spec(conv_w.shape), _const_spec((1, width))] + alias_specs,
        out_specs=[out_spec, out_spec,
                   pl.BlockSpec((1, n_hp, width), lambda b, i: (b, 0, 0)),
                   pl.BlockSpec((1, n_hc, width), lambda b, i: (b, 0, 0))],
        out_shape=[jax.ShapeDtypeStruct((mix.shape[0], width), BF16), jax.ShapeDtypeStruct((mix.shape[0], width), BF16),
                   jax.ShapeDtypeStruct((bt, n_hp, width), F32), jax.ShapeDtypeStruct((bt, n_hc, width), F32)],
        scratch_shapes=[pltpu.VMEM((POOL_PAD + tl, width), F32), pltpu.VMEM((CONV_PAD + tl, width), F32)],
        input_output_aliases=alias_map,
        compiler_params=_cparams(("parallel", "arbitrary")),
        name="local_mixers",
    )(mix, mix, mix, mix, hp, hc, pool_w.astype(BF16), pool_scale.reshape(1, width),
      conv_w, conv_b.reshape(1, width), *alias_in)


def _discretize_kernel(are_ref, aim_ref, ldt_ref, bre_ref, bim_ref, abr_ref, abi_ref, bbr_ref, bbi_ref):
    a_re, a_im = are_ref[...], aim_ref[...]
    dt = jnp.exp(ldt_ref[...])
    mag = jnp.exp(a_re * dt)
    ab_re = mag * jnp.cos(a_im * dt)
    ab_im = mag * jnp.sin(a_im * dt)
    den = a_re * a_re + a_im * a_im
    cr = ((ab_re - 1.0) * a_re + ab_im * a_im) / den
    ci = (ab_im * a_re - (ab_re - 1.0) * a_im) / den
    b_re, b_im = bre_ref[...], bim_ref[...]
    abr_ref[...] = ab_re
    abi_ref[...] = ab_im
    bbr_ref[...] = cr * b_re - ci * b_im
    bbi_ref[...] = cr * b_im + ci * b_re


def _discretize(a_re, a_im, log_dt, b_re, b_im):
    g, p = a_re.shape
    n = b_re.shape[-1]
    col = lambda t: t.reshape(g * p, 1)
    ldt = jnp.broadcast_to(log_dt[:, None], (g, p))
    shapes = [jax.ShapeDtypeStruct((g * p, 1), F32)] * 2 + [jax.ShapeDtypeStruct((g * p, n), F32)] * 2
    return pl.pallas_call(_discretize_kernel, out_shape=shapes, name="ssm_discretize")(
        col(a_re), col(a_im), col(ldt), b_re.reshape(g * p, n), b_im.reshape(g * p, n))


def _scan_tables(are_ref, aim_ref, tab_ref):
    sub = lax.broadcasted_iota(jnp.int32, (SUBLANES, LANES), 0)
    for j in range(tab_ref.shape[0]):
        ar, ai = are_ref[j], aim_ref[j]
        powers = [(ar, ai)]
        for _ in range(SUBLANES - 1):
            pr, pi = powers[-1]
            powers.append((pr * ar - pi * ai, pr * ai + pi * ar))
        for t, d in enumerate(SCAN_SHIFTS):
            dr, di = powers[d - 1]
            tab_ref[j, 2 * t] = jnp.where(sub >= d, dr, 0.0)
            tab_ref[j, 2 * t + 1] = jnp.where(sub >= d, di, 0.0)
        tab_ref[j, 2 * len(SCAN_SHIFTS)] = jnp.concatenate([p[0] for p in powers], axis=0)
        tab_ref[j, 2 * len(SCAN_SHIFTS) + 1] = jnp.concatenate([p[1] for p in powers], axis=0)


def _ssm_kernel(u_ref, hre_ref, him_ref, are_ref, aim_ref, bcat_ref, ccat_ref, d_ref, wg_ref, *refs,
                tl, halves, chunks):
    o_ref, nre_ref, nim_ref, s_ref, cre_ref, cim_ref, tab_ref = refs[-7:]
    i = pl.program_id(1)

    @pl.when(i == 0)
    def _():
        cre_ref[...] = hre_ref[0]
        cim_ref[...] = him_ref[0]

    u = u_ref[...]
    ub = u.astype(BF16)
    kw = ub.shape[1] // halves
    per_half = 2 * chunks
    for h in range(halves):
        bu = jnp.dot(ub[:, h * kw:(h + 1) * kw], bcat_ref[h], preferred_element_type=F32)
        for q in range(per_half):
            s_ref[h * per_half + q] = bu[:, q * LANES:(q + 1) * LANES]

    @pl.when(i == 0)
    def _():
        _scan_tables(are_ref, aim_ref, tab_ref)

    def scan_chunk(j, carry):
        ire = (j // chunks) * per_half + (j % chunks)
        iim = ire + chunks
        cr = jnp.broadcast_to(cre_ref[j], (SUBLANES, LANES))
        ci = jnp.broadcast_to(cim_ref[j], (SUBLANES, LANES))
        steps = [(d, tab_ref[j, 2 * t], tab_ref[j, 2 * t + 1]) for t, d in enumerate(SCAN_SHIFTS)]
        pw_re, pw_im = tab_ref[j, 2 * len(SCAN_SHIFTS)], tab_ref[j, 2 * len(SCAN_SHIFTS) + 1]
        for r in range(tl // SUBLANES):
            rows = pl.ds(SUBLANES * r, SUBLANES)
            xr, xi = s_ref[ire, rows, :], s_ref[iim, rows, :]
            for d, mr, mi in steps:
                sr, si = pltpu.roll(xr, d, 0), pltpu.roll(xi, d, 0)
                xr, xi = xr + mr * sr - mi * si, xi + mr * si + mi * sr
            xr, xi = xr + pw_re * cr - pw_im * ci, xi + pw_re * ci + pw_im * cr
            s_ref[ire, rows, :] = xr
            s_ref[iim, rows, :] = xi
            cr = jnp.broadcast_to(xr[SUBLANES - 1:SUBLANES], (SUBLANES, LANES))
            ci = jnp.broadcast_to(xi[SUBLANES - 1:SUBLANES], (SUBLANES, LANES))
        cre_ref[j] = cr[0:1]
        cim_ref[j] = ci[0:1]
        return carry

    for j in range(halves * chunks):
        scan_chunk(j, 0)

    ys = []
    for h in range(halves):
        st = jnp.concatenate([s_ref[h * per_half + q] for q in range(per_half)], axis=1)
        ys.append(jnp.dot(st.astype(BF16), ccat_ref[h], preferred_element_type=F32))
    y = jnp.concatenate(ys, axis=1) + d_ref[...] * u
    v = 0.5 * y * (1.0 + jnp.tanh(math.sqrt(2.0 / math.pi) * (y + 0.044715 * (y * y * y))))
    gate = jax.nn.sigmoid(jnp.dot(v.astype(BF16), wg_ref[...], preferred_element_type=F32))
    o_ref[...] = (v * gate).astype(o_ref.dtype)

    @pl.when(i == pl.num_programs(1) - 1)
    def _():
        nre_ref[0] = cre_ref[...]
        nim_ref[0] = cim_ref[...]


def _ssm_weights(ab_re, ab_im, bb_re, bb_im, c_re, c_im, halves):
    g, n, p = c_re.shape
    gh = g // halves
    eye = jnp.eye(gh, dtype=F32)
    bcat, ccat = [], []
    for h in range(halves):
        sl = slice(h * gh, (h + 1) * gh)
        dense_b = lambda t: jnp.einsum('gpn,gk->gnkp', t.reshape(g, p, n)[sl], eye).reshape(gh * n, gh * p)
        dense_c = lambda t: jnp.einsum('gnp,gk->gpkn', t[sl], eye).reshape(gh * p, gh * n)
        bcat.append(jnp.concatenate([dense_b(bb_re), dense_b(bb_im)], axis=1))
        ccat.append(jnp.concatenate([dense_c(c_re), -dense_c(c_im)], axis=0))
    nch = g * p // LANES
    return (ab_re.reshape(nch, 1, LANES), ab_im.reshape(nch, 1, LANES),
            jnp.stack(bcat).astype(BF16), jnp.stack(ccat).astype(BF16))


def _ssm_mixer(mix, h_re, h_im, ssm_w, d_skip, w_glu, row_off, bt, L, prev):
    a_re, a_im, bcat, ccat = ssm_w
    alias_specs, alias_map, alias_in = _alias_args(prev, 9)
    halves = bcat.shape[0]
    width = d_skip.shape[-1]
    nch = a_re.shape[0]
    chunks = nch // halves
    g, p = h_re.shape[1], h_re.shape[2]
    tl = min(SSM_TILE, L)
    nl = L // tl
    blk0 = row_off // tl
    state_spec = pl.BlockSpec((1, nch, 1, LANES), lambda b, i: (b, 0, 0, 0))
    o, n_re, n_im = pl.pallas_call(
        functools.partial(_ssm_kernel, tl=tl, halves=halves, chunks=chunks),
        grid=(bt, nl),
        in_specs=[pl.BlockSpec((tl, width), lambda b, i: (blk0 + b * nl + i, 1)),
                  state_spec, state_spec,
                  _const_spec(a_re.shape), _const_spec(a_im.shape),
                  _const_spec(bcat.shape), _const_spec(ccat.shape),
                  _const_spec((1, width)), _const_spec(w_glu.shape)] + alias_specs,
        out_specs=[pl.BlockSpec((tl, width), lambda b, i: (blk0 + b * nl + i, 0)), state_spec, state_spec],
        out_shape=[jax.ShapeDtypeStruct((mix.shape[0], width), BF16),
                   jax.ShapeDtypeStruct((bt, nch, 1, LANES), F32), jax.ShapeDtypeStruct((bt, nch, 1, LANES), F32)],
        scratch_shapes=[pltpu.VMEM((2 * nch, tl, LANES), F32),
                        pltpu.VMEM((nch, 1, LANES), F32), pltpu.VMEM((nch, 1, LANES), F32),
                        pltpu.VMEM((nch, 2 * len(SCAN_SHIFTS) + 2, SUBLANES, LANES), F32)],
        input_output_aliases=alias_map,
        compiler_params=_cparams(("parallel", "arbitrary")),
        name="ssm_mixer",
    )(mix, h_re.astype(F32).reshape(bt, nch, 1, LANES), h_im.astype(F32).reshape(bt, nch, 1, LANES),
      a_re, a_im, bcat, ccat, d_skip.reshape(1, width), w_glu.astype(BF16), *alias_in)
    return o, n_re.reshape(bt, g, p), n_im.reshape(bt, g, p)


def _attn_init(m_ref, l_ref, acc_ref):
    m_ref[...] = jnp.full(m_ref.shape, -jnp.inf, F32)
    l_ref[...] = jnp.zeros(l_ref.shape, F32)
    acc_ref[...] = jnp.zeros(acc_ref.shape, F32)


def _attn_update(q_of, k_of, v_of, visible, heads, m_ref, l_ref, acc_ref):
    low_lanes = lax.broadcasted_iota(jnp.int32, (1, LANES), 1) < (LANES // 2)
    if visible is not None:
        visible = jnp.concatenate([visible, visible], axis=0)
    for h in range(heads):
        qh, kh, vh = q_of(h), k_of(h), v_of(h)
        zero = jnp.zeros_like(qh)
        qm = jnp.concatenate([jnp.where(low_lanes, qh, zero), jnp.where(low_lanes, zero, qh)], axis=0)
        s = jnp.dot(qm, kh, preferred_element_type=F32)
        if visible is not None:
            s = jnp.where(visible, s, -jnp.inf)
        m_old = m_ref[h]
        m_new = jnp.maximum(m_old, jnp.max(s, axis=1, keepdims=True))
        alpha = jnp.exp2(m_old - m_new)
        ps = [jnp.exp2(s[:, j * LANES:(j + 1) * LANES] - m_new) for j in range(s.shape[1] // LANES)]
        lsum = ps[0]
        for pj in ps[1:]:
            lsum = lsum + pj
        l_ref[h] = alpha * l_ref[h] + lsum
        p = jnp.concatenate([pj.astype(BF16) for pj in ps], axis=1)
        acc_ref[h] = alpha * acc_ref[h] + jnp.dot(p, vh, preferred_element_type=F32)
        m_ref[h] = m_new


def _attn_finalize(lam_ref, sw_ref, o_ref, l_ref, acc_ref, heads, lam_init):
    lp = lam_ref[...]
    lam = (jnp.exp(jnp.sum(lp[0:1] * lp[1:2], axis=1, keepdims=True))
           - jnp.exp(jnp.sum(lp[2:3] * lp[3:4], axis=1, keepdims=True)) + lam_init)
    rows = o_ref.shape[0]
    for h in range(heads):
        oh = acc_ref[h] * (1.0 / jnp.sum(l_ref[h], axis=1, keepdims=True))
        o = oh[:rows] - lam * oh[rows:]
        o = o * lax.rsqrt(jnp.mean(o * o, axis=1, keepdims=True) + LN_EPS) * sw_ref[...] * (1.0 - lam_init)
        o_ref[:, h * LANES:(h + 1) * LANES] = o.astype(o_ref.dtype)


def _head_cols(ref):
    lead = (0,) * (len(ref.shape) - 2)
    return lambda h: ref[(*lead, slice(None), slice(h * LANES, (h + 1) * LANES))].astype(BF16)


def _head_rows(ref):
    lead = (0,) * (len(ref.shape) - 2)
    return lambda h: ref[(*lead, slice(h * LANES, (h + 1) * LANES), slice(None))].astype(BF16)


def _attn_kernel(qi_ref, ki_ref, fl_ref, q_ref, k_ref, v_ref, lam_ref, sw_ref, o_ref, m_ref, l_ref, acc_ref,
                 *, tq, tk, heads, q_pos0, lk, lam_init):
    step = pl.program_id(1)
    qi, ki, fl = qi_ref[step], ki_ref[step], fl_ref[step]

    @pl.when(ki == 0)
    def _():
        _attn_init(m_ref, l_ref, acc_ref)

    def accumulate(masked, keys):
        visible = None
        if masked:
            q_pos = q_pos0 + qi * tq + lax.broadcasted_iota(jnp.int32, (tq, 1), 0)
            k_pos = ki * tk + lax.broadcasted_iota(jnp.int32, (1, keys), 1)
            visible = (k_pos < (q_pos // CHUNK + 1) * CHUNK) & (k_pos < lk)
        k_of = lambda h: k_ref[0, h * LANES:(h + 1) * LANES, 0:keys]
        v_of = lambda h: v_ref[0, 0:keys, h * LANES:(h + 1) * LANES]
        _attn_update(_head_cols(q_ref), k_of, v_of, visible, heads, m_ref, l_ref, acc_ref)

    @pl.when((fl & 1) == 0)
    def _():
        accumulate(False, tk)

    @pl.when(((fl & 1) != 0) & ((fl & 4) == 0))
    def _():
        accumulate(True, tk)

    @pl.when((fl & 4) != 0)
    def _():
        accumulate(True, tk // 2)

    @pl.when((fl & 2) != 0)
    def _():
        _attn_finalize(lam_ref, sw_ref, o_ref, l_ref, acc_ref, heads, lam_init)


def _decode_attn_kernel(q_ref, ck_ref, cv_ref, nk_ref, nv_ref, lam_ref, sw_ref, joint_ref, o_ref, m_ref, l_ref,
                        acc_ref, *, tq, heads, past, n_cache, lam_init):
    j = pl.program_id(1)

    @pl.when(j == 0)
    def _():
        _attn_init(m_ref, l_ref, acc_ref)

    @pl.when(j < n_cache)
    def _():
        n_keys = cv_ref.shape[2] // heads
        cached_v = lambda h: cv_ref[0, 0, pl.ds(h, n_keys, stride=heads), :].astype(BF16)
        _attn_update(_head_cols(q_ref), _head_rows(ck_ref), cached_v, None, heads, m_ref, l_ref, acc_ref)

    @pl.when(j == n_cache)
    def _():
        nk = nk_ref.shape[2]
        t = lax.broadcasted_iota(jnp.int32, (tq, 1), 0)
        i = lax.broadcasted_iota(jnp.int32, (1, nk), 1)
        visible = (past + i < ((past + t) // CHUNK + 1) * CHUNK) & (i < tq)
        _attn_update(_head_cols(q_ref), _head_rows(nk_ref), _head_cols(nv_ref), visible, heads, m_ref, l_ref, acc_ref)
        _attn_finalize(lam_ref, sw_ref, o_ref, l_ref, acc_ref, heads, lam_init)


def _attn_schedule(L, lk, tq, tk, q_pos0):
    qi, ki, fl = [], [], []
    for a in range(L // tq):
        first_end = ((q_pos0 + a * tq) // CHUNK + 1) * CHUNK
        last_end = min(((q_pos0 + a * tq + tq - 1) // CHUNK + 1) * CHUNK, lk)
        nk = -(-last_end // tk)
        for b in range(nk):
            full = (b + 1) * tk <= min(first_end, lk)
            half = not full and last_end <= b * tk + tk // 2
            qi.append(a); ki.append(b); fl.append((0 if full else 1) | (2 if b == nk - 1 else 0) | (4 if half else 0))
    return tuple(jnp.asarray(np.asarray(t, np.int32)) for t in (qi, ki, fl))


def _attention(q, kt_all, v_all, lam_p, subln_w, bt, L, lk, q_pos0, lam_init, total_rows):
    width = q.shape[-1]
    heads = width // LANES
    tq = min(ATTN_Q_TILE, L)
    tk = ATTN_TILE
    lk_pad = kt_all.shape[2]
    nq = L // tq
    qi, ki, fl = _attn_schedule(L, lk, tq, tk, q_pos0)
    grid_spec = pltpu.PrefetchScalarGridSpec(
        num_scalar_prefetch=3,
        grid=(bt, int(qi.shape[0])),
        in_specs=[pl.BlockSpec((1, tq, width), lambda b, s, qi, ki, fl: (b, qi[s], 0)),
                  pl.BlockSpec((1, width, tk), lambda b, s, qi, ki, fl: (b, 0, ki[s])),
                  pl.BlockSpec((1, tk, width), lambda b, s, qi, ki, fl: (b, ki[s], 0)),
                  pl.BlockSpec(lam_p.shape, lambda b, s, qi, ki, fl: (0, 0)),
                  pl.BlockSpec((1, LANES), lambda b, s, qi, ki, fl: (0, 0))],
        out_specs=pl.BlockSpec((tq, width), lambda b, s, qi, ki, fl: (b * nq + qi[s], 0)),
        scratch_shapes=[pltpu.VMEM((heads, 2 * tq, LANES), F32)] * 3)
    assert lk_pad % tk == 0 and L % tq == 0
    return pl.pallas_call(
        functools.partial(_attn_kernel, tq=tq, tk=tk, heads=heads, q_pos0=q_pos0, lk=lk, lam_init=lam_init),
        grid_spec=grid_spec,
        out_shape=jax.ShapeDtypeStruct((total_rows, width), BF16),
        compiler_params=_cparams(("parallel", "arbitrary")),
        name="diff_attention",
    )(qi, ki, fl, q.reshape(bt, L, width), kt_all, v_all, lam_p, subln_w.reshape(1, LANES))


def _decode_attention(q, cache_kt, cache_v, layer, new_k, new_v, lam_p, subln_w, bt, L, past, lam_init, row_off, joint):
    width = q.shape[-1]
    heads = width // LANES
    tk = min(DECODE_TILE, past)
    assert past % tk == 0 and past >= tk and L <= LANES and past % CHUNK == 0 and row_off % L == 0
    n_cache = past // tk
    new_kt = jnp.pad(jnp.swapaxes(new_k.reshape(bt, L, width), 1, 2), ((0, 0), (0, 0), (0, LANES - L)))
    new_v = jnp.pad(new_v.reshape(bt, L, width), ((0, 0), (0, LANES - L), (0, 0)))
    row_spec = pl.BlockSpec((1, L, width), lambda b, j: (b, 0, 0))
    scratch = pltpu.VMEM((heads, 2 * L, LANES), F32)
    return pl.pallas_call(
        functools.partial(_decode_attn_kernel, tq=L, heads=heads, past=past, n_cache=n_cache, lam_init=lam_init),
        grid=(bt, n_cache + 1),
        in_specs=[row_spec,
                  pl.BlockSpec((1, 1, width, tk), lambda b, j: (layer, b, 0, jnp.minimum(j, n_cache - 1))),
                  pl.BlockSpec((1, 1, tk * heads, LANES), lambda b, j: (layer, b, jnp.minimum(j, n_cache - 1), 0)),
                  pl.BlockSpec((1, width, LANES), lambda b, j: (b, 0, 0)),
                  pl.BlockSpec((1, LANES, width), lambda b, j: (b, 0, 0)),
                  pl.BlockSpec(lam_p.shape, lambda b, j: (0, 0)), pl.BlockSpec((1, LANES), lambda b, j: (0, 0)),
                  pl.BlockSpec(memory_space=pl.ANY)],
        out_specs=pl.BlockSpec((L, width), lambda b, j: (row_off // L + b, 0)),
        out_shape=jax.ShapeDtypeStruct(joint.shape, BF16),
        scratch_shapes=[scratch, scratch, scratch],
        input_output_aliases={7: 0},
        compiler_params=_cparams(("parallel", "arbitrary")),
        name="decode_attention",
    )(q.reshape(bt, L, width), cache_kt, cache_v, new_kt, new_v, lam_p, subln_w.reshape(1, LANES), joint)


def _merge_kernel(*refs, d_model, alpha, offs, n_x, head_tiles):
    g_ref, op_ref, os_ref, oc_ref, oa_ref, wb_ref, wo_ref, lg_ref, lb_ref, o_ref = refs[n_x:]
    merged = None
    for b, (o_b, (lo, hi)) in enumerate(zip((op_ref, os_ref, oc_ref, oa_ref), offs)):
        t = jnp.dot(o_b[...], wb_ref[lo:hi, :], preferred_element_type=F32)
        t = t * g_ref[:, b * d_model:(b + 1) * d_model].astype(F32)
        merged = t if merged is None else merged + t
    y = alpha * _token_tile(refs[:n_x], head_tiles)
    y = y + jnp.dot(merged.astype(BF16), wo_ref[...], preferred_element_type=F32)
    o_ref[...] = _layer_norm(y, lg_ref[...], lb_ref[...])


def _merge(x_parts, gates, o_pool, o_ssm, o_conv, o_attn, w_branch, w_out, ln_g, ln_b, alpha):
    m, d = gates.shape[0], x_parts[0].shape[1]
    widths = [o_pool.shape[1], o_ssm.shape[1], o_conv.shape[1], o_attn.shape[1]]
    ends = np.cumsum(widths)
    offs = tuple((int(e - w), int(e)) for e, w in zip(ends, widths))
    row = lambda cols: pl.BlockSpec((ROW_TILE, cols), lambda i: (i, 0))
    return pl.pallas_call(
        functools.partial(_merge_kernel, d_model=d, alpha=alpha, offs=offs, n_x=len(x_parts),
                          head_tiles=x_parts[0].shape[0] // ROW_TILE),
        grid=(m // ROW_TILE,),
        in_specs=_token_specs(x_parts) + [row(gates.shape[1])] + [row(w) for w in widths]
                 + [_const_spec(w_branch.shape), _const_spec(w_out.shape), _const_spec((1, d)), _const_spec((1, d))],
        out_specs=row(d),
        out_shape=jax.ShapeDtypeStruct((m, d), F32),
        compiler_params=_cparams(("parallel",)),
        name="merge_out_ln",
    )(*x_parts, gates, o_pool, o_ssm, o_conv, o_attn, w_branch, w_out, ln_g.reshape(1, d), ln_b.reshape(1, d))


def _ffn_kernel(x_ref, wu_ref, wd_ref, lg_ref, lb_ref, o_ref, *, alpha, chunk):
    x = x_ref[...]
    xb = x.astype(BF16)
    acc = alpha * x
    for c in range(wu_ref.shape[1] // chunk):
        hid = jnp.dot(xb, wu_ref[:, c * chunk:(c + 1) * chunk], preferred_element_type=F32)
        hid = jnp.square(jnp.maximum(hid, 0.0)).astype(BF16)
        acc = acc + jnp.dot(hid, wd_ref[c * chunk:(c + 1) * chunk, :], preferred_element_type=F32)
    o_ref[...] = _layer_norm(acc, lg_ref[...], lb_ref[...])


def _ffn(x, w_up, w_down, ln_g, ln_b, alpha, row_off=0, rows=None):
    d = x.shape[1]
    rows = x.shape[0] if rows is None else rows
    blk0 = row_off // ROW_TILE
    return pl.pallas_call(
        functools.partial(_ffn_kernel, alpha=alpha, chunk=1024),
        grid=(rows // ROW_TILE,),
        in_specs=[pl.BlockSpec((ROW_TILE, d), lambda i: (blk0 + i, 0)),
                  _const_spec(w_up.shape), _const_spec(w_down.shape), _const_spec((1, d)), _const_spec((1, d))],
        out_specs=pl.BlockSpec((ROW_TILE, d), lambda i: (i, 0)),
        out_shape=jax.ShapeDtypeStruct((rows, d), F32),
        compiler_params=_cparams(("parallel",)),
        name="ffn_ln",
    )(x, w_up, w_down, ln_g.reshape(1, d), ln_b.reshape(1, d))


def kernel(x_prompt, x_sample, cache_k, cache_v, state_ssm_re, state_ssm_im, state_conv, state_pool, w_in, pool_w, pool_scale, ssm_a_re, ssm_a_im, ssm_log_dt, ssm_b_re, ssm_b_im, ssm_c_re, ssm_c_im, ssm_d, ssm_w_glu, conv_w, conv_b, lambda_q1, lambda_k1, lambda_q2, lambda_k2, subln_w, w_branch, w_out, ln1_g, ln1_b, w_up, w_down, ln2_g, ln2_b):
    depth = w_in.shape[0]
    bp, lp, d = x_prompt.shape
    bs, ls, _ = x_sample.shape
    past = cache_k.shape[2]
    heads, qk_dim = cache_k.shape[3], cache_k.shape[5]
    width_qk = heads * 2 * qk_dim
    pool_width, ssm_width, conv_width = pool_scale.shape[1], ssm_d.shape[1], conv_w.shape[2]
    n_mix = pool_width + ssm_width + 3 * conv_width
    alpha = float((2 * depth) ** 0.25)
    paths = ((bp, lp, 0, 0), (bs, ls, past, bp * lp))

    x_parts = (x_prompt.reshape(bp * lp, d), x_sample.reshape(bs * ls, d))
    total_rows = bp * lp + bs * ls
    tables = (_rope_tables(jnp.arange(lp, dtype=jnp.int32)),
              _rope_tables(jnp.tile(past + jnp.arange(ls, dtype=jnp.int32), bs)))

    cache_kt = jnp.transpose(cache_k, (0, 1, 3, 4, 5, 2)).reshape(depth, bs, width_qk, past)
    cache_vr = cache_v.reshape(depth, bs, past * heads, 2 * qk_dim)

    outs = [[[] for _ in range(6)] for _ in paths]
    kv_stack = None
    for l in range(depth):
        wl = w_in[l].astype(BF16)
        mix = _project(x_parts, wl[:, :n_mix], F32, sigmoid=False)
        gates = _project(x_parts, wl[:, n_mix + 3 * width_qk:], BF16, sigmoid=True)
        ab_re, ab_im, bb_re, bb_im = _discretize(ssm_a_re[l], ssm_a_im[l], ssm_log_dt[l], ssm_b_re[l], ssm_b_im[l])
        ssm_w = _ssm_weights(ab_re, ab_im, bb_re, bb_im, ssm_c_re[l], ssm_c_im[l], halves=2)
        lam_p = jnp.stack([lambda_q1[l], lambda_k1[l], lambda_q2[l], lambda_k2[l]]).astype(F32)
        lam_init = 0.8 - 0.6 * math.exp(-0.3 * l)

        w_qkv = wl[:, n_mix:n_mix + 3 * width_qk]
        q_scale = float(qk_dim) ** -0.5 * math.log2(math.e)
        o_pool = o_conv = o_ssm = o_attn = None
        for pi, (bt, L, pos0, row_off) in enumerate(paths):
            if pi == 0:
                q, k_stack, v_stack, kbt, vb = _qkv_project_stacked(x_parts[0], w_qkv, tables[pi], bt, L, q_scale, l, depth,
                                                                    kv_stack)
                kv_stack = (k_stack, v_stack)
                hist_pool = jnp.zeros((bt, state_pool.shape[2], pool_width), F32)
                hist_conv = jnp.zeros((bt, state_conv.shape[2], conv_width), F32)
                h_re = h_im = jnp.zeros((bt,) + state_ssm_re.shape[2:], F32)
                o_attn = _attention(q, kbt, vb.reshape(bt, L, width_qk), lam_p, subln_w[l], bt, L, L, pos0, lam_init,
                                    total_rows)
                kv_new = ()
            else:
                q, kf, vf, kb, vb = _qkv_project(x_parts[-1], w_qkv, tables[pi], row_off if len(x_parts) == 1 else 0,
                                                 bt * L, q_scale)
                hist_pool, hist_conv, h_re, h_im = state_pool[l], state_conv[l], state_ssm_re[l], state_ssm_im[l]
                o_attn = _decode_attention(q, cache_kt, cache_vr, l, kb, vb, lam_p, subln_w[l], bt, L, past, lam_init,
                                           row_off, o_attn)
                kv_new = (kf.reshape(bt, L, heads, 2, qk_dim), vf.reshape(bt, L, heads, 2 * qk_dim))
            o_pool, o_conv, new_pool, new_conv = _local_mixers(
                mix, hist_pool, hist_conv, pool_w[l], pool_scale[l], conv_w[l], conv_b[l], row_off, bt, L, pos0,
                None if pi == 0 else (o_pool, o_conv))
            o_ssm, new_re, new_im = _ssm_mixer(mix, h_re, h_im, ssm_w, ssm_d[l], ssm_w_glu[l], row_off, bt, L,
                                               None if pi == 0 else (o_ssm,))
            for slot, val in zip(outs[pi], (new_re, new_im, new_conv, new_pool) + kv_new):
                slot.append(val)

        x = _merge(x_parts, gates, o_pool, o_ssm, o_conv, o_attn, w_branch[l].astype(BF16), w_out[l].astype(BF16),
                   ln1_g[l], ln1_b[l], alpha)
        ffn_w = (w_up[l].astype(BF16), w_down[l].astype(BF16), ln2_g[l], ln2_b[l], alpha)
        if l + 1 < depth:
            x_parts = (_ffn(x, *ffn_w),)

    y_prompt, y_sample = (_ffn(x, *ffn_w, row_off, bt * L).reshape(bt, L, d) for bt, L, _, row_off in paths)
    k_stack, v_stack = kv_stack
    k_prompt = jnp.transpose(k_stack.reshape(depth, bp, heads, 2, qk_dim, lp), (0, 1, 5, 2, 3, 4))
    v_prompt = v_stack.reshape(depth, bp, lp, heads, 2 * qk_dim)
    (p_re, p_im, p_conv, p_pool), (s_re, s_im, s_conv, s_pool, s_k, s_v) = (
        [jnp.stack(slot) for slot in path_outs if slot] for path_outs in outs)
    return (y_prompt, y_sample, k_prompt, v_prompt, p_re, p_im, p_conv, p_pool,
            s_k, s_v, s_re, s_im, s_conv, s_pool)
```
